```python
import math
import jax, jax.numpy as jnp
from jax import lax
import numpy as np

D_MODEL = 1024
BATCH = 16
SEQ = 2048
DEPTH = 1
DEC_BATCH = 8
DEC_SEQ = 2048
PAST_LEN = 128

GRID_W = 64
D_ATTN = D_MODEL // 2
HEAD_DIM = 64
N_Q_HEADS = D_ATTN // HEAD_DIM
N_KV_HEADS = 2
Q_PER_KV = N_Q_HEADS // N_KV_HEADS
KV_WIDTH = N_KV_HEADS * HEAD_DIM
Q_BLOCK = 128
ROPE_THETA = 10000.0
ROPE_SECTION = HEAD_DIM // 2
QK_EPS = 1e-6
D_LRU = D_MODEL - D_ATTN
N_LRU_HEADS = 8
LRU_HEAD_DIM = D_LRU // N_LRU_HEADS
CONV_WIDTH = 4
CONV_LEFT = CONV_WIDTH // 2
LRU_C = 8.0
D_MIX = D_ATTN + D_LRU
D_IN_PROJ = D_ATTN + 2 * KV_WIDTH + 2 * D_LRU
N_GROUPS = 4
EXPERTS_PER_GROUP = 4
N_EXPERTS = N_GROUPS * EXPERTS_PER_GROUP
TOP_K = 2
D_EXPERT = 256
LN_EPS = 1e-5
DEEPNORM_ALPHA = (2.0 * DEPTH) ** 0.25
DEEPNORM_BETA = (8.0 * DEPTH) ** -0.25

kernel_name = 'hymba_rglru_axial_gqa_hmoe_encoder'


def layer_norm(x, g, b):
    xf = x.astype(jnp.float32)
    mu = jnp.mean(xf, axis=-1, keepdims=True)
    xc = xf - mu
    var = jnp.mean(jnp.square(xc), axis=-1, keepdims=True)
    return (xc * lax.rsqrt(var + LN_EPS) * g + b).astype(x.dtype)


def rms_norm(x, g, eps):
    xf = x.astype(jnp.float32)
    return xf * lax.rsqrt(jnp.mean(jnp.square(xf), axis=-1, keepdims=True) + eps) * g


def axial_rope_tables(S):
    rows = S // GRID_W
    row = jnp.repeat(jnp.arange(rows, dtype=jnp.float32), GRID_W)
    col = jnp.tile(jnp.arange(GRID_W, dtype=jnp.float32), rows)
    half = ROPE_SECTION // 2
    inv_freq = ROPE_THETA ** (-jnp.arange(half, dtype=jnp.float32) / half)
    ang_r = row[:, None] * inv_freq
    ang_c = col[:, None] * inv_freq
    return jnp.cos(ang_r), jnp.sin(ang_r), jnp.cos(ang_c), jnp.sin(ang_c)


def rotate_section(x, cos, sin):
    x1, x2 = jnp.split(x, 2, axis=-1)
    c = cos[None, :, None, :]
    s = sin[None, :, None, :]
    return jnp.concatenate([x1 * c - x2 * s, x1 * s + x2 * c], axis=-1)


def apply_axial_rope(x, cos_r, sin_r, cos_c, sin_c):
    x_row, x_col = jnp.split(x, 2, axis=-1)
    return jnp.concatenate([rotate_section(x_row, cos_r, sin_r),
                            rotate_section(x_col, cos_c, sin_c)], axis=-1)


def bidir_block_attention(q, k, v):
    B, S = q.shape[0], q.shape[1]
    n_blk = S // Q_BLOCK
    qb = q.reshape(B, n_blk, Q_BLOCK, N_KV_HEADS, Q_PER_KV, HEAD_DIM).transpose(1, 0, 3, 4, 2, 5)
    kt = k.transpose(0, 2, 1, 3)
    vt = v.transpose(0, 2, 1, 3)
    scale = HEAD_DIM ** -0.5

    def one_block(q_blk):
        s = jnp.einsum('bkgqd,bksd->bkgqs', q_blk, kt, preferred_element_type=jnp.float32) * scale
        p = jax.nn.softmax(s, axis=-1)
        return jnp.einsum('bkgqs,bksd->bkgqd', p.astype(vt.dtype), vt)

    o = lax.map(one_block, qb)
    return o.transpose(1, 0, 4, 2, 3, 5).reshape(B, S, N_Q_HEADS * HEAD_DIM)


def centred_depthwise_conv(x, w, b):
    S = x.shape[1]
    xp = jnp.pad(x, ((0, 0), (CONV_LEFT, CONV_WIDTH - 1 - CONV_LEFT), (0, 0)))
    out = b
    for tap in range(CONV_WIDTH):
        out = out + xp[:, tap:tap + S, :] * w[tap]
    return out


def rg_lru_scan(xc, wa, ba, wx, bx, lam, reverse):
    B, S, _ = xc.shape
    xf = xc.astype(jnp.float32)
    xh = xf.reshape(B, S, N_LRU_HEADS, LRU_HEAD_DIM)
    r = jax.nn.sigmoid(jnp.einsum('bshi,hij->bshj', xh, wa) + ba).reshape(B, S, D_LRU)
    i = jax.nn.sigmoid(jnp.einsum('bshi,hij->bshj', xh, wx) + bx).reshape(B, S, D_LRU)
    log_a = -LRU_C * r * jax.nn.softplus(-lam.astype(jnp.float32))
    a = jnp.exp(log_a)
    u = jnp.sqrt(-jnp.expm1(2.0 * log_a)) * (i * xf)
    if reverse:
        a = jnp.flip(a, axis=1)
        u = jnp.flip(u, axis=1)

    def combine(left, right):
        a_l, u_l = left
        a_r, u_r = right
        return a_l * a_r, a_r * u_l + u_r

    _, h = lax.associative_scan(combine, (a, u), axis=1)
    if reverse:
        h = jnp.flip(h, axis=1)
    return h


def mixer(h, w_in, conv_w, conv_b, lru_wa, lru_ba, lru_wx, lru_bx, lru_lambda,
          q_norm_g, k_norm_g, attn_out_g, lru_out_g, w_out):
    B, S, _ = h.shape
    proj = h @ w_in
    cuts = [D_ATTN, D_ATTN + KV_WIDTH, D_ATTN + 2 * KV_WIDTH, D_ATTN + 2 * KV_WIDTH + D_LRU]
    q, k, v, x_lru, y_lru = jnp.split(proj, cuts, axis=-1)
    cos_r, sin_r, cos_c, sin_c = axial_rope_tables(S)
    q = rms_norm(q.reshape(B, S, N_Q_HEADS, HEAD_DIM), q_norm_g, QK_EPS)
    k = rms_norm(k.reshape(B, S, N_KV_HEADS, HEAD_DIM), k_norm_g, QK_EPS)
    q = apply_axial_rope(q, cos_r, sin_r, cos_c, sin_c).astype(h.dtype)
    k = apply_axial_rope(k, cos_r, sin_r, cos_c, sin_c).astype(h.dtype)
    v = v.reshape(B, S, N_KV_HEADS, HEAD_DIM)
    attn = bidir_block_attention(q, k, v)
    xc = centred_depthwise_conv(x_lru, conv_w, conv_b)
    h_fwd = rg_lru_scan(xc, lru_wa[0], lru_ba[0], lru_wx[0], lru_bx[0], lru_lambda[0], False)
    h_bwd = rg_lru_scan(xc, lru_wa[1], lru_ba[1], lru_wx[1], lru_bx[1], lru_lambda[1], True)
    lru = (h_fwd + h_bwd) * jax.nn.gelu(y_lru.astype(jnp.float32))
    merged = jnp.concatenate([rms_norm(attn, attn_out_g, LN_EPS),
                              rms_norm(lru, lru_out_g, LN_EPS)], axis=-1).astype(h.dtype)
    return merged @ w_out


def hier_moe(h, router_wg, router_bg, router_we, router_be, exp_w1, exp_w3, exp_w2):
    B, S, D = h.shape
    xt = h.reshape(B * S, D)
    g_logits = (xt @ router_wg).astype(jnp.float32) + router_bg
    g_prob = jax.nn.softmax(g_logits, axis=-1)
    g_onehot = jax.nn.one_hot(jnp.argmax(g_logits, axis=-1), N_GROUPS, dtype=jnp.float32)
    g_sel_prob = jnp.sum(g_prob * g_onehot, axis=-1)
    e_logits = jnp.einsum('td,dge->tge', xt, router_we).astype(jnp.float32) + router_be
    e_logits_sel = jnp.einsum('tge,tg->te', e_logits, g_onehot)
    top_v, top_i = lax.top_k(e_logits_sel, TOP_K)
    top_w = jax.nn.softmax(top_v, axis=-1) * g_sel_prob[:, None]
    g_idx = jnp.argmax(g_onehot, axis=-1)
    expert_id = g_idx[:, None] * EXPERTS_PER_GROUP + top_i
    gate = jnp.sum(jax.nn.one_hot(expert_id, N_EXPERTS, dtype=jnp.float32) * top_w[..., None], axis=1)
    y = jnp.zeros((B * S, D), jnp.float32)
    for e in range(N_EXPERTS):
        hid = jax.nn.silu(xt @ exp_w1[e]) * (xt @ exp_w3[e])
        y = y + gate[:, e:e + 1] * (hid @ exp_w2[e]).astype(jnp.float32)
    return y.reshape(B, S, D).astype(h.dtype)


def trunk(x, ln_in_g, ln_in_b, w_in, conv_w, conv_b, lru_wa, lru_ba, lru_wx, lru_bx,
          lru_lambda, q_norm_g, k_norm_g, attn_out_g, lru_out_g, w_out, ln1_g, ln1_b,
          router_wg, router_bg, router_we, router_be, exp_w1, exp_w3, exp_w2, ln2_g, ln2_b):
    h = layer_norm(x, ln_in_g, ln_in_b)
    for l in range(DEPTH):
        mix = mixer(h, w_in[l], conv_w[l], conv_b[l], lru_wa[l], lru_ba[l], lru_wx[l], lru_bx[l],
                    lru_lambda[l], q_norm_g[l], k_norm_g[l], attn_out_g[l], lru_out_g[l], w_out[l])
        h = layer_norm(DEEPNORM_ALPHA * h + mix, ln1_g[l], ln1_b[l])
        ffn = hier_moe(h, router_wg[l], router_bg[l], router_we[l], router_be[l],
                       exp_w1[l], exp_w3[l], exp_w2[l])
        h = layer_norm(DEEPNORM_ALPHA * h + ffn, ln2_g[l], ln2_b[l])
    return h


def setup_inputs(seed: int = 0) -> dict:
    key = jax.random.key(seed)
    ks = jax.random.split(key, 32)
    f32 = jnp.float32
    nrm = lambda k, shape: jax.random.normal(k, shape, f32)
    u = jax.random.uniform(ks[11], (DEPTH, 2, D_LRU), f32, minval=0.9, maxval=0.999)
    a0 = u ** (1.0 / LRU_C)
    lam = jnp.log(a0) - jnp.log1p(-a0)
    return {
        'x_prompt': nrm(ks[0], (BATCH, SEQ, D_MODEL)),
        'x_sample': nrm(ks[1], (DEC_BATCH, DEC_SEQ, D_MODEL)),
        'ln_in_g': 1.0 + 0.05 * nrm(ks[2], (D_MODEL,)),
        'ln_in_b': 0.02 * nrm(ks[3], (D_MODEL,)),
        'w_in': nrm(ks[4], (DEPTH, D_MODEL, D_IN_PROJ)) * D_MODEL ** -0.5,
        'conv_w': nrm(ks[5], (DEPTH, CONV_WIDTH, D_LRU)) * CONV_WIDTH ** -0.5,
        'conv_b': 0.02 * nrm(ks[6], (DEPTH, D_LRU)),
        'lru_wa': nrm(ks[7], (DEPTH, 2, N_LRU_HEADS, LRU_HEAD_DIM, LRU_HEAD_DIM)) * LRU_HEAD_DIM ** -0.5,
        'lru_ba': 0.02 * nrm(ks[8], (DEPTH, 2, N_LRU_HEADS, LRU_HEAD_DIM)),
        'lru_wx': nrm(ks[9], (DEPTH, 2, N_LRU_HEADS, LRU_HEAD_DIM, LRU_HEAD_DIM)) * LRU_HEAD_DIM ** -0.5,
        'lru_bx': 0.02 * nrm(ks[10], (DEPTH, 2, N_LRU_HEADS, LRU_HEAD_DIM)),
        'lru_lambda': lam,
        'q_norm_g': 1.0 + 0.05 * nrm(ks[12], (DEPTH, HEAD_DIM)),
        'k_norm_g': 1.0 + 0.05 * nrm(ks[13], (DEPTH, HEAD_DIM)),
        'attn_out_g': 1.0 + 0.05 * nrm(ks[14], (DEPTH, D_ATTN)),
        'lru_out_g': 1.0 + 0.05 * nrm(ks[15], (DEPTH, D_LRU)),
        'w_out': nrm(ks[16], (DEPTH, D_MIX, D_MODEL)) * D_MIX ** -0.5 * DEEPNORM_BETA,
        'ln1_g': 1.0 + 0.05 * nrm(ks[17], (DEPTH, D_MODEL)),
        'ln1_b': 0.02 * nrm(ks[18], (DEPTH, D_MODEL)),
        'router_wg': nrm(ks[19], (DEPTH, D_MODEL, N_GROUPS)) * D_MODEL ** -0.5,
        'router_bg': 0.01 * nrm(ks[20], (DEPTH, N_GROUPS)),
        'router_we': nrm(ks[21], (DEPTH, D_MODEL, N_GROUPS, EXPERTS_PER_GROUP)) * D_MODEL ** -0.5,
        'router_be': 0.01 * nrm(ks[22], (DEPTH, N_GROUPS, EXPERTS_PER_GROUP)),
        'exp_w1': nrm(ks[23], (DEPTH, N_EXPERTS, D_MODEL, D_EXPERT)) * D_MODEL ** -0.5,
        'exp_w3': nrm(ks[24], (DEPTH, N_EXPERTS, D_MODEL, D_EXPERT)) * D_MODEL ** -0.5,
        'exp_w2': nrm(ks[25], (DEPTH, N_EXPERTS, D_EXPERT, D_MODEL)) * D_EXPERT ** -0.5 * DEEPNORM_BETA,
        'ln2_g': 1.0 + 0.05 * nrm(ks[26], (DEPTH, D_MODEL)),
        'ln2_b': 0.02 * nrm(ks[27], (DEPTH, D_MODEL)),
    }


def reference(x_prompt, x_sample, ln_in_g, ln_in_b, w_in, conv_w, conv_b, lru_wa, lru_ba,
              lru_wx, lru_bx, lru_lambda, q_norm_g, k_norm_g, attn_out_g, lru_out_g, w_out,
              ln1_g, ln1_b, router_wg, router_bg, router_we, router_be, exp_w1, exp_w3,
              exp_w2, ln2_g, ln2_b):
    weights = (ln_in_g, ln_in_b, w_in, conv_w, conv_b, lru_wa, lru_ba, lru_wx, lru_bx,
               lru_lambda, q_norm_g, k_norm_g, attn_out_g, lru_out_g, w_out, ln1_g, ln1_b,
               router_wg, router_bg, router_we, router_be, exp_w1, exp_w3, exp_w2, ln2_g, ln2_b)
    y_prompt = trunk(x_prompt, *weights)
    y_sample = trunk(x_sample, *weights)
    return (y_prompt, y_sample)
```

```python
import functools
import math

import jax
import jax.numpy as jnp
from jax import lax
from jax.experimental import pallas as pl
from jax.experimental.pallas import tpu as pltpu

F32 = jnp.float32
BF16 = jnp.bfloat16

D_MODEL = 1024
GRID_W = 64
D_ATTN = 512
HEAD_DIM = 64
N_Q_HEADS = 8
N_KV_HEADS = 2
Q_PER_KV = 4
KV_WIDTH = 128
ROPE_THETA = 10000.0
ROPE_SECTION = 32
QK_EPS = 1e-6
D_LRU = 512
N_LRU_HEADS = 8
LRU_HEAD_DIM = 64
CONV_WIDTH = 4
LRU_C = 8.0
N_GROUPS = 4
EXPERTS_PER_GROUP = 4
N_EXPERTS = 16
D_EXPERT = 256
LN_EPS = 1e-5
DEPTH = 1
DEEPNORM_ALPHA = (2.0 * DEPTH) ** 0.25

LANES = 128
SUBLANES = 8
VMEM_LIMIT = 48 * 1024 * 1024

TM_PROJ = 512
TQ = 256
LRU_CH = 256
LRU_ROWS = 256
TM_MOE = 1024
ROUTER_PAD = LANES
EXPERT_LANE0 = N_GROUPS


def _layer_norm(x, g, b):
    mu = jnp.mean(x, axis=-1, keepdims=True)
    xc = x - mu
    var = jnp.mean(xc * xc, axis=-1, keepdims=True)
    return xc * lax.rsqrt(var + LN_EPS) * g + b


def _rms_norm_rows(x, g, eps):
    return x * lax.rsqrt(jnp.mean(x * x, axis=-1, keepdims=True) + eps) * g


def _head_norm_rope(xf, seg, g, cos, sin):
    ms = jnp.dot((xf * xf).astype(BF16), seg, preferred_element_type=F32)
    xn = xf * lax.rsqrt(ms + QK_EPS) * g
    n = xn.shape[1]
    half = ROPE_SECTION // 2
    up = pltpu.roll(xn, n - half, axis=1)
    dn = pltpu.roll(xn, half, axis=1)
    lane = lax.broadcasted_iota(jnp.int32, xn.shape, 1)
    first = (lane % ROPE_SECTION) < half
    partner = jnp.where(first, up, dn)
    return xn * cos + partner * sin


def _in_proj_kernel(x_ref, g_ref, b_ref, wq_ref, wkv_ref, wx_ref, wy_ref, seg_ref,
                    qg_ref, kg_ref, cos_ref, sin_ref,
                    q_ref, k_ref, v_ref, xl_ref, yl_ref):
    h = _layer_norm(x_ref[...], g_ref[...], b_ref[...])
    hb = h.astype(BF16)
    cos = cos_ref[...]
    sin = sin_ref[...]
    reps = D_ATTN // KV_WIDTH

    qf = jnp.dot(hb, wq_ref[...], preferred_element_type=F32)
    q = _head_norm_rope(qf, seg_ref[...], qg_ref[...],
                        jnp.concatenate([cos] * reps, axis=1),
                        jnp.concatenate([sin] * reps, axis=1))
    q_ref[...] = (q * (HEAD_DIM ** -0.5)).astype(BF16)

    kvf = jnp.dot(hb, wkv_ref[...], preferred_element_type=F32)
    k = _head_norm_rope(kvf[:, :KV_WIDTH], seg_ref[:KV_WIDTH, :KV_WIDTH], kg_ref[...], cos, sin)
    kb = k.astype(BF16)
    vb = kvf[:, KV_WIDTH:].astype(BF16)
    for j in range(N_KV_HEADS):
        k_ref[j] = kb[:, j * HEAD_DIM:(j + 1) * HEAD_DIM]
        v_ref[j] = vb[:, j * HEAD_DIM:(j + 1) * HEAD_DIM]

    xl_ref[...] = jnp.dot(hb, wx_ref[...], preferred_element_type=F32)
    yl_ref[...] = jnp.dot(hb, wy_ref[...], preferred_element_type=F32)


def _in_proj(x2, B, S, ln_g, ln_b, wq, wkv, wx, wy, seg, qg, kg, cos_t, sin_t):
    T = B * S
    tm = TM_PROJ
    nt = S // tm
    const = lambda i: (0, 0)
    return pl.pallas_call(
        _in_proj_kernel,
        grid=(T // tm,),
        in_specs=[
            pl.BlockSpec((tm, D_MODEL), lambda i: (i, 0)),
            pl.BlockSpec((1, D_MODEL), const),
            pl.BlockSpec((1, D_MODEL), const),
            pl.BlockSpec((D_MODEL, D_ATTN), const),
            pl.BlockSpec((D_MODEL, 2 * KV_WIDTH), const),
            pl.BlockSpec((D_MODEL, D_LRU), const),
            pl.BlockSpec((D_MODEL, D_LRU), const),
            pl.BlockSpec((D_ATTN, D_ATTN), const),
            pl.BlockSpec((1, D_ATTN), const),
            pl.BlockSpec((1, KV_WIDTH), const),
            pl.BlockSpec((tm, KV_WIDTH), lambda i: (i % nt, 0)),
            pl.BlockSpec((tm, KV_WIDTH), lambda i: (i % nt, 0)),
        ],
        out_specs=[
            pl.BlockSpec((tm, D_ATTN), lambda i: (i, 0)),
            pl.BlockSpec((None, N_KV_HEADS, tm, HEAD_DIM), lambda i: (i // nt, 0, i % nt, 0)),
            pl.BlockSpec((None, N_KV_HEADS, tm, HEAD_DIM), lambda i: (i // nt, 0, i % nt, 0)),
            pl.BlockSpec((tm, D_LRU), lambda i: (i, 0)),
            pl.BlockSpec((tm, D_LRU), lambda i: (i, 0)),
        ],
        out_shape=[
            jax.ShapeDtypeStruct((T, D_ATTN), BF16),
            jax.ShapeDtypeStruct((B, N_KV_HEADS, S, HEAD_DIM), BF16),
            jax.ShapeDtypeStruct((B, N_KV_HEADS, S, HEAD_DIM), BF16),
            jax.ShapeDtypeStruct((T, D_LRU), F32),
            jax.ShapeDtypeStruct((T, D_LRU), F32),
        ],
        compiler_params=pltpu.CompilerParams(
            dimension_semantics=("parallel",), vmem_limit_bytes=VMEM_LIMIT),
        name="in_proj",
    )(x2, ln_g, ln_b, wq, wkv, wx, wy, seg, qg, kg, cos_t, sin_t)


def _attn_kernel(q_ref, k_ref, v_ref, o_ref):
    k = k_ref[...]
    v = v_ref[...]
    outs = []
    for h in range(Q_PER_KV):
        qh = q_ref[:, h * HEAD_DIM:(h + 1) * HEAD_DIM]
        s = lax.dot_general(qh, k, (((1,), (1,)), ((), ())), preferred_element_type=F32)
        m = jnp.max(s, axis=1, keepdims=True)
        p = jnp.exp(s - m)
        l = jnp.sum(p, axis=1, keepdims=True)
        o = jnp.dot(p.astype(BF16), v, preferred_element_type=F32)
        outs.append(o / l)
    o_ref[...] = jnp.concatenate(outs, axis=1)


def _attention(q, k, v, B, S):
    T = B * S
    nq = S // TQ
    width = Q_PER_KV * HEAD_DIM
    return pl.pallas_call(
        _attn_kernel,
        grid=(B, N_KV_HEADS, nq),
        in_specs=[
            pl.BlockSpec((TQ, width), lambda b, j, t: (b * nq + t, j)),
            pl.BlockSpec((None, None, S, HEAD_DIM), lambda b, j, t: (b, j, 0, 0)),
            pl.BlockSpec((None, None, S, HEAD_DIM), lambda b, j, t: (b, j, 0, 0)),
        ],
        out_specs=pl.BlockSpec((TQ, width), lambda b, j, t: (b * nq + t, j)),
        out_shape=jax.ShapeDtypeStruct((T, D_ATTN), F32),
        compiler_params=pltpu.CompilerParams(
            dimension_semantics=("parallel", "parallel", "parallel"),
            vmem_limit_bytes=VMEM_LIMIT),
        name="attention",
    )(q, k, v)


def _scan_rows(a, u, reverse):
    row = lax.broadcasted_iota(jnp.int32, a.shape, 0)
    for d in (1, 2, 4):
        if reverse:
            shift, valid = SUBLANES - d, row < SUBLANES - d
        else:
            shift, valid = d, row >= d
        a_sh = jnp.where(valid, pltpu.roll(a, shift, axis=0), 1.0)
        u_sh = jnp.where(valid, pltpu.roll(u, shift, axis=0), 0.0)
        u = a * u_sh + u
        a = a * a_sh
    return a, u


def _gelu_tanh(x):
    c = math.sqrt(2.0 / math.pi)
    return 0.5 * x * (1.0 + jnp.tanh(c * (x + 0.044715 * (x * x * x))))


def _rg_lru_kernel(x_ref, y_ref, cw_ref, cb_ref, wg_ref, bg_ref, lam_ref, o_ref,
                   af_ref, uf_ref, ab_ref, ub_ref):
    S, C = x_ref.shape
    R = LRU_ROWS
    nblk = S // R
    lam = lam_ref[...]
    neg_c_softplus = -LRU_C * jnp.logaddexp(-lam, 0.0)
    cw = cw_ref[...]
    cb = cb_ref[...]
    bg = bg_ref[...]

    def gate_block(r, carry):
        start = pl.multiple_of(r * R, R)
        cur = x_ref[pl.ds(start, R), :]
        prev = x_ref[pl.ds(pl.multiple_of(jnp.maximum(start - SUBLANES, 0), SUBLANES), SUBLANES), :]
        prev = jnp.where(r > 0, prev, 0.0)
        nxt = x_ref[pl.ds(pl.multiple_of(jnp.minimum(start + R, S - SUBLANES), SUBLANES), SUBLANES), :]
        nxt = jnp.where(r < nblk - 1, nxt, 0.0)
        ext = jnp.concatenate([prev, cur, nxt], axis=0)
        xc = cb
        for tap in range(CONV_WIDTH):
            off = SUBLANES + tap - CONV_WIDTH // 2
            xc = xc + ext[off:off + R, :] * cw[tap:tap + 1, :]
        z = jnp.dot(xc.astype(BF16), wg_ref[...], preferred_element_type=F32) + bg
        for d, (a_ref, u_ref) in enumerate(((af_ref, uf_ref), (ab_ref, ub_ref))):
            rg = jax.nn.sigmoid(z[:, (2 * d) * C:(2 * d + 1) * C])
            ig = jax.nn.sigmoid(z[:, (2 * d + 1) * C:(2 * d + 2) * C])
            a = jnp.exp(neg_c_softplus[d:d + 1, :] * rg)
            u = jnp.sqrt(1.0 - a * a) * (ig * xc)
            a_ref[pl.ds(start, R), :] = a
            u_ref[pl.ds(start, R), :] = u
        return carry

    lax.fori_loop(0, nblk, gate_block, 0)

    nchunk = S // SUBLANES

    def scan_chunk(c, carry):
        hf, hb = carry
        sf = pl.multiple_of(c * SUBLANES, SUBLANES)
        a_in, u_in = _scan_rows(af_ref[pl.ds(sf, SUBLANES), :], uf_ref[pl.ds(sf, SUBLANES), :], False)
        h_cur = a_in * hf + u_in
        uf_ref[pl.ds(sf, SUBLANES), :] = h_cur
        hf = jnp.broadcast_to(h_cur[SUBLANES - 1:SUBLANES, :], h_cur.shape)
        sb = pl.multiple_of((nchunk - 1 - c) * SUBLANES, SUBLANES)
        a_in, u_in = _scan_rows(ab_ref[pl.ds(sb, SUBLANES), :], ub_ref[pl.ds(sb, SUBLANES), :], True)
        h_cur = a_in * hb + u_in
        ub_ref[pl.ds(sb, SUBLANES), :] = h_cur
        hb = jnp.broadcast_to(h_cur[0:1, :], h_cur.shape)
        return hf, hb

    zero = jnp.zeros((SUBLANES, C), F32)
    lax.fori_loop(0, nchunk, scan_chunk, (zero, zero), unroll=4)

    def out_block(r, carry):
        start = pl.multiple_of(r * R, R)
        rows = pl.ds(start, R)
        o_ref[rows, :] = (uf_ref[rows, :] + ub_ref[rows, :]) * _gelu_tanh(y_ref[rows, :])
        return carry

    lax.fori_loop(0, nblk, out_block, 0)


def _rg_lru(xl, yl, B, S, conv_w, conv_b, wg, bg, lam):
    C = LRU_CH
    ncg = D_LRU // C
    xl3 = xl.reshape(B, S, D_LRU)
    yl3 = yl.reshape(B, S, D_LRU)
    out = pl.pallas_call(
        _rg_lru_kernel,
        grid=(B, ncg),
        in_specs=[
            pl.BlockSpec((None, S, C), lambda b, c: (b, 0, c)),
            pl.BlockSpec((None, S, C), lambda b, c: (b, 0, c)),
            pl.BlockSpec((CONV_WIDTH, C), lambda b, c: (0, c)),
            pl.BlockSpec((1, C), lambda b, c: (0, c)),
            pl.BlockSpec((None, C, 4 * C), lambda b, c: (c, 0, 0)),
            pl.BlockSpec((None, 1, 4 * C), lambda b, c: (c, 0, 0)),
            pl.BlockSpec((2, C), lambda b, c: (0, c)),
        ],
        out_specs=pl.BlockSpec((None, S, C), lambda b, c: (b, 0, c)),
        out_shape=jax.ShapeDtypeStruct((B, S, D_LRU), F32),
        scratch_shapes=[pltpu.VMEM((S, C), F32)] * 4,
        compiler_params=pltpu.CompilerParams(
            dimension_semantics=("parallel", "parallel"), vmem_limit_bytes=VMEM_LIMIT),
        name="rg_lru",
    )(xl3, yl3, conv_w, conv_b, wg, bg, lam)
    return out.reshape(B * S, D_LRU)


def _router_gates(logits):
    lane = lax.broadcasted_iota(jnp.int32, logits.shape, 1)
    neg = -jnp.inf
    big = jnp.int32(2 * LANES)
    gl = jnp.where(lane < N_GROUPS, logits, neg)
    gmax = jnp.max(gl, axis=1, keepdims=True)
    gidx = jnp.min(jnp.where(gl == gmax, lane, big), axis=1, keepdims=True)
    g_sel_prob = 1.0 / jnp.sum(jnp.exp(gl - gmax), axis=1, keepdims=True)
    lo = EXPERT_LANE0 + EXPERTS_PER_GROUP * gidx
    el = jnp.where((lane >= lo) & (lane < lo + EXPERTS_PER_GROUP), logits, neg)
    v1 = jnp.max(el, axis=1, keepdims=True)
    i1 = jnp.min(jnp.where(el == v1, lane, big), axis=1, keepdims=True)
    el2 = jnp.where(lane == i1, neg, el)
    v2 = jnp.max(el2, axis=1, keepdims=True)
    i2 = jnp.min(jnp.where(el2 == v2, lane, big), axis=1, keepdims=True)
    e2 = jnp.exp(v2 - v1)
    w1 = g_sel_prob / (1.0 + e2)
    w2 = g_sel_prob * e2 / (1.0 + e2)
    return jnp.where(lane == i1, w1, jnp.where(lane == i2, w2, 0.0))


def _out_proj_kernel(x_ref, attn_ref, lru_ref, lng_ref, lnb_ref, ag_ref, lg_ref, wo_ref,
                     g1_ref, b1_ref, wrh_ref, wrl_ref, br_ref, h1_ref, gate_ref):
    h = _layer_norm(x_ref[...], lng_ref[...], lnb_ref[...])
    merged = jnp.concatenate(
        [_rms_norm_rows(attn_ref[...], ag_ref[...], LN_EPS),
         _rms_norm_rows(lru_ref[...], lg_ref[...], LN_EPS)], axis=1).astype(BF16)
    mix = jnp.dot(merged, wo_ref[...], preferred_element_type=F32)
    h1 = _layer_norm(DEEPNORM_ALPHA * h + mix, g1_ref[...], b1_ref[...])
    h1_ref[...] = h1
    hi = h1.astype(BF16)
    lo = (h1 - hi.astype(F32)).astype(BF16)
    wrh = wrh_ref[...]
    logits = (jnp.dot(hi, wrh, preferred_element_type=F32)
              + jnp.dot(lo, wrh, preferred_element_type=F32)
              + jnp.dot(hi, wrl_ref[...], preferred_element_type=F32)) + br_ref[...]
    gate_ref[...] = _router_gates(logits)


def _out_proj(x2, attn, lru, ln_g, ln_b, ag, lg, wo, g1, b1, wrh, wrl, br):
    T = x2.shape[0]
    tm = TM_PROJ
    const = lambda i: (0, 0)
    row = lambda i: (i, 0)
    return pl.pallas_call(
        _out_proj_kernel,
        grid=(T // tm,),
        in_specs=[
            pl.BlockSpec((tm, D_MODEL), row),
            pl.BlockSpec((tm, D_ATTN), row),
            pl.BlockSpec((tm, D_LRU), row),
            pl.BlockSpec((1, D_MODEL), const),
            pl.BlockSpec((1, D_MODEL), const),
            pl.BlockSpec((1, D_ATTN), const),
            pl.BlockSpec((1, D_LRU), const),
            pl.BlockSpec((D_MODEL, D_MODEL), const),
            pl.BlockSpec((1, D_MODEL), const),
            pl.BlockSpec((1, D_MODEL), const),
            pl.BlockSpec((D_MODEL, ROUTER_PAD), const),
            pl.BlockSpec((D_MODEL, ROUTER_PAD), const),
            pl.BlockSpec((1, ROUTER_PAD), const),
        ],
        out_specs=[pl.BlockSpec((tm, D_MODEL), row), pl.BlockSpec((tm, ROUTER_PAD), row)],
        out_shape=[jax.ShapeDtypeStruct((T, D_MODEL), F32),
                   jax.ShapeDtypeStruct((T, ROUTER_PAD), F32)],
        compiler_params=pltpu.CompilerParams(
            dimension_semantics=("parallel",), vmem_limit_bytes=VMEM_LIMIT),
        name="out_proj",
    )(x2, attn, lru, ln_g, ln_b, ag, lg, wo, g1, b1, wrh, wrl, br)


def _moe_kernel(h1_ref, gate_ref, w13_ref, w2_ref, g2_ref, b2_ref, o_ref, xb_ref, acc_ref):
    e = pl.program_id(1)

    @pl.when(e == 0)
    def _():
        xb_ref[...] = h1_ref[...].astype(BF16)
        acc_ref[...] = jnp.zeros_like(acc_ref)

    gate = gate_ref[...]
    lane = lax.broadcasted_iota(jnp.int32, gate.shape, 1)
    ge = jnp.sum(jnp.where(lane == EXPERT_LANE0 + e, gate, 0.0), axis=1, keepdims=True)
    up = jnp.dot(xb_ref[...], w13_ref[...], preferred_element_type=F32)
    u1 = up[:, :D_EXPERT]
    hid = (u1 * jax.nn.sigmoid(u1)) * up[:, D_EXPERT:]
    acc_ref[...] += ge * jnp.dot(hid.astype(BF16), w2_ref[...], preferred_element_type=F32)

    @pl.when(e == N_EXPERTS - 1)
    def _():
        o_ref[...] = _layer_norm(DEEPNORM_ALPHA * h1_ref[...] + acc_ref[...],
                                 g2_ref[...], b2_ref[...])


def _moe(h1, gate, w13, w2, g2, b2):
    T = h1.shape[0]
    tm = TM_MOE
    return pl.pallas_call(
        _moe_kernel,
        grid=(T // tm, N_EXPERTS),
        in_specs=[
            pl.BlockSpec((tm, D_MODEL), lambda i, e: (i, 0)),
            pl.BlockSpec((tm, ROUTER_PAD), lambda i, e: (i, 0)),
            pl.BlockSpec((None, D_MODEL, 2 * D_EXPERT), lambda i, e: (e, 0, 0)),
            pl.BlockSpec((None, D_EXPERT, D_MODEL), lambda i, e: (e, 0, 0)),
            pl.BlockSpec((1, D_MODEL), lambda i, e: (0, 0)),
            pl.BlockSpec((1, D_MODEL), lambda i, e: (0, 0)),
        ],
        out_specs=pl.BlockSpec((tm, D_MODEL), lambda i, e: (i, 0)),
        out_shape=jax.ShapeDtypeStruct((T, D_MODEL), F32),
        scratch_shapes=[pltpu.VMEM((tm, D_MODEL), BF16), pltpu.VMEM((tm, D_MODEL), F32)],
        compiler_params=pltpu.CompilerParams(
            dimension_semantics=("parallel", "arbitrary"), vmem_limit_bytes=VMEM_LIMIT),
        name="moe",
    )(h1, gate, w13, w2, g2, b2)


def _rope_tables(S):
    t = jnp.arange(S, dtype=jnp.int32)
    row = (t // GRID_W).astype(F32)
    col = (t % GRID_W).astype(F32)
    half = ROPE_SECTION // 2
    inv_freq = ROPE_THETA ** (-jnp.arange(half, dtype=F32) / half)
    ang_r = row[:, None] * inv_freq
    ang_c = col[:, None] * inv_freq
    cos_h = jnp.concatenate([jnp.cos(ang_r)] * 2 + [jnp.cos(ang_c)] * 2, axis=1)
    sin_h = jnp.concatenate([-jnp.sin(ang_r), jnp.sin(ang_r), -jnp.sin(ang_c), jnp.sin(ang_c)], axis=1)
    reps = KV_WIDTH // HEAD_DIM
    return jnp.tile(cos_h, (1, reps)), jnp.tile(sin_h, (1, reps))


def _block_diag(w):
    H, d, _ = w.shape
    eye = jnp.eye(H, dtype=w.dtype)
    return (eye[:, None, :, None] * w[:, :, None, :]).reshape(H * d, H * d)


def _lru_gate_weights(wa, ba, wx, bx):
    C = LRU_CH
    ncg = D_LRU // C
    hpg = C // LRU_HEAD_DIM
    ws, bs = [], []
    for c in range(ncg):
        hs = slice(c * hpg, (c + 1) * hpg)
        cols, bias = [], []
        for d in range(2):
            cols += [_block_diag(wa[d, hs]), _block_diag(wx[d, hs])]
            bias += [ba[d, hs].reshape(C), bx[d, hs].reshape(C)]
        ws.append(jnp.concatenate(cols, axis=1))
        bs.append(jnp.concatenate(bias)[None, :])
    return jnp.stack(ws).astype(BF16), jnp.stack(bs)


def _trunk(x, P):
    B, S, _ = x.shape
    x2 = x.reshape(B * S, D_MODEL)
    q, k, v, xl, yl = _in_proj(x2, B, S, P["ln_in_g"], P["ln_in_b"], P["wq"], P["wkv"], P["wx"],
                               P["wy"], P["seg"], P["qg"], P["kg"], P["cos"], P["sin"])
    attn = _attention(q, k, v, B, S)
    lru = _rg_lru(xl, yl, B, S, P["conv_w"], P["conv_b"], P["wg"], P["bg"], P["lam"])
    h1, gate = _out_proj(x2, attn, lru, P["ln_in_g"], P["ln_in_b"], P["ag"], P["lg"], P["wo"],
                         P["g1"], P["b1"], P["wrh"], P["wrl"], P["br"])
    out = _moe(h1, gate, P["w13"], P["w2"], P["g2"], P["b2"])
    return out.reshape(B, S, D_MODEL)


def kernel(x_prompt, x_sample, ln_in_g, ln_in_b, w_in, conv_w, conv_b, lru_wa, lru_ba, lru_wx, lru_bx, lru_lambda, q_norm_g, k_norm_g, attn_out_g, lru_out_g, w_out, ln1_g, ln1_b, router_wg, router_bg, router_we, router_be, exp_w1, exp_w3, exp_w2, ln2_g, ln2_b):
    assert w_in.shape[0] == DEPTH == 1
    S = x_prompt.shape[1]
    l = 0
    w = w_in[l]
    c0, c1, c2 = D_ATTN, D_ATTN + 2 * KV_WIDTH, D_ATTN + 2 * KV_WIDTH + D_LRU
    seg = _block_diag(jnp.full((N_Q_HEADS, HEAD_DIM, HEAD_DIM), 1.0 / HEAD_DIM, F32)).astype(BF16)
    cos_t, sin_t = _rope_tables(S)
    wg, bg = _lru_gate_weights(lru_wa[l], lru_ba[l], lru_wx[l], lru_bx[l])
    wr = jnp.concatenate([router_wg[l], router_we[l].reshape(D_MODEL, N_EXPERTS)], axis=1)
    wr = jnp.pad(wr, ((0, 0), (0, ROUTER_PAD - wr.shape[1])))
    wrh = wr.astype(BF16)
    wrl = (wr - wrh.astype(F32)).astype(BF16)
    br = jnp.concatenate([router_bg[l], router_be[l].reshape(N_EXPERTS)])
    br = jnp.pad(br, (0, ROUTER_PAD - br.shape[0]))[None, :]
    P = dict(
        ln_in_g=ln_in_g[None, :], ln_in_b=ln_in_b[None, :],
        wq=w[:, :c0].astype(BF16), wkv=w[:, c0:c1].astype(BF16),
        wx=w[:, c1:c2].astype(BF16), wy=w[:, c2:].astype(BF16),
        seg=seg, qg=jnp.tile(q_norm_g[l], N_Q_HEADS)[None, :],
        kg=jnp.tile(k_norm_g[l], N_KV_HEADS)[None, :], cos=cos_t, sin=sin_t,
        conv_w=conv_w[l], conv_b=conv_b[l][None, :], wg=wg, bg=bg, lam=lru_lambda[l],
        ag=attn_out_g[l][None, :], lg=lru_out_g[l][None, :], wo=w_out[l].astype(BF16),
        g1=ln1_g[l][None, :], b1=ln1_b[l][None, :], wrh=wrh, wrl=wrl, br=br,
        w13=jnp.concatenate([exp_w1[l], exp_w3[l]], axis=2).astype(BF16),
        w2=exp_w2[l].astype(BF16), g2=ln2_g[l][None, :], b2=ln2_b[l][None, :],
    )
    return (_trunk(x_prompt, P), _trunk(x_sample, P))
```

```python
import functools
import math

import jax
import jax.numpy as jnp
from jax import lax
from jax.experimental import pallas as pl
from jax.experimental.pallas import tpu as pltpu

F32 = jnp.float32
BF16 = jnp.bfloat16

D_MODEL = 1024
GRID_W = 64
D_ATTN = 512
HEAD_DIM = 64
N_Q_HEADS = 8
N_KV_HEADS = 2
Q_PER_KV = 4
KV_WIDTH = 128
ROPE_THETA = 10000.0
ROPE_SECTION = 32
QK_EPS = 1e-6
D_LRU = 512
N_LRU_HEADS = 8
LRU_HEAD_DIM = 64
CONV_WIDTH = 4
LRU_C = 8.0
N_GROUPS = 4
EXPERTS_PER_GROUP = 4
N_EXPERTS = 16
D_EXPERT = 256
LN_EPS = 1e-5
DEPTH = 1
DEEPNORM_ALPHA = (2.0 * DEPTH) ** 0.25
LOG2_E = math.log2(math.e)

LANES = 128
SUBLANES = 8
VMEM_LIMIT = 48 * 1024 * 1024

TM_PROJ = 512
TQ = 256
LRU_CH = 256
LRU_ROWS = 256
TM_MOE = 512
MOE_CHUNK = 128
ROUTER_PAD = LANES
EXPERT_LANE0 = N_GROUPS


def _layer_norm(x, g, b):
    mu = jnp.mean(x, axis=-1, keepdims=True)
    xc = x - mu
    var = jnp.mean(xc * xc, axis=-1, keepdims=True)
    return xc * lax.rsqrt(var + LN_EPS) * g + b


def _rms_norm_rows(x, g, eps):
    return x * lax.rsqrt(jnp.mean(x * x, axis=-1, keepdims=True) + eps) * g


def _head_norm_rope(xf, seg, g, cos, sin):
    ms = jnp.dot((xf * xf).astype(BF16), seg, preferred_element_type=F32)
    xn = xf * lax.rsqrt(ms + QK_EPS) * g
    n = xn.shape[1]
    half = ROPE_SECTION // 2
    up = pltpu.roll(xn, n - half, axis=1)
    dn = pltpu.roll(xn, half, axis=1)
    lane = lax.broadcasted_iota(jnp.int32, xn.shape, 1)
    first = (lane % ROPE_SECTION) < half
    partner = jnp.where(first, up, dn)
    return xn * cos + partner * sin


def _in_proj_kernel(x_ref, g_ref, b_ref, wq_ref, wkv_ref, wx_ref, wy_ref, seg_ref,
                    qg_ref, kg_ref, cos_ref, sin_ref,
                    q_ref, k_ref, v_ref, xl_ref, yl_ref):
    h = _layer_norm(x_ref[...], g_ref[...], b_ref[...])
    hb = h.astype(BF16)
    cos = cos_ref[...]
    sin = sin_ref[...]
    reps = D_ATTN // KV_WIDTH

    qf = jnp.dot(hb, wq_ref[...], preferred_element_type=F32)
    q = _head_norm_rope(qf, seg_ref[...], qg_ref[...],
                        jnp.concatenate([cos] * reps, axis=1),
                        jnp.concatenate([sin] * reps, axis=1))
    q_ref[...] = (q * (HEAD_DIM ** -0.5 * LOG2_E)).astype(BF16)

    kvf = jnp.dot(hb, wkv_ref[...], preferred_element_type=F32)
    k = _head_norm_rope(kvf[:, :KV_WIDTH], seg_ref[:KV_WIDTH, :KV_WIDTH], kg_ref[...], cos, sin)
    kb = k.astype(BF16)
    vb = kvf[:, KV_WIDTH:].astype(BF16)
    for j in range(N_KV_HEADS):
        k_ref[j] = kb[:, j * HEAD_DIM:(j + 1) * HEAD_DIM]
        v_ref[j] = vb[:, j * HEAD_DIM:(j + 1) * HEAD_DIM]

    xl_ref[...] = jnp.dot(hb, wx_ref[...], preferred_element_type=F32)
    yl_ref[...] = jnp.dot(hb, wy_ref[...], preferred_element_type=F32)


def _in_proj(x2, B, S, ln_g, ln_b, wq, wkv, wx, wy, seg, qg, kg, cos_t, sin_t):
    T = B * S
    tm = TM_PROJ
    nt = S // tm
    const = lambda i: (0, 0)
    return pl.pallas_call(
        _in_proj_kernel,
        grid=(T // tm,),
        in_specs=[
            pl.BlockSpec((tm, D_MODEL), lambda i: (i, 0)),
            pl.BlockSpec((1, D_MODEL), const),
            pl.BlockSpec((1, D_MODEL), const),
            pl.BlockSpec((D_MODEL, D_ATTN), const),
            pl.BlockSpec((D_MODEL, 2 * KV_WIDTH), const),
            pl.BlockSpec((D_MODEL, D_LRU), const),
            pl.BlockSpec((D_MODEL, D_LRU), const),
            pl.BlockSpec((D_ATTN, D_ATTN), const),
            pl.BlockSpec((1, D_ATTN), const),
            pl.BlockSpec((1, KV_WIDTH), const),
            pl.BlockSpec((tm, KV_WIDTH), lambda i: (i % nt, 0)),
            pl.BlockSpec((tm, KV_WIDTH), lambda i: (i % nt, 0)),
        ],
        out_specs=[
            pl.BlockSpec((tm, D_ATTN), lambda i: (i, 0)),
            pl.BlockSpec((None, N_KV_HEADS, tm, HEAD_DIM), lambda i: (i // nt, 0, i % nt, 0)),
            pl.BlockSpec((None, N_KV_HEADS, tm, HEAD_DIM), lambda i: (i // nt, 0, i % nt, 0)),
            pl.BlockSpec((tm, D_LRU), lambda i: (i, 0)),
            pl.BlockSpec((tm, D_LRU), lambda i: (i, 0)),
        ],
        out_shape=[
            jax.ShapeDtypeStruct((T, D_ATTN), BF16),
            jax.ShapeDtypeStruct((B, N_KV_HEADS, S, HEAD_DIM), BF16),
            jax.ShapeDtypeStruct((B, N_KV_HEADS, S, HEAD_DIM), BF16),
            jax.ShapeDtypeStruct((T, D_LRU), F32),
            jax.ShapeDtypeStruct((T, D_LRU), F32),
        ],
        compiler_params=pltpu.CompilerParams(
            dimension_semantics=("parallel",), vmem_limit_bytes=VMEM_LIMIT),
        name="in_proj",
    )(x2, ln_g, ln_b, wq, wkv, wx, wy, seg, qg, kg, cos_t, sin_t)


def _attn_kernel(q_ref, k_ref, v_ref, o_ref):
    k = k_ref[...]
    v = v_ref[...]
    outs = []
    for h in range(Q_PER_KV):
        qh = q_ref[:, h * HEAD_DIM:(h + 1) * HEAD_DIM]
        s = lax.dot_general(qh, k, (((1,), (1,)), ((), ())), preferred_element_type=F32)
        m = jnp.max(s, axis=1, keepdims=True)
        p = jnp.exp2(s - m)
        l = jnp.sum(p, axis=1, keepdims=True)
        o = jnp.dot(p.astype(BF16), v, preferred_element_type=F32)
        outs.append(o / l)
    o_ref[...] = jnp.concatenate(outs, axis=1)


def _attention(q, k, v, B, S):
    T = B * S
    nq = S // TQ
    width = Q_PER_KV * HEAD_DIM
    return pl.pallas_call(
        _attn_kernel,
        grid=(B, N_KV_HEADS, nq),
        in_specs=[
            pl.BlockSpec((TQ, width), lambda b, j, t: (b * nq + t, j)),
            pl.BlockSpec((None, None, S, HEAD_DIM), lambda b, j, t: (b, j, 0, 0)),
            pl.BlockSpec((None, None, S, HEAD_DIM), lambda b, j, t: (b, j, 0, 0)),
        ],
        out_specs=pl.BlockSpec((TQ, width), lambda b, j, t: (b * nq + t, j)),
        out_shape=jax.ShapeDtypeStruct((T, D_ATTN), F32),
        compiler_params=pltpu.CompilerParams(
            dimension_semantics=("parallel", "parallel", "parallel"),
            vmem_limit_bytes=VMEM_LIMIT),
        name="attention",
    )(q, k, v)


def _scan_rows(a, u, reverse):
    row = lax.broadcasted_iota(jnp.int32, a.shape, 0)
    for d in (1, 2, 4):
        if reverse:
            shift, valid = SUBLANES - d, row < SUBLANES - d
        else:
            shift, valid = d, row >= d
        a_sh = jnp.where(valid, pltpu.roll(a, shift, axis=0), 1.0)
        u_sh = jnp.where(valid, pltpu.roll(u, shift, axis=0), 0.0)
        u = a * u_sh + u
        a = a * a_sh
    return a, u


def _gelu_tanh(x):
    c = math.sqrt(2.0 / math.pi)
    return 0.5 * x * (1.0 + jnp.tanh(c * (x + 0.044715 * (x * x * x))))


def _rg_lru_kernel(x_ref, y_ref, cw_ref, cb_ref, wg_ref, bg_ref, lam_ref, o_ref,
                   af_ref, uf_ref, ab_ref, ub_ref):
    S, C = x_ref.shape
    R = LRU_ROWS
    nblk = S // R
    lam = lam_ref[...]
    neg_c_softplus = -LRU_C * jnp.logaddexp(-lam, 0.0)
    cw = cw_ref[...]
    cb = cb_ref[...]
    bg = bg_ref[...]

    def gate_block(r, carry):
        start = pl.multiple_of(r * R, R)
        cur = x_ref[pl.ds(start, R), :]
        prev = x_ref[pl.ds(pl.multiple_of(jnp.maximum(start - SUBLANES, 0), SUBLANES), SUBLANES), :]
        prev = jnp.where(r > 0, prev, 0.0)
        nxt = x_ref[pl.ds(pl.multiple_of(jnp.minimum(start + R, S - SUBLANES), SUBLANES), SUBLANES), :]
        nxt = jnp.where(r < nblk - 1, nxt, 0.0)
        ext = jnp.concatenate([prev, cur, nxt], axis=0)
        xc = cb
        for tap in range(CONV_WIDTH):
            off = SUBLANES + tap - CONV_WIDTH // 2
            xc = xc + ext[off:off + R, :] * cw[tap:tap + 1, :]
        z = jnp.dot(xc.astype(BF16), wg_ref[...], preferred_element_type=F32) + bg
        for d, (a_ref, u_ref) in enumerate(((af_ref, uf_ref), (ab_ref, ub_ref))):
            rg = jax.nn.sigmoid(z[:, (2 * d) * C:(2 * d + 1) * C])
            ig = jax.nn.sigmoid(z[:, (2 * d + 1) * C:(2 * d + 2) * C])
            a = jnp.exp(neg_c_softplus[d:d + 1, :] * rg)
            u = jnp.sqrt(1.0 - a * a) * (ig * xc)
            a_ref[pl.ds(start, R), :] = a
            u_ref[pl.ds(start, R), :] = u
        return carry

    lax.fori_loop(0, nblk, gate_block, 0)

    nchunk = S // SUBLANES

    def scan_chunk(c, carry):
        hf, hb = carry
        sf = pl.multiple_of(c * SUBLANES, SUBLANES)
        a_in, u_in = _scan_rows(af_ref[pl.ds(sf, SUBLANES), :], uf_ref[pl.ds(sf, SUBLANES), :], False)
        h_cur = a_in * hf + u_in
        uf_ref[pl.ds(sf, SUBLANES), :] = h_cur
        hf = jnp.broadcast_to(h_cur[SUBLANES - 1:SUBLANES, :], h_cur.shape)
        sb = pl.multiple_of((nchunk - 1 - c) * SUBLANES, SUBLANES)
        a_in, u_in = _scan_rows(ab_ref[pl.ds(sb, SUBLANES), :], ub_ref[pl.ds(sb, SUBLANES), :], True)
        h_cur = a_in * hb + u_in
        ub_ref[pl.ds(sb, SUBLANES), :] = h_cur
        hb = jnp.broadcast_to(h_cur[0:1, :], h_cur.shape)
        return hf, hb

    zero = jnp.zeros((SUBLANES, C), F32)
    lax.fori_loop(0, nchunk, scan_chunk, (zero, zero), unroll=4)

    def out_block(r, carry):
        start = pl.multiple_of(r * R, R)
        rows = pl.ds(start, R)
        o_ref[rows, :] = (uf_ref[rows, :] + ub_ref[rows, :]) * _gelu_tanh(y_ref[rows, :])
        return carry

    lax.fori_loop(0, nblk, out_block, 0)


def _rg_lru(xl, yl, B, S, conv_w, conv_b, wg, bg, lam):
    C = LRU_CH
    ncg = D_LRU // C
    xl3 = xl.reshape(B, S, D_LRU)
    yl3 = yl.reshape(B, S, D_LRU)
    out = pl.pallas_call(
        _rg_lru_kernel,
        grid=(B, ncg),
        in_specs=[
            pl.BlockSpec((None, S, C), lambda b, c: (b, 0, c)),
            pl.BlockSpec((None, S, C), lambda b, c: (b, 0, c)),
            pl.BlockSpec((CONV_WIDTH, C), lambda b, c: (0, c)),
            pl.BlockSpec((1, C), lambda b, c: (0, c)),
            pl.BlockSpec((None, C, 4 * C), lambda b, c: (c, 0, 0)),
            pl.BlockSpec((None, 1, 4 * C), lambda b, c: (c, 0, 0)),
            pl.BlockSpec((2, C), lambda b, c: (0, c)),
        ],
        out_specs=pl.BlockSpec((None, S, C), lambda b, c: (b, 0, c)),
        out_shape=jax.ShapeDtypeStruct((B, S, D_LRU), F32),
        scratch_shapes=[pltpu.VMEM((S, C), F32)] * 4,
        compiler_params=pltpu.CompilerParams(
            dimension_semantics=("parallel", "parallel"), vmem_limit_bytes=VMEM_LIMIT),
        name="rg_lru",
    )(xl3, yl3, conv_w, conv_b, wg, bg, lam)
    return out.reshape(B * S, D_LRU)


def _router_gates(logits):
    lane = lax.broadcasted_iota(jnp.int32, logits.shape, 1)
    neg = -jnp.inf
    big = jnp.int32(2 * LANES)
    gl = jnp.where(lane < N_GROUPS, logits, neg)
    gmax = jnp.max(gl, axis=1, keepdims=True)
    gidx = jnp.min(jnp.where(gl == gmax, lane, big), axis=1, keepdims=True)
    g_sel_prob = 1.0 / jnp.sum(jnp.exp(gl - gmax), axis=1, keepdims=True)
    lo = EXPERT_LANE0 + EXPERTS_PER_GROUP * gidx
    el = jnp.where((lane >= lo) & (lane < lo + EXPERTS_PER_GROUP), logits, neg)
    v1 = jnp.max(el, axis=1, keepdims=True)
    i1 = jnp.min(jnp.where(el == v1, lane, big), axis=1, keepdims=True)
    el2 = jnp.where(lane == i1, neg, el)
    v2 = jnp.max(el2, axis=1, keepdims=True)
    i2 = jnp.min(jnp.where(el2 == v2, lane, big), axis=1, keepdims=True)
    e2 = jnp.exp(v2 - v1)
    w1 = g_sel_prob / (1.0 + e2)
    w2 = g_sel_prob * e2 / (1.0 + e2)
    gates = jnp.where(lane == i1, w1, jnp.where(lane == i2, w2, 0.0))
    return jnp.where(lane == 0, gidx.astype(F32), gates)


def _out_proj_kernel(x_ref, attn_ref, lru_ref, lng_ref, lnb_ref, ag_ref, lg_ref, wo_ref,
                     g1_ref, b1_ref, wrh_ref, wrl_ref, br_ref, h1_ref, gate_ref):
    h = _layer_norm(x_ref[...], lng_ref[...], lnb_ref[...])
    merged = jnp.concatenate(
        [_rms_norm_rows(attn_ref[...], ag_ref[...], LN_EPS),
         _rms_norm_rows(lru_ref[...], lg_ref[...], LN_EPS)], axis=1).astype(BF16)
    mix = jnp.dot(merged, wo_ref[...], preferred_element_type=F32)
    h1 = _layer_norm(DEEPNORM_ALPHA * h + mix, g1_ref[...], b1_ref[...])
    h1_ref[...] = h1
    hi = h1.astype(BF16)
    lo = (h1 - hi.astype(F32)).astype(BF16)
    wrh = wrh_ref[...]
    logits = (jnp.dot(hi, wrh, preferred_element_type=F32)
              + jnp.dot(lo, wrh, preferred_element_type=F32)
              + jnp.dot(hi, wrl_ref[...], preferred_element_type=F32)) + br_ref[...]
    gate_ref[...] = _router_gates(logits)


def _out_proj(x2, attn, lru, ln_g, ln_b, ag, lg, wo, g1, b1, wrh, wrl, br):
    T = x2.shape[0]
    tm = TM_PROJ
    const = lambda i: (0, 0)
    row = lambda i: (i, 0)
    return pl.pallas_call(
        _out_proj_kernel,
        grid=(T // tm,),
        in_specs=[
            pl.BlockSpec((tm, D_MODEL), row),
            pl.BlockSpec((tm, D_ATTN), row),
            pl.BlockSpec((tm, D_LRU), row),
            pl.BlockSpec((1, D_MODEL), const),
            pl.BlockSpec((1, D_MODEL), const),
            pl.BlockSpec((1, D_ATTN), const),
            pl.BlockSpec((1, D_LRU), const),
            pl.BlockSpec((D_MODEL, D_MODEL), const),
            pl.BlockSpec((1, D_MODEL), const),
            pl.BlockSpec((1, D_MODEL), const),
            pl.BlockSpec((D_MODEL, ROUTER_PAD), const),
            pl.BlockSpec((D_MODEL, ROUTER_PAD), const),
            pl.BlockSpec((1, ROUTER_PAD), const),
        ],
        out_specs=[pl.BlockSpec((tm, D_MODEL), row), pl.BlockSpec((tm, ROUTER_PAD), row)],
        out_shape=[jax.ShapeDtypeStruct((T, D_MODEL), F32),
                   jax.ShapeDtypeStruct((T, ROUTER_PAD), F32)],
        compiler_params=pltpu.CompilerParams(
            dimension_semantics=("parallel",), vmem_limit_bytes=VMEM_LIMIT),
        name="out_proj",
    )(x2, attn, lru, ln_g, ln_b, ag, lg, wo, g1, b1, wrh, wrl, br)


def _lane_scalar(row, idx):
    lane = lax.broadcasted_iota(jnp.int32, row.shape, 1)
    return jnp.sum(jnp.where(lane == idx, row, 0.0))


def _moe_kernel(h1_ref, route_ref, tril_ref, w13_ref, w2_ref, g2_ref, b2_ref, o_ref,
                pt_ref, xs_ref, gs_ref, ys_ref):
    tm = h1_ref.shape[0]
    c = MOE_CHUNK
    h1 = h1_ref[...]
    route = route_ref[...]
    lane = lax.broadcasted_iota(jnp.int32, route.shape, 1)
    gid = jnp.sum(jnp.where(lane == 0, route, 0.0), axis=1, keepdims=True)
    onehot = jnp.where(lane.astype(F32) == gid, 1.0, 0.0)
    half = tm // 2
    ohb = onehot.astype(BF16)
    tril = tril_ref[...]
    cnt_top = jnp.sum(onehot[:half], axis=0, keepdims=True)
    before = jnp.concatenate(
        [jnp.dot(tril, ohb[:half], preferred_element_type=F32),
         jnp.dot(tril, ohb[half:], preferred_element_type=F32) + cnt_top], axis=0)
    cnt = jnp.broadcast_to(cnt_top + jnp.sum(onehot[half:], axis=0, keepdims=True), (SUBLANES, LANES))
    start = pltpu.roll(cnt, 1, axis=1) + pltpu.roll(cnt, 2, axis=1) + pltpu.roll(cnt, 3, axis=1)
    start = start[0:1, :]
    dest = jnp.sum(onehot * (before + start), axis=1, keepdims=True)
    col = lax.broadcasted_iota(jnp.int32, (tm, tm), 1).astype(F32)
    pt_ref[...] = jnp.where(col == dest, 1.0, 0.0).astype(BF16)
    dest_row = jnp.broadcast_to(dest, (tm, LANES)).T[0:1, :]
    row = lax.broadcasted_iota(jnp.int32, (tm, tm), 0).astype(F32)
    p = jnp.where(row == dest_row, 1.0, 0.0).astype(BF16)

    xs_ref[...] = jnp.dot(p, h1.astype(BF16), preferred_element_type=F32).astype(BF16)
    rhi = route.astype(BF16)
    rlo = (route - rhi.astype(F32)).astype(BF16)
    gs2 = jnp.dot(p, jnp.concatenate([rhi, rlo], axis=1), preferred_element_type=F32)
    gs_ref[...] = gs2[:, :LANES] + gs2[:, LANES:]
    ys_ref[...] = jnp.zeros_like(ys_ref)

    bounds = [jnp.int32(0)] + [_lane_scalar(start, g).astype(jnp.int32) for g in range(1, N_GROUPS)]
    bounds.append(jnp.int32(tm))
    gw = EXPERTS_PER_GROUP * D_EXPERT
    for j in range(tm // c):
        rows = slice(j * c, (j + 1) * c)
        for g in range(N_GROUPS):
            @pl.when((bounds[g] < (j + 1) * c) & (bounds[g + 1] > j * c))
            def _():
                gsj = gs_ref[rows, :]
                lane_c = lax.broadcasted_iota(jnp.int32, gsj.shape, 1)
                up = jnp.dot(xs_ref[rows, :], w13_ref[g], preferred_element_type=F32)
                u1 = up[:, :gw]
                hid = (u1 * jax.nn.sigmoid(u1)) * up[:, gw:]
                gate_cols = []
                for k in range(EXPERTS_PER_GROUP):
                    e = g * EXPERTS_PER_GROUP + k
                    ge = jnp.sum(jnp.where(lane_c == EXPERT_LANE0 + e, gsj, 0.0), axis=1, keepdims=True)
                    gate_cols.append(jnp.broadcast_to(ge, (c, D_EXPERT)))
                hid = (hid * jnp.concatenate(gate_cols, axis=1)).astype(BF16)
                ys_ref[rows, :] += jnp.dot(hid, w2_ref[g], preferred_element_type=F32)

    ffn = jnp.dot(pt_ref[...], ys_ref[...].astype(BF16), preferred_element_type=F32)
    o_ref[...] = _layer_norm(DEEPNORM_ALPHA * h1 + ffn, g2_ref[...], b2_ref[...])


def _moe(h1, route, tril, w13, w2, g2, b2):
    T = h1.shape[0]
    tm = TM_MOE
    gw = EXPERTS_PER_GROUP * D_EXPERT
    const2 = lambda i: (0, 0)
    const3 = lambda i: (0, 0, 0)
    resident = pl.Buffered(1)
    return pl.pallas_call(
        _moe_kernel,
        grid=(T // tm,),
        in_specs=[
            pl.BlockSpec((tm, D_MODEL), lambda i: (i, 0)),
            pl.BlockSpec((tm, ROUTER_PAD), lambda i: (i, 0)),
            pl.BlockSpec((tm // 2, tm // 2), const2, pipeline_mode=resident),
            pl.BlockSpec((N_GROUPS, D_MODEL, 2 * gw), const3, pipeline_mode=resident),
            pl.BlockSpec((N_GROUPS, gw, D_MODEL), const3, pipeline_mode=resident),
            pl.BlockSpec((1, D_MODEL), const2),
            pl.BlockSpec((1, D_MODEL), const2),
        ],
        out_specs=pl.BlockSpec((tm, D_MODEL), lambda i: (i, 0)),
        out_shape=jax.ShapeDtypeStruct((T, D_MODEL), F32),
        scratch_shapes=[pltpu.VMEM((tm, tm), BF16), pltpu.VMEM((tm, D_MODEL), BF16),
                        pltpu.VMEM((tm, ROUTER_PAD), F32), pltpu.VMEM((tm, D_MODEL), F32)],
        compiler_params=pltpu.CompilerParams(
            dimension_semantics=("parallel",), vmem_limit_bytes=VMEM_LIMIT),
        name="moe",
    )(h1, route, tril, w13, w2, g2, b2)


def _rope_tables(S):
    t = jnp.arange(S, dtype=jnp.int32)
    row = (t // GRID_W).astype(F32)
    col = (t % GRID_W).astype(F32)
    half = ROPE_SECTION // 2
    inv_freq = ROPE_THETA ** (-jnp.arange(half, dtype=F32) / half)
    ang_r = row[:, None] * inv_freq
    ang_c = col[:, None] * inv_freq
    cos_h = jnp.concatenate([jnp.cos(ang_r)] * 2 + [jnp.cos(ang_c)] * 2, axis=1)
    sin_h = jnp.concatenate([-jnp.sin(ang_r), jnp.sin(ang_r), -jnp.sin(ang_c), jnp.sin(ang_c)], axis=1)
    reps = KV_WIDTH // HEAD_DIM
    return jnp.tile(cos_h, (1, reps)), jnp.tile(sin_h, (1, reps))


def _block_diag(w):
    H, d, _ = w.shape
    eye = jnp.eye(H, dtype=w.dtype)
    return (eye[:, None, :, None] * w[:, :, None, :]).reshape(H * d, H * d)


def _lru_gate_weights(wa, ba, wx, bx):
    C = LRU_CH
    ncg = D_LRU // C
    hpg = C // LRU_HEAD_DIM
    ws, bs = [], []
    for c in range(ncg):
        hs = slice(c * hpg, (c + 1) * hpg)
        cols, bias = [], []
        for d in range(2):
            cols += [_block_diag(wa[d, hs]), _block_diag(wx[d, hs])]
            bias += [ba[d, hs].reshape(C), bx[d, hs].reshape(C)]
        ws.append(jnp.concatenate(cols, axis=1))
        bs.append(jnp.concatenate(bias)[None, :])
    return jnp.stack(ws).astype(BF16), jnp.stack(bs)


def _group_columns(w):
    E, D, Fh = w.shape
    w = w.reshape(N_GROUPS, EXPERTS_PER_GROUP, D, Fh)
    return jnp.transpose(w, (0, 2, 1, 3)).reshape(N_GROUPS, D, EXPERTS_PER_GROUP * Fh)


def _trunk(x, P):
    B, S, _ = x.shape
    x2 = x.reshape(B * S, D_MODEL)
    q, k, v, xl, yl = _in_proj(x2, B, S, P["ln_in_g"], P["ln_in_b"], P["wq"], P["wkv"], P["wx"],
                               P["wy"], P["seg"], P["qg"], P["kg"], P["cos"], P["sin"])
    attn = _attention(q, k, v, B, S)
    lru = _rg_lru(xl, yl, B, S, P["conv_w"], P["conv_b"], P["wg"], P["bg"], P["lam"])
    h1, route = _out_proj(x2, attn, lru, P["ln_in_g"], P["ln_in_b"], P["ag"], P["lg"], P["wo"],
                          P["g1"], P["b1"], P["wrh"], P["wrl"], P["br"])
    out = _moe(h1, route, P["tril"], P["w13"], P["w2"], P["g2"], P["b2"])
    return out.reshape(B, S, D_MODEL)


def kernel(x_prompt, x_sample, ln_in_g, ln_in_b, w_in, conv_w, conv_b, lru_wa, lru_ba, lru_wx, lru_bx, lru_lambda, q_norm_g, k_norm_g, attn_out_g, lru_out_g, w_out, ln1_g, ln1_b, router_wg, router_bg, router_we, router_be, exp_w1, exp_w3, exp_w2, ln2_g, ln2_b):
    assert w_in.shape[0] == DEPTH == 1
    S = x_prompt.shape[1]
    l = 0
    w = w_in[l]
    c0, c1, c2 = D_ATTN, D_ATTN + 2 * KV_WIDTH, D_ATTN + 2 * KV_WIDTH + D_LRU
    seg = _block_diag(jnp.full((N_Q_HEADS, HEAD_DIM, HEAD_DIM), 1.0 / HEAD_DIM, F32)).astype(BF16)
    cos_t, sin_t = _rope_tables(S)
    wg, bg = _lru_gate_weights(lru_wa[l], lru_ba[l], lru_wx[l], lru_bx[l])
    wr = jnp.concatenate([router_wg[l], router_we[l].reshape(D_MODEL, N_EXPERTS)], axis=1)
    wr = jnp.pad(wr, ((0, 0), (0, ROUTER_PAD - wr.shape[1])))
    wrh = wr.astype(BF16)
    wrl = (wr - wrh.astype(F32)).astype(BF16)
    br = jnp.concatenate([router_bg[l], router_be[l].reshape(N_EXPERTS)])
    br = jnp.pad(br, (0, ROUTER_PAD - br.shape[0]))[None, :]
    P = dict(
        ln_in_g=ln_in_g[None, :], ln_in_b=ln_in_b[None, :],
        wq=w[:, :c0].astype(BF16), wkv=w[:, c0:c1].astype(BF16),
        wx=w[:, c1:c2].astype(BF16), wy=w[:, c2:].astype(BF16),
        seg=seg, qg=jnp.tile(q_norm_g[l], N_Q_HEADS)[None, :],
        kg=jnp.tile(k_norm_g[l], N_KV_HEADS)[None, :], cos=cos_t, sin=sin_t,
        conv_w=conv_w[l], conv_b=conv_b[l][None, :], wg=wg, bg=bg, lam=lru_lambda[l],
        ag=attn_out_g[l][None, :], lg=lru_out_g[l][None, :], wo=w_out[l].astype(BF16),
        g1=ln1_g[l][None, :], b1=ln1_b[l][None, :], wrh=wrh, wrl=wrl, br=br,
        w13=jnp.concatenate([_group_columns(exp_w1[l]), _group_columns(exp_w3[l])], axis=2).astype(BF16),
        w2=exp_w2[l].reshape(N_GROUPS, EXPERTS_PER_GROUP * D_EXPERT, D_MODEL).astype(BF16),
        g2=ln2_g[l][None, :], b2=ln2_b[l][None, :],
        tril=jnp.tril(jnp.ones((TM_MOE // 2, TM_MOE // 2), F32), -1).astype(BF16),
    )
    return (_trunk(x_prompt, P), _trunk(x_sample, P))
```

```python
import functools
import math

import jax
import jax.numpy as jnp
from jax import lax
from jax.experimental import pallas as pl
from jax.experimental.pallas import tpu as pltpu

F32 = jnp.float32
BF16 = jnp.bfloat16

D_MODEL = 1024
GRID_W = 64
D_ATTN = 512
HEAD_DIM = 64
N_Q_HEADS = 8
N_KV_HEADS = 2
Q_PER_KV = 4
KV_WIDTH = 128
ROPE_THETA = 10000.0
ROPE_SECTION = 32
QK_EPS = 1e-6
D_LRU = 512
N_LRU_HEADS = 8
LRU_HEAD_DIM = 64
CONV_WIDTH = 4
LRU_C = 8.0
N_GROUPS = 4
EXPERTS_PER_GROUP = 4
N_EXPERTS = 16
D_EXPERT = 256
LN_EPS = 1e-5
DEPTH = 1
DEEPNORM_ALPHA = (2.0 * DEPTH) ** 0.25
LOG2_E = math.log2(math.e)

LANES = 128
SUBLANES = 8
VMEM_LIMIT = 48 * 1024 * 1024

TM_PROJ = 512
TQ = 512
LRU_CH = 256
SEG = 256
LRU_SCAN_UNROLL = 4
LRU_COMBINE_ROWS = 128
TM_MOE = 512
MOE_CHUNK = 128
ROUTER_PAD = LANES
EXPERT_LANE0 = N_GROUPS


def _layer_norm(x, g, b):
    mu = jnp.mean(x, axis=-1, keepdims=True)
    xc = x - mu
    var = jnp.mean(xc * xc, axis=-1, keepdims=True)
    return xc * lax.rsqrt(var + LN_EPS) * g + b


def _rms_norm_rows(x, g, eps):
    return x * lax.rsqrt(jnp.mean(x * x, axis=-1, keepdims=True) + eps) * g


def _head_norm_rope(xf, seg, g, cos, sin):
    ms = jnp.dot((xf * xf).astype(BF16), seg, preferred_element_type=F32)
    xn = xf * lax.rsqrt(ms + QK_EPS) * g
    n = xn.shape[1]
    half = ROPE_SECTION // 2
    up = pltpu.roll(xn, n - half, axis=1)
    dn = pltpu.roll(xn, half, axis=1)
    lane = lax.broadcasted_iota(jnp.int32, xn.shape, 1)
    first = (lane % ROPE_SECTION) < half
    partner = jnp.where(first, up, dn)
    return xn * cos + partner * sin


def _in_proj_kernel(x_ref, g_ref, b_ref, wq_ref, wkv_ref, wx_ref, wy_ref, seg_ref,
                    qg_ref, kg_ref, cos_ref, sin_ref,
                    q_ref, k_ref, v_ref, xl_ref, yl_ref):
    h = _layer_norm(x_ref[...], g_ref[...], b_ref[...])
    hb = h.astype(BF16)
    cos = cos_ref[...]
    sin = sin_ref[...]
    reps = D_ATTN // KV_WIDTH

    qf = jnp.dot(hb, wq_ref[...], preferred_element_type=F32)
    q = _head_norm_rope(qf, seg_ref[...], qg_ref[...],
                        jnp.concatenate([cos] * reps, axis=1),
                        jnp.concatenate([sin] * reps, axis=1))
    q_ref[...] = (q * (HEAD_DIM ** -0.5 * LOG2_E)).astype(BF16)

    kvf = jnp.dot(hb, wkv_ref[...], preferred_element_type=F32)
    k = _head_norm_rope(kvf[:, :KV_WIDTH], seg_ref[:KV_WIDTH, :KV_WIDTH], kg_ref[...], cos, sin)
    kb = k.astype(BF16)
    vb = kvf[:, KV_WIDTH:].astype(BF16)
    for j in range(N_KV_HEADS):
        k_ref[j] = kb[:, j * HEAD_DIM:(j + 1) * HEAD_DIM]
        v_ref[j] = vb[:, j * HEAD_DIM:(j + 1) * HEAD_DIM]

    xl_ref[...] = jnp.dot(hb, wx_ref[...], preferred_element_type=F32)
    yl_ref[...] = jnp.dot(hb, wy_ref[...], preferred_element_type=F32)


def _in_proj(x2, B, S, ln_g, ln_b, wq, wkv, wx, wy, seg, qg, kg, cos_t, sin_t):
    T = B * S
    tm = TM_PROJ
    nt = S // tm
    const = lambda i: (0, 0)
    return pl.pallas_call(
        _in_proj_kernel,
        grid=(T // tm,),
        in_specs=[
            pl.BlockSpec((tm, D_MODEL), lambda i: (i, 0)),
            pl.BlockSpec((1, D_MODEL), const),
            pl.BlockSpec((1, D_MODEL), const),
            pl.BlockSpec((D_MODEL, D_ATTN), const),
            pl.BlockSpec((D_MODEL, 2 * KV_WIDTH), const),
            pl.BlockSpec((D_MODEL, D_LRU), const),
            pl.BlockSpec((D_MODEL, D_LRU), const),
            pl.BlockSpec((D_ATTN, D_ATTN), const),
            pl.BlockSpec((1, D_ATTN), const),
            pl.BlockSpec((1, KV_WIDTH), const),
            pl.BlockSpec((tm, KV_WIDTH), lambda i: (i % nt, 0)),
            pl.BlockSpec((tm, KV_WIDTH), lambda i: (i % nt, 0)),
        ],
        out_specs=[
            pl.BlockSpec((tm, D_ATTN), lambda i: (i, 0)),
            pl.BlockSpec((None, N_KV_HEADS, tm, HEAD_DIM), lambda i: (i // nt, 0, i % nt, 0)),
            pl.BlockSpec((None, N_KV_HEADS, tm, HEAD_DIM), lambda i: (i // nt, 0, i % nt, 0)),
            pl.BlockSpec((tm, D_LRU), lambda i: (i, 0)),
            pl.BlockSpec((tm, D_LRU), lambda i: (i, 0)),
        ],
        out_shape=[
            jax.ShapeDtypeStruct((T, D_ATTN), BF16),
            jax.ShapeDtypeStruct((B, N_KV_HEADS, S, HEAD_DIM), BF16),
            jax.ShapeDtypeStruct((B, N_KV_HEADS, S, HEAD_DIM), BF16),
            jax.ShapeDtypeStruct((T, D_LRU), F32),
            jax.ShapeDtypeStruct((T, D_LRU), F32),
        ],
        compiler_params=pltpu.CompilerParams(
            dimension_semantics=("parallel",), vmem_limit_bytes=VMEM_LIMIT),
        name="in_proj",
    )(x2, ln_g, ln_b, wq, wkv, wx, wy, seg, qg, kg, cos_t, sin_t)


def _attn_kernel(q_ref, k_ref, v_ref, o_ref):
    k = k_ref[...]
    v = v_ref[...]
    outs = []
    for h in range(Q_PER_KV):
        qh = q_ref[:, h * HEAD_DIM:(h + 1) * HEAD_DIM]
        s = lax.dot_general(qh, k, (((1,), (1,)), ((), ())), preferred_element_type=F32)
        m = jnp.max(s, axis=1, keepdims=True)
        p = jnp.exp2(s - m)
        l = jnp.sum(p, axis=1, keepdims=True)
        o = jnp.dot(p.astype(BF16), v, preferred_element_type=F32)
        outs.append(o / l)
    o_ref[...] = jnp.concatenate(outs, axis=1)


def _attention(q, k, v, B, S):
    T = B * S
    nq = S // TQ
    width = Q_PER_KV * HEAD_DIM
    return pl.pallas_call(
        _attn_kernel,
        grid=(B, N_KV_HEADS, nq),
        in_specs=[
            pl.BlockSpec((TQ, width), lambda b, j, t: (b * nq + t, j)),
            pl.BlockSpec((None, None, S, HEAD_DIM), lambda b, j, t: (b, j, 0, 0)),
            pl.BlockSpec((None, None, S, HEAD_DIM), lambda b, j, t: (b, j, 0, 0)),
        ],
        out_specs=pl.BlockSpec((TQ, width), lambda b, j, t: (b * nq + t, j)),
        out_shape=jax.ShapeDtypeStruct((T, D_ATTN), F32),
        compiler_params=pltpu.CompilerParams(
            dimension_semantics=("parallel", "parallel", "parallel"),
            vmem_limit_bytes=VMEM_LIMIT),
        name="attention",
    )(q, k, v)


def _gelu_tanh(x):
    c = math.sqrt(2.0 / math.pi)
    return 0.5 * x * (1.0 + jnp.tanh(c * (x + 0.044715 * (x * x * x))))


def _rg_lru_kernel(x_ref, y_ref, cw_ref, cb_ref, wg_ref, bg_ref, lam_ref, o_ref,
                   af_ref, uf_ref, ab_ref, ub_ref, hf_ref, pf_ref, hb_ref, pb_ref, cf_ref, cbk_ref):
    S, C = x_ref.shape
    nslab = C // LANES
    nseg = S // SEG
    assert nseg == SUBLANES
    lam = lam_ref[...]
    neg_c_softplus = -LRU_C * jnp.logaddexp(-lam, 0.0)
    cw = cw_ref[...]
    cb = cb_ref[...]
    bg = bg_ref[...]

    def gate_block(s, carry):
        start = pl.multiple_of(s * SEG, SEG)
        seg_rows = pl.ds(s, SEG, stride=SUBLANES)
        cur = x_ref[pl.ds(start, SEG), :]
        prev = x_ref[pl.ds(pl.multiple_of(jnp.maximum(start - SUBLANES, 0), SUBLANES), SUBLANES), :]
        prev = jnp.where(s > 0, prev, 0.0)
        nxt = x_ref[pl.ds(pl.multiple_of(jnp.minimum(start + SEG, S - SUBLANES), SUBLANES), SUBLANES), :]
        nxt = jnp.where(s < nseg - 1, nxt, 0.0)
        ext = jnp.concatenate([prev, cur, nxt], axis=0)
        xc = cb
        for tap in range(CONV_WIDTH):
            off = SUBLANES + tap - CONV_WIDTH // 2
            xc = xc + ext[off:off + SEG, :] * cw[tap:tap + 1, :]
        z = jnp.dot(xc.astype(BF16), wg_ref[...], preferred_element_type=F32) + bg
        for d, (a_ref, u_ref) in enumerate(((af_ref, uf_ref), (ab_ref, ub_ref))):
            rg = jax.nn.sigmoid(z[:, (2 * d) * C:(2 * d + 1) * C])
            ig = jax.nn.sigmoid(z[:, (2 * d + 1) * C:(2 * d + 2) * C])
            a = jnp.exp(neg_c_softplus[d:d + 1, :] * rg)
            u = jnp.sqrt(1.0 - a * a) * (ig * xc)
            for slab in range(nslab):
                lanes = slice(slab * LANES, (slab + 1) * LANES)
                a_ref[slab, seg_rows, :] = a[:, lanes]
                u_ref[slab, seg_rows, :] = u[:, lanes]
        return carry

    lax.fori_loop(0, nseg, gate_block, 0)

    def scan_step(j, carry):
        out = []
        for slab in range(nslab):
            hf, pf, hb, pb = carry[slab]
            rows = pl.ds(pl.multiple_of(j * SUBLANES, SUBLANES), SUBLANES)
            a = af_ref[slab, rows, :]
            hf = a * hf + uf_ref[slab, rows, :]
            pf = a * pf
            hf_ref[slab, rows, :] = hf
            pf_ref[slab, rows, :] = pf
            rows = pl.ds(pl.multiple_of((SEG - 1 - j) * SUBLANES, SUBLANES), SUBLANES)
            a = ab_ref[slab, rows, :]
            hb = a * hb + ub_ref[slab, rows, :]
            pb = a * pb
            hb_ref[slab, rows, :] = hb
            pb_ref[slab, rows, :] = pb
            out.append((hf, pf, hb, pb))
        return tuple(out)

    zero = jnp.zeros((SUBLANES, LANES), F32)
    one = jnp.ones((SUBLANES, LANES), F32)
    final = lax.fori_loop(0, SEG, scan_step, tuple((zero, one, zero, one) for _ in range(nslab)),
                          unroll=LRU_SCAN_UNROLL)

    for slab in range(nslab):
        hf, pf, hb, pb = final[slab]
        c = jnp.zeros((1, LANES), F32)
        fwd = [c]
        for sgm in range(1, nseg):
            c = hf[sgm - 1:sgm, :] + pf[sgm - 1:sgm, :] * c
            fwd.append(c)
        c = jnp.zeros((1, LANES), F32)
        bwd = [c]
        for sgm in range(nseg - 2, -1, -1):
            c = hb[sgm + 1:sgm + 2, :] + pb[sgm + 1:sgm + 2, :] * c
            bwd.append(c)
        cf_ref[slab] = jnp.concatenate(fwd, axis=0)
        cbk_ref[slab] = jnp.concatenate(bwd[::-1], axis=0)

    reps = LRU_COMBINE_ROWS // SUBLANES

    def combine(i, carry):
        rows = pl.ds(pl.multiple_of(i * LRU_COMBINE_ROWS, LRU_COMBINE_ROWS), LRU_COMBINE_ROWS)
        for slab in range(nslab):
            cf = jnp.concatenate([cf_ref[slab]] * reps, axis=0)
            cbk = jnp.concatenate([cbk_ref[slab]] * reps, axis=0)
            uf_ref[slab, rows, :] = ((hf_ref[slab, rows, :] + pf_ref[slab, rows, :] * cf)
                                     + (hb_ref[slab, rows, :] + pb_ref[slab, rows, :] * cbk))
        return carry

    lax.fori_loop(0, S // LRU_COMBINE_ROWS, combine, 0)

    def out_block(s, carry):
        start = pl.multiple_of(s * SEG, SEG)
        for slab in range(nslab):
            lanes = slice(slab * LANES, (slab + 1) * LANES)
            h = uf_ref[slab, pl.ds(s, SEG, stride=SUBLANES), :]
            o_ref[pl.ds(start, SEG), lanes] = h * _gelu_tanh(y_ref[pl.ds(start, SEG), lanes])
        return carry

    lax.fori_loop(0, nseg, out_block, 0)


def _rg_lru(xl, yl, B, S, conv_w, conv_b, wg, bg, lam):
    C = LRU_CH
    ncg = D_LRU // C
    nslab = C // LANES
    xl3 = xl.reshape(B, S, D_LRU)
    yl3 = yl.reshape(B, S, D_LRU)
    seg_buf = pltpu.VMEM((nslab, S, LANES), F32)
    carry_buf = pltpu.VMEM((nslab, S // SEG, LANES), F32)
    out = pl.pallas_call(
        _rg_lru_kernel,
        grid=(B, ncg),
        in_specs=[
            pl.BlockSpec((None, S, C), lambda b, c: (b, 0, c)),
            pl.BlockSpec((None, S, C), lambda b, c: (b, 0, c)),
            pl.BlockSpec((CONV_WIDTH, C), lambda b, c: (0, c)),
            pl.BlockSpec((1, C), lambda b, c: (0, c)),
            pl.BlockSpec((None, C, 4 * C), lambda b, c: (c, 0, 0)),
            pl.BlockSpec((None, 1, 4 * C), lambda b, c: (c, 0, 0)),
            pl.BlockSpec((2, C), lambda b, c: (0, c)),
        ],
        out_specs=pl.BlockSpec((None, S, C), lambda b, c: (b, 0, c)),
        out_shape=jax.ShapeDtypeStruct((B, S, D_LRU), F32),
        scratch_shapes=[seg_buf] * 8 + [carry_buf] * 2,
        compiler_params=pltpu.CompilerParams(
            dimension_semantics=("parallel", "parallel"), vmem_limit_bytes=VMEM_LIMIT),
        name="rg_lru",
    )(xl3, yl3, conv_w, conv_b, wg, bg, lam)
    return out.reshape(B * S, D_LRU)


def _router_gates(logits):
    lane = lax.broadcasted_iota(jnp.int32, logits.shape, 1).astype(F32)
    neg = -jnp.inf
    big = float(2 * LANES)
    gl = jnp.where(lane < N_GROUPS, logits, neg)
    gmax = jnp.max(gl, axis=1, keepdims=True)
    gidx = jnp.min(jnp.where(gl == gmax, lane, big), axis=1, keepdims=True)
    g_sel_prob = 1.0 / jnp.sum(jnp.exp(gl - gmax), axis=1, keepdims=True)
    lo = EXPERT_LANE0 + EXPERTS_PER_GROUP * gidx
    el = jnp.where((lane >= lo) & (lane < lo + EXPERTS_PER_GROUP), logits, neg)
    v1 = jnp.max(el, axis=1, keepdims=True)
    i1 = jnp.min(jnp.where(el == v1, lane, big), axis=1, keepdims=True)
    el2 = jnp.where(lane == i1, neg, el)
    v2 = jnp.max(el2, axis=1, keepdims=True)
    i2 = jnp.min(jnp.where(el2 == v2, lane, big), axis=1, keepdims=True)
    e2 = jnp.exp(v2 - v1)
    w1 = g_sel_prob / (1.0 + e2)
    w2 = g_sel_prob * e2 / (1.0 + e2)
    gates = jnp.where(lane == i1, w1, jnp.where(lane == i2, w2, 0.0))
    return jnp.where(lane == 0.0, gidx, gates)


def _out_proj_kernel(x_ref, attn_ref, lru_ref, lng_ref, lnb_ref, ag_ref, lg_ref, wo_ref,
                     g1_ref, b1_ref, wrh_ref, wrl_ref, br_ref, h1_ref, gate_ref):
    h = _layer_norm(x_ref[...], lng_ref[...], lnb_ref[...])
    merged = jnp.concatenate(
        [_rms_norm_rows(attn_ref[...], ag_ref[...], LN_EPS),
         _rms_norm_rows(lru_ref[...], lg_ref[...], LN_EPS)], axis=1).astype(BF16)
    mix = jnp.dot(merged, wo_ref[...], preferred_element_type=F32)
    h1 = _layer_norm(DEEPNORM_ALPHA * h + mix, g1_ref[...], b1_ref[...])
    h1_ref[...] = h1
    hi = h1.astype(BF16)
    lo = (h1 - hi.astype(F32)).astype(BF16)
    wrh = wrh_ref[...]
    logits = (jnp.dot(hi, wrh, preferred_element_type=F32)
              + jnp.dot(lo, wrh, preferred_element_type=F32)
              + jnp.dot(hi, wrl_ref[...], preferred_element_type=F32)) + br_ref[...]
    gate_ref[...] = _router_gates(logits)


def _out_proj(x2, attn, lru, ln_g, ln_b, ag, lg, wo, g1, b1, wrh, wrl, br):
    T = x2.shape[0]
    tm = TM_PROJ
    const = lambda i: (0, 0)
    row = lambda i: (i, 0)
    return pl.pallas_call(
        _out_proj_kernel,
        grid=(T // tm,),
        in_specs=[
            pl.BlockSpec((tm, D_MODEL), row),
            pl.BlockSpec((tm, D_ATTN), row),
            pl.BlockSpec((tm, D_LRU), row),
            pl.BlockSpec((1, D_MODEL), const),
            pl.BlockSpec((1, D_MODEL), const),
            pl.BlockSpec((1, D_ATTN), const),
            pl.BlockSpec((1, D_LRU), const),
            pl.BlockSpec((D_MODEL, D_MODEL), const),
            pl.BlockSpec((1, D_MODEL), const),
            pl.BlockSpec((1, D_MODEL), const),
            pl.BlockSpec((D_MODEL, ROUTER_PAD), const),
            pl.BlockSpec((D_MODEL, ROUTER_PAD), const),
            pl.BlockSpec((1, ROUTER_PAD), const),
        ],
        out_specs=[pl.BlockSpec((tm, D_MODEL), row), pl.BlockSpec((tm, ROUTER_PAD), row)],
        out_shape=[jax.ShapeDtypeStruct((T, D_MODEL), F32),
                   jax.ShapeDtypeStruct((T, ROUTER_PAD), F32)],
        compiler_params=pltpu.CompilerParams(
            dimension_semantics=("parallel",), vmem_limit_bytes=VMEM_LIMIT),
        name="out_proj",
    )(x2, attn, lru, ln_g, ln_b, ag, lg, wo, g1, b1, wrh, wrl, br)


def _lane_scalar(row, idx):
    lane = lax.broadcasted_iota(jnp.int32, row.shape, 1)
    return jnp.sum(jnp.where(lane == idx, row, 0.0))


def _moe_kernel(h1_ref, route_ref, tril_ref, w13_ref, w2_ref, g2_ref, b2_ref, o_ref,
                pt_ref, xs_ref, gs_ref, ys_ref):
    tm = h1_ref.shape[0]
    c = MOE_CHUNK
    h1 = h1_ref[...]
    route = route_ref[...]
    lane = lax.broadcasted_iota(jnp.int32, route.shape, 1)
    gid = jnp.sum(jnp.where(lane == 0, route, 0.0), axis=1, keepdims=True)
    onehot = jnp.where(lane.astype(F32) == gid, 1.0, 0.0)
    half = tm // 2
    ohb = onehot.astype(BF16)
    tril = tril_ref[...]
    cnt_top = jnp.sum(onehot[:half], axis=0, keepdims=True)
    before = jnp.concatenate(
        [jnp.dot(tril, ohb[:half], preferred_element_type=F32),
         jnp.dot(tril, ohb[half:], preferred_element_type=F32) + cnt_top], axis=0)
    cnt = jnp.broadcast_to(cnt_top + jnp.sum(onehot[half:], axis=0, keepdims=True), (SUBLANES, LANES))
    start = pltpu.roll(cnt, 1, axis=1) + pltpu.roll(cnt, 2, axis=1) + pltpu.roll(cnt, 3, axis=1)
    start = start[0:1, :]
    dest = jnp.sum(onehot * (before + start), axis=1, keepdims=True)
    col = lax.broadcasted_iota(jnp.int32, (tm, tm), 1).astype(F32)
    pt_ref[...] = jnp.where(col == dest, 1.0, 0.0).astype(BF16)
    dest_row = jnp.broadcast_to(dest, (tm, LANES)).T[0:1, :]
    row = lax.broadcasted_iota(jnp.int32, (tm, tm), 0).astype(F32)
    p = jnp.where(row == dest_row, 1.0, 0.0).astype(BF16)

    xs_ref[...] = jnp.dot(p, h1.astype(BF16), preferred_element_type=F32).astype(BF16)
    rhi = route.astype(BF16)
    rlo = (route - rhi.astype(F32)).astype(BF16)
    gs2 = jnp.dot(p, jnp.concatenate([rhi, rlo], axis=1), preferred_element_type=F32)
    gs_ref[...] = gs2[:, :LANES] + gs2[:, LANES:]
    ys_ref[...] = jnp.zeros_like(ys_ref)

    bounds = [jnp.int32(0)] + [_lane_scalar(start, g).astype(jnp.int32) for g in range(1, N_GROUPS)]
    bounds.append(jnp.int32(tm))
    for j in range(tm // c):
        rows = slice(j * c, (j + 1) * c)
        for g in range(N_GROUPS):
            @pl.when((bounds[g] < (j + 1) * c) & (bounds[g + 1] > j * c))
            def _():
                xj = xs_ref[rows, :]
                gsj = gs_ref[rows, :]
                lane_c = lax.broadcasted_iota(jnp.int32, gsj.shape, 1)
                hids = []
                for k in range(EXPERTS_PER_GROUP):
                    e = g * EXPERTS_PER_GROUP + k
                    up = jnp.dot(xj, w13_ref[e], preferred_element_type=F32)
                    u1 = up[:, :D_EXPERT]
                    ge = jnp.sum(jnp.where(lane_c == EXPERT_LANE0 + e, gsj, 0.0), axis=1, keepdims=True)
                    hids.append(((u1 * jax.nn.sigmoid(u1)) * up[:, D_EXPERT:] * ge).astype(BF16))
                ys_ref[rows, :] += jnp.dot(jnp.concatenate(hids, axis=1), w2_ref[g],
                                           preferred_element_type=F32)

    ffn = jnp.dot(pt_ref[...], ys_ref[...].astype(BF16), preferred_element_type=F32)
    o_ref[...] = _layer_norm(DEEPNORM_ALPHA * h1 + ffn, g2_ref[...], b2_ref[...])


def _moe(h1, route, tril, w13, w2, g2, b2):
    T = h1.shape[0]
    tm = TM_MOE
    gw = EXPERTS_PER_GROUP * D_EXPERT
    const2 = lambda i: (0, 0)
    const3 = lambda i: (0, 0, 0)
    resident = pl.Buffered(1)
    return pl.pallas_call(
        _moe_kernel,
        grid=(T // tm,),
        in_specs=[
            pl.BlockSpec((tm, D_MODEL), lambda i: (i, 0)),
            pl.BlockSpec((tm, ROUTER_PAD), lambda i: (i, 0)),
            pl.BlockSpec((tm // 2, tm // 2), const2, pipeline_mode=resident),
            pl.BlockSpec((N_EXPERTS, D_MODEL, 2 * D_EXPERT), const3, pipeline_mode=resident),
            pl.BlockSpec((N_GROUPS, gw, D_MODEL), const3, pipeline_mode=resident),
            pl.BlockSpec((1, D_MODEL), const2),
            pl.BlockSpec((1, D_MODEL), const2),
        ],
        out_specs=pl.BlockSpec((tm, D_MODEL), lambda i: (i, 0)),
        out_shape=jax.ShapeDtypeStruct((T, D_MODEL), F32),
        scratch_shapes=[pltpu.VMEM((tm, tm), BF16), pltpu.VMEM((tm, D_MODEL), BF16),
                        pltpu.VMEM((tm, ROUTER_PAD), F32), pltpu.VMEM((tm, D_MODEL), F32)],
        compiler_params=pltpu.CompilerParams(
            dimension_semantics=("parallel",), vmem_limit_bytes=VMEM_LIMIT),
        name="moe",
    )(h1, route, tril, w13, w2, g2, b2)


def _rope_tables(S):
    t = jnp.arange(S, dtype=jnp.int32)
    row = (t // GRID_W).astype(F32)
    col = (t % GRID_W).astype(F32)
    half = ROPE_SECTION // 2
    inv_freq = ROPE_THETA ** (-jnp.arange(half, dtype=F32) / half)
    ang_r = row[:, None] * inv_freq
    ang_c = col[:, None] * inv_freq
    cos_h = jnp.concatenate([jnp.cos(ang_r)] * 2 + [jnp.cos(ang_c)] * 2, axis=1)
    sin_h = jnp.concatenate([-jnp.sin(ang_r), jnp.sin(ang_r), -jnp.sin(ang_c), jnp.sin(ang_c)], axis=1)
    reps = KV_WIDTH // HEAD_DIM
    return jnp.tile(cos_h, (1, reps)), jnp.tile(sin_h, (1, reps))


def _block_diag(w):
    H, d, _ = w.shape
    eye = jnp.eye(H, dtype=w.dtype)
    return (eye[:, None, :, None] * w[:, :, None, :]).reshape(H * d, H * d)


def _lru_gate_weights(wa, ba, wx, bx):
    C = LRU_CH
    ncg = D_LRU // C
    hpg = C // LRU_HEAD_DIM
    ws, bs = [], []
    for c in range(ncg):
        hs = slice(c * hpg, (c + 1) * hpg)
        cols, bias = [], []
        for d in range(2):
            cols += [_block_diag(wa[d, hs]), _block_diag(wx[d, hs])]
            bias += [ba[d, hs].reshape(C), bx[d, hs].reshape(C)]
        ws.append(jnp.concatenate(cols, axis=1))
        bs.append(jnp.concatenate(bias)[None, :])
    return jnp.stack(ws).astype(BF16), jnp.stack(bs)


def _trunk(x, P):
    B, S, _ = x.shape
    x2 = x.reshape(B * S, D_MODEL)
    q, k, v, xl, yl = _in_proj(x2, B, S, P["ln_in_g"], P["ln_in_b"], P["wq"], P["wkv"], P["wx"],
                               P["wy"], P["seg"], P["qg"], P["kg"], P["cos"], P["sin"])
    attn = _attention(q, k, v, B, S)
    lru = _rg_lru(xl, yl, B, S, P["conv_w"], P["conv_b"], P["wg"], P["bg"], P["lam"])
    h1, route = _out_proj(x2, attn, lru, P["ln_in_g"], P["ln_in_b"], P["ag"], P["lg"], P["wo"],
                          P["g1"], P["b1"], P["wrh"], P["wrl"], P["br"])
    out = _moe(h1, route, P["tril"], P["w13"], P["w2"], P["g2"], P["b2"])
    return out.reshape(B, S, D_MODEL)


def kernel(x_prompt, x_sample, ln_in_g, ln_in_b, w_in, conv_w, conv_b, lru_wa, lru_ba, lru_wx, lru_bx, lru_lambda, q_norm_g, k_norm_g, attn_out_g, lru_out_g, w_out, ln1_g, ln1_b, router_wg, router_bg, router_we, router_be, exp_w1, exp_w3, exp_w2, ln2_g, ln2_b):
    assert w_in.shape[0] == DEPTH == 1
    S = x_prompt.shape[1]
    l = 0
    w = w_in[l]
    c0, c1, c2 = D_ATTN, D_ATTN + 2 * KV_WIDTH, D_ATTN + 2 * KV_WIDTH + D_LRU
    seg = _block_diag(jnp.full((N_Q_HEADS, HEAD_DIM, HEAD_DIM), 1.0 / HEAD_DIM, F32)).astype(BF16)
    cos_t, sin_t = _rope_tables(S)
    wg, bg = _lru_gate_weights(lru_wa[l], lru_ba[l], lru_wx[l], lru_bx[l])
    wr = jnp.concatenate([router_wg[l], router_we[l].reshape(D_MODEL, N_EXPERTS)], axis=1)
    wr = jnp.pad(wr, ((0, 0), (0, ROUTER_PAD - wr.shape[1])))
    wrh = wr.astype(BF16)
    wrl = (wr - wrh.astype(F32)).astype(BF16)
    br = jnp.concatenate([router_bg[l], router_be[l].reshape(N_EXPERTS)])
    br = jnp.pad(br, (0, ROUTER_PAD - br.shape[0]))[None, :]
    P = dict(
        ln_in_g=ln_in_g[None, :], ln_in_b=ln_in_b[None, :],
        wq=w[:, :c0].astype(BF16), wkv=w[:, c0:c1].astype(BF16),
        wx=w[:, c1:c2].astype(BF16), wy=w[:, c2:].astype(BF16),
        seg=seg, qg=jnp.tile(q_norm_g[l], N_Q_HEADS)[None, :],
        kg=jnp.tile(k_norm_g[l], N_KV_HEADS)[None, :], cos=cos_t, sin=sin_t,
        conv_w=conv_w[l], conv_b=conv_b[l][None, :], wg=wg, bg=bg, lam=lru_lambda[l],
        ag=attn_out_g[l][None, :], lg=lru_out_g[l][None, :], wo=w_out[l].astype(BF16),
        g1=ln1_g[l][None, :], b1=ln1_b[l][None, :], wrh=wrh, wrl=wrl, br=br,
        w13=jnp.concatenate([exp_w1[l], exp_w3[l]], axis=2).astype(BF16),
        w2=exp_w2[l].reshape(N_GROUPS, EXPERTS_PER_GROUP * D_EXPERT, D_MODEL).astype(BF16),
        g2=ln2_g[l][None, :], b2=ln2_b[l][None, :],
        tril=jnp.tril(jnp.ones((TM_MOE // 2, TM_MOE // 2), F32), -1).astype(BF16),
    )
    return (_trunk(x_prompt, P), _trunk(x_sample, P))
```

```python
import functools
import math

import jax
import jax.numpy as jnp
from jax import lax
from jax.experimental import pallas as pl
from jax.experimental.pallas import tpu as pltpu

F32 = jnp.float32
BF16 = jnp.bfloat16

D_MODEL = 1024
GRID_W = 64
D_ATTN = 512
HEAD_DIM = 64
N_Q_HEADS = 8
N_KV_HEADS = 2
Q_PER_KV = 4
KV_WIDTH = 128
ROPE_THETA = 10000.0
ROPE_SECTION = 32
QK_EPS = 1e-6
D_LRU = 512
N_LRU_HEADS = 8
LRU_HEAD_DIM = 64
CONV_WIDTH = 4
LRU_C = 8.0
N_GROUPS = 4
EXPERTS_PER_GROUP = 4
N_EXPERTS = 16
D_EXPERT = 256
LN_EPS = 1e-5
DEPTH = 1
DEEPNORM_ALPHA = (2.0 * DEPTH) ** 0.25
LOG2_E = math.log2(math.e)

LANES = 128
SUBLANES = 8
VMEM_LIMIT = 48 * 1024 * 1024

TM_IN = 1024
IN_PROJ_SPLIT = 4
TM_OUT = 1024
TQ = 1024
KEY_BLOCK = 1024
LRU_CH = 256
SEG = 256
LRU_SCAN_UNROLL = 4
LRU_COMBINE_ROWS = 128
TM_MOE = 512
MOE_CHUNK = 128
ROUTER_PAD = LANES
EXPERT_LANE0 = N_GROUPS


def _layer_norm(x, g, b):
    mu = jnp.mean(x, axis=-1, keepdims=True)
    xc = x - mu
    var = jnp.mean(xc * xc, axis=-1, keepdims=True)
    return xc * lax.rsqrt(var + LN_EPS) * g + b


def _rms_norm_rows(x, g, eps):
    return x * lax.rsqrt(jnp.mean(x * x, axis=-1, keepdims=True) + eps) * g


def _head_norm_rope(xf, seg, g, cos, sin):
    ms = jnp.dot((xf * xf).astype(BF16), seg, preferred_element_type=F32)
    xn = xf * lax.rsqrt(ms + QK_EPS) * g
    n = xn.shape[1]
    half = ROPE_SECTION // 2
    up = pltpu.roll(xn, n - half, axis=1)
    dn = pltpu.roll(xn, half, axis=1)
    lane = lax.broadcasted_iota(jnp.int32, xn.shape, 1)
    first = (lane % ROPE_SECTION) < half
    partner = jnp.where(first, up, dn)
    return xn * cos + partner * sin


def _in_proj_kernel(x_ref, g_ref, b_ref, wq_ref, wkv_ref, wx_ref, wy_ref, seg_ref,
                    qg_ref, kg_ref, cos_ref, sin_ref,
                    q_ref, k_ref, v_ref, xl_ref, yl_ref):
    reps = D_ATTN // KV_WIDTH
    blk = x_ref.shape[0] // IN_PROJ_SPLIT
    for i in range(IN_PROJ_SPLIT):
        rows = slice(i * blk, (i + 1) * blk)
        h = _layer_norm(x_ref[rows, :], g_ref[...], b_ref[...])
        hb = h.astype(BF16)
        cos = cos_ref[rows, :]
        sin = sin_ref[rows, :]

        qf = jnp.dot(hb, wq_ref[...], preferred_element_type=F32)
        q = _head_norm_rope(qf, seg_ref[...], qg_ref[...],
                            jnp.concatenate([cos] * reps, axis=1),
                            jnp.concatenate([sin] * reps, axis=1))
        q_ref[rows, :] = (q * (HEAD_DIM ** -0.5 * LOG2_E)).astype(BF16)

        kvf = jnp.dot(hb, wkv_ref[...], preferred_element_type=F32)
        k = _head_norm_rope(kvf[:, :KV_WIDTH], seg_ref[:KV_WIDTH, :KV_WIDTH], kg_ref[...], cos, sin)
        kb = k.astype(BF16)
        vb = kvf[:, KV_WIDTH:].astype(BF16)
        for j in range(N_KV_HEADS):
            k_ref[j, rows, :] = kb[:, j * HEAD_DIM:(j + 1) * HEAD_DIM]
            v_ref[j, rows, :] = vb[:, j * HEAD_DIM:(j + 1) * HEAD_DIM]

        xl_ref[rows, :] = jnp.dot(hb, wx_ref[...], preferred_element_type=F32)
        yl_ref[rows, :] = jnp.dot(hb, wy_ref[...], preferred_element_type=F32)


def _in_proj(x2, B, S, ln_g, ln_b, wq, wkv, wx, wy, seg, qg, kg, cos_t, sin_t):
    T = B * S
    tm = TM_IN
    nt = S // tm
    const = lambda i: (0, 0)
    return pl.pallas_call(
        _in_proj_kernel,
        grid=(T // tm,),
        in_specs=[
            pl.BlockSpec((tm, D_MODEL), lambda i: (i, 0)),
            pl.BlockSpec((1, D_MODEL), const),
            pl.BlockSpec((1, D_MODEL), const),
            pl.BlockSpec((D_MODEL, D_ATTN), const),
            pl.BlockSpec((D_MODEL, 2 * KV_WIDTH), const),
            pl.BlockSpec((D_MODEL, D_LRU), const),
            pl.BlockSpec((D_MODEL, D_LRU), const),
            pl.BlockSpec((D_ATTN, D_ATTN), const),
            pl.BlockSpec((1, D_ATTN), const),
            pl.BlockSpec((1, KV_WIDTH), const),
            pl.BlockSpec((tm, KV_WIDTH), lambda i: (i % nt, 0)),
            pl.BlockSpec((tm, KV_WIDTH), lambda i: (i % nt, 0)),
        ],
        out_specs=[
            pl.BlockSpec((tm, D_ATTN), lambda i: (i, 0)),
            pl.BlockSpec((None, N_KV_HEADS, tm, HEAD_DIM), lambda i: (i // nt, 0, i % nt, 0)),
            pl.BlockSpec((None, N_KV_HEADS, tm, HEAD_DIM), lambda i: (i // nt, 0, i % nt, 0)),
            pl.BlockSpec((tm, D_LRU), lambda i: (i, 0)),
            pl.BlockSpec((tm, D_LRU), lambda i: (i, 0)),
        ],
        out_shape=[
            jax.ShapeDtypeStruct((T, D_ATTN), BF16),
            jax.ShapeDtypeStruct((B, N_KV_HEADS, S, HEAD_DIM), BF16),
            jax.ShapeDtypeStruct((B, N_KV_HEADS, S, HEAD_DIM), BF16),
            jax.ShapeDtypeStruct((T, D_LRU), F32),
            jax.ShapeDtypeStruct((T, D_LRU), F32),
        ],
        compiler_params=pltpu.CompilerParams(
            dimension_semantics=("parallel",), vmem_limit_bytes=VMEM_LIMIT),
        name="in_proj",
    )(x2, ln_g, ln_b, wq, wkv, wx, wy, seg, qg, kg, cos_t, sin_t)


def _attn_kernel(q_ref, k_ref, v_ref, o_ref):
    nblk = k_ref.shape[0] // KEY_BLOCK
    outs = []
    for h in range(Q_PER_KV):
        qh = q_ref[:, h * HEAD_DIM:(h + 1) * HEAD_DIM]
        m = l = acc = None
        for b in range(nblk):
            kb = k_ref[b * KEY_BLOCK:(b + 1) * KEY_BLOCK, :]
            vb = v_ref[b * KEY_BLOCK:(b + 1) * KEY_BLOCK, :]
            s = lax.dot_general(qh, kb, (((1,), (1,)), ((), ())), preferred_element_type=F32)
            mb = jnp.max(s, axis=1, keepdims=True)
            if b == 0:
                m = mb
                p = jnp.exp2(s - m)
                l = jnp.sum(p, axis=1, keepdims=True)
                acc = jnp.dot(p.astype(BF16), vb, preferred_element_type=F32)
            else:
                m_new = jnp.maximum(m, mb)
                alpha = jnp.exp2(m - m_new)
                p = jnp.exp2(s - m_new)
                l = alpha * l + jnp.sum(p, axis=1, keepdims=True)
                acc = alpha * acc + jnp.dot(p.astype(BF16), vb, preferred_element_type=F32)
                m = m_new
        outs.append(acc / l)
    o_ref[...] = jnp.concatenate(outs, axis=1)


def _attention(q, k, v, B, S):
    T = B * S
    nq = S // TQ
    width = Q_PER_KV * HEAD_DIM
    return pl.pallas_call(
        _attn_kernel,
        grid=(B, N_KV_HEADS, nq),
        in_specs=[
            pl.BlockSpec((TQ, width), lambda b, j, t: (b * nq + t, j)),
            pl.BlockSpec((None, None, S, HEAD_DIM), lambda b, j, t: (b, j, 0, 0)),
            pl.BlockSpec((None, None, S, HEAD_DIM), lambda b, j, t: (b, j, 0, 0)),
        ],
        out_specs=pl.BlockSpec((TQ, width), lambda b, j, t: (b * nq + t, j)),
        out_shape=jax.ShapeDtypeStruct((T, D_ATTN), F32),
        compiler_params=pltpu.CompilerParams(
            dimension_semantics=("parallel", "parallel", "parallel"),
            vmem_limit_bytes=VMEM_LIMIT),
        name="attention",
    )(q, k, v)


def _gelu_tanh(x):
    c = math.sqrt(2.0 / math.pi)
    return 0.5 * x * (1.0 + jnp.tanh(c * (x + 0.044715 * (x * x * x))))


def _rg_lru_kernel(x_ref, y_ref, cw_ref, cb_ref, wg_ref, bg_ref, lam_ref, o_ref,
                   af_ref, uf_ref, ab_ref, ub_ref, hf_ref, pf_ref, hb_ref, pb_ref, cf_ref, cbk_ref):
    S, C = x_ref.shape
    nslab = C // LANES
    nseg = S // SEG
    assert nseg == SUBLANES
    lam = lam_ref[...]
    neg_c_softplus = -LRU_C * jnp.logaddexp(-lam, 0.0)
    cw = cw_ref[...]
    cb = cb_ref[...]
    bg = bg_ref[...]

    def gate_block(s, carry):
        start = pl.multiple_of(s * SEG, SEG)
        seg_rows = pl.ds(s, SEG, stride=SUBLANES)
        cur = x_ref[pl.ds(start, SEG), :]
        prev = x_ref[pl.ds(pl.multiple_of(jnp.maximum(start - SUBLANES, 0), SUBLANES), SUBLANES), :]
        prev = jnp.where(s > 0, prev, 0.0)
        nxt = x_ref[pl.ds(pl.multiple_of(jnp.minimum(start + SEG, S - SUBLANES), SUBLANES), SUBLANES), :]
        nxt = jnp.where(s < nseg - 1, nxt, 0.0)
        ext = jnp.concatenate([prev, cur, nxt], axis=0)
        xc = cb
        for tap in range(CONV_WIDTH):
            off = SUBLANES + tap - CONV_WIDTH // 2
            xc = xc + ext[off:off + SEG, :] * cw[tap:tap + 1, :]
        z = jnp.dot(xc.astype(BF16), wg_ref[...], preferred_element_type=F32) + bg
        for d, (a_ref, u_ref) in enumerate(((af_ref, uf_ref), (ab_ref, ub_ref))):
            rg = jax.nn.sigmoid(z[:, (2 * d) * C:(2 * d + 1) * C])
            ig = jax.nn.sigmoid(z[:, (2 * d + 1) * C:(2 * d + 2) * C])
            a = jnp.exp(neg_c_softplus[d:d + 1, :] * rg)
            u = jnp.sqrt(1.0 - a * a) * (ig * xc)
            for slab in range(nslab):
                lanes = slice(slab * LANES, (slab + 1) * LANES)
                a_ref[slab, seg_rows, :] = a[:, lanes]
                u_ref[slab, seg_rows, :] = u[:, lanes]
        return carry

    lax.fori_loop(0, nseg, gate_block, 0)

    def scan_step(j, carry):
        out = []
        for slab in range(nslab):
            hf, pf, hb, pb = carry[slab]
            rows = pl.ds(pl.multiple_of(j * SUBLANES, SUBLANES), SUBLANES)
            a = af_ref[slab, rows, :]
            hf = a * hf + uf_ref[slab, rows, :]
            pf = a * pf
            hf_ref[slab, rows, :] = hf
            pf_ref[slab, rows, :] = pf
            rows = pl.ds(pl.multiple_of((SEG - 1 - j) * SUBLANES, SUBLANES), SUBLANES)
            a = ab_ref[slab, rows, :]
            hb = a * hb + ub_ref[slab, rows, :]
            pb = a * pb
            hb_ref[slab, rows, :] = hb
            pb_ref[slab, rows, :] = pb
            out.append((hf, pf, hb, pb))
        return tuple(out)

    zero = jnp.zeros((SUBLANES, LANES), F32)
    one = jnp.ones((SUBLANES, LANES), F32)
    final = lax.fori_loop(0, SEG, scan_step, tuple((zero, one, zero, one) for _ in range(nslab)),
                          unroll=LRU_SCAN_UNROLL)

    for slab in range(nslab):
        hf, pf, hb, pb = final[slab]
        c = jnp.zeros((1, LANES), F32)
        fwd = [c]
        for sgm in range(1, nseg):
            c = hf[sgm - 1:sgm, :] + pf[sgm - 1:sgm, :] * c
            fwd.append(c)
        c = jnp.zeros((1, LANES), F32)
        bwd = [c]
        for sgm in range(nseg - 2, -1, -1):
            c = hb[sgm + 1:sgm + 2, :] + pb[sgm + 1:sgm + 2, :] * c
            bwd.append(c)
        cf_ref[slab] = jnp.concatenate(fwd, axis=0)
        cbk_ref[slab] = jnp.concatenate(bwd[::-1], axis=0)

    reps = LRU_COMBINE_ROWS // SUBLANES

    def combine(i, carry):
        rows = pl.ds(pl.multiple_of(i * LRU_COMBINE_ROWS, LRU_COMBINE_ROWS), LRU_COMBINE_ROWS)
        for slab in range(nslab):
            cf = jnp.concatenate([cf_ref[slab]] * reps, axis=0)
            cbk = jnp.concatenate([cbk_ref[slab]] * reps, axis=0)
            uf_ref[slab, rows, :] = ((hf_ref[slab, rows, :] + pf_ref[slab, rows, :] * cf)
                                     + (hb_ref[slab, rows, :] + pb_ref[slab, rows, :] * cbk))
        return carry

    lax.fori_loop(0, S // LRU_COMBINE_ROWS, combine, 0)

    def out_block(s, carry):
        start = pl.multiple_of(s * SEG, SEG)
        for slab in range(nslab):
            lanes = slice(slab * LANES, (slab + 1) * LANES)
            h = uf_ref[slab, pl.ds(s, SEG, stride=SUBLANES), :]
            o_ref[pl.ds(start, SEG), lanes] = h * _gelu_tanh(y_ref[pl.ds(start, SEG), lanes])
        return carry

    lax.fori_loop(0, nseg, out_block, 0)


def _rg_lru(xl, yl, B, S, conv_w, conv_b, wg, bg, lam):
    C = LRU_CH
    ncg = D_LRU // C
    nslab = C // LANES
    xl3 = xl.reshape(B, S, D_LRU)
    yl3 = yl.reshape(B, S, D_LRU)
    seg_buf = pltpu.VMEM((nslab, S, LANES), F32)
    carry_buf = pltpu.VMEM((nslab, S // SEG, LANES), F32)
    out = pl.pallas_call(
        _rg_lru_kernel,
        grid=(B, ncg),
        in_specs=[
            pl.BlockSpec((None, S, C), lambda b, c: (b, 0, c)),
            pl.BlockSpec((None, S, C), lambda b, c: (b, 0, c)),
            pl.BlockSpec((CONV_WIDTH, C), lambda b, c: (0, c)),
            pl.BlockSpec((1, C), lambda b, c: (0, c)),
            pl.BlockSpec((None, C, 4 * C), lambda b, c: (c, 0, 0)),
            pl.BlockSpec((None, 1, 4 * C), lambda b, c: (c, 0, 0)),
            pl.BlockSpec((2, C), lambda b, c: (0, c)),
        ],
        out_specs=pl.BlockSpec((None, S, C), lambda b, c: (b, 0, c)),
        out_shape=jax.ShapeDtypeStruct((B, S, D_LRU), F32),
        scratch_shapes=[seg_buf] * 8 + [carry_buf] * 2,
        compiler_params=pltpu.CompilerParams(
            dimension_semantics=("parallel", "parallel"), vmem_limit_bytes=VMEM_LIMIT),
        name="rg_lru",
    )(xl3, yl3, conv_w, conv_b, wg, bg, lam)
    return out.reshape(B * S, D_LRU)


def _router_gates(logits):
    lane = lax.broadcasted_iota(jnp.int32, logits.shape, 1).astype(F32)
    neg = -jnp.inf
    big = float(2 * LANES)
    gl = jnp.where(lane < N_GROUPS, logits, neg)
    gmax = jnp.max(gl, axis=1, keepdims=True)
    gidx = jnp.min(jnp.where(gl == gmax, lane, big), axis=1, keepdims=True)
    g_sel_prob = 1.0 / jnp.sum(jnp.exp(gl - gmax), axis=1, keepdims=True)
    lo = EXPERT_LANE0 + EXPERTS_PER_GROUP * gidx
    el = jnp.where((lane >= lo) & (lane < lo + EXPERTS_PER_GROUP), logits, neg)
    v1 = jnp.max(el, axis=1, keepdims=True)
    i1 = jnp.min(jnp.where(el == v1, lane, big), axis=1, keepdims=True)
    el2 = jnp.where(lane == i1, neg, el)
    v2 = jnp.max(el2, axis=1, keepdims=True)
    i2 = jnp.min(jnp.where(el2 == v2, lane, big), axis=1, keepdims=True)
    e2 = jnp.exp(v2 - v1)
    w1 = g_sel_prob / (1.0 + e2)
    w2 = g_sel_prob * e2 / (1.0 + e2)
    gates = jnp.where(lane == i1, w1, jnp.where(lane == i2, w2, 0.0))
    return jnp.where(lane == 0.0, gidx, gates)


def _out_proj_kernel(x_ref, attn_ref, lru_ref, lng_ref, lnb_ref, ag_ref, lg_ref, wo_ref,
                     g1_ref, b1_ref, wrh_ref, wrl_ref, br_ref, h1_ref, gate_ref):
    h = _layer_norm(x_ref[...], lng_ref[...], lnb_ref[...])
    merged = jnp.concatenate(
        [_rms_norm_rows(attn_ref[...], ag_ref[...], LN_EPS),
         _rms_norm_rows(lru_ref[...], lg_ref[...], LN_EPS)], axis=1).astype(BF16)
    mix = jnp.dot(merged, wo_ref[...], preferred_element_type=F32)
    h1 = _layer_norm(DEEPNORM_ALPHA * h + mix, g1_ref[...], b1_ref[...])
    h1_ref[...] = h1
    hi = h1.astype(BF16)
    lo = (h1 - hi.astype(F32)).astype(BF16)
    wrh = wrh_ref[...]
    logits = (jnp.dot(hi, wrh, preferred_element_type=F32)
              + jnp.dot(lo, wrh, preferred_element_type=F32)
              + jnp.dot(hi, wrl_ref[...], preferred_element_type=F32)) + br_ref[...]
    gate_ref[...] = _router_gates(logits)


def _out_proj(x2, attn, lru, ln_g, ln_b, ag, lg, wo, g1, b1, wrh, wrl, br):
    T = x2.shape[0]
    tm = TM_OUT
    const = lambda i: (0, 0)
    row = lambda i: (i, 0)
    return pl.pallas_call(
        _out_proj_kernel,
        grid=(T // tm,),
        in_specs=[
            pl.BlockSpec((tm, D_MODEL), row),
            pl.BlockSpec((tm, D_ATTN), row),
            pl.BlockSpec((tm, D_LRU), row),
            pl.BlockSpec((1, D_MODEL), const),
            pl.BlockSpec((1, D_MODEL), const),
            pl.BlockSpec((1, D_ATTN), const),
            pl.BlockSpec((1, D_LRU), const),
            pl.BlockSpec((D_MODEL, D_MODEL), const),
            pl.BlockSpec((1, D_MODEL), const),
            pl.BlockSpec((1, D_MODEL), const),
            pl.BlockSpec((D_MODEL, ROUTER_PAD), const),
            pl.BlockSpec((D_MODEL, ROUTER_PAD), const),
            pl.BlockSpec((1, ROUTER_PAD), const),
        ],
        out_specs=[pl.BlockSpec((tm, D_MODEL), row), pl.BlockSpec((tm, ROUTER_PAD), row)],
        out_shape=[jax.ShapeDtypeStruct((T, D_MODEL), F32),
                   jax.ShapeDtypeStruct((T, ROUTER_PAD), F32)],
        compiler_params=pltpu.CompilerParams(
            dimension_semantics=("parallel",), vmem_limit_bytes=VMEM_LIMIT),
        name="out_proj",
    )(x2, attn, lru, ln_g, ln_b, ag, lg, wo, g1, b1, wrh, wrl, br)


def _lane_scalar(row, idx):
    lane = lax.broadcasted_iota(jnp.int32, row.shape, 1)
    return jnp.sum(jnp.where(lane == idx, row, 0.0))


def _moe_kernel(h1_ref, route_ref, tril_ref, w13_ref, w2_ref, g2_ref, b2_ref, o_ref,
                pt_ref, xs_ref, gs_ref, ys_ref):
    tm = h1_ref.shape[0]
    c = MOE_CHUNK
    h1 = h1_ref[...]
    route = route_ref[...]
    lane = lax.broadcasted_iota(jnp.int32, route.shape, 1)
    gid = jnp.sum(jnp.where(lane == 0, route, 0.0), axis=1, keepdims=True)
    onehot = jnp.where(lane.astype(F32) == gid, 1.0, 0.0)
    half = tm // 2
    ohb = onehot.astype(BF16)
    tril = tril_ref[...]
    cnt_top = jnp.sum(onehot[:half], axis=0, keepdims=True)
    before = jnp.concatenate(
        [jnp.dot(tril, ohb[:half], preferred_element_type=F32),
         jnp.dot(tril, ohb[half:], preferred_element_type=F32) + cnt_top], axis=0)
    cnt = jnp.broadcast_to(cnt_top + jnp.sum(onehot[half:], axis=0, keepdims=True), (SUBLANES, LANES))
    start = pltpu.roll(cnt, 1, axis=1) + pltpu.roll(cnt, 2, axis=1) + pltpu.roll(cnt, 3, axis=1)
    start = start[0:1, :]
    dest = jnp.sum(onehot * (before + start), axis=1, keepdims=True)
    col = lax.broadcasted_iota(jnp.int32, (tm, tm), 1).astype(F32)
    pt_ref[...] = jnp.where(col == dest, 1.0, 0.0).astype(BF16)
    dest_row = jnp.broadcast_to(dest, (tm, LANES)).T[0:1, :]
    row = lax.broadcasted_iota(jnp.int32, (tm, tm), 0).astype(F32)
    p = jnp.where(row == dest_row, 1.0, 0.0).astype(BF16)

    xs_ref[...] = jnp.dot(p, h1.astype(BF16), preferred_element_type=F32).astype(BF16)
    rhi = route.astype(BF16)
    rlo = (route - rhi.astype(F32)).astype(BF16)
    gs2 = jnp.dot(p, jnp.concatenate([rhi, rlo], axis=1), preferred_element_type=F32)
    gs_ref[...] = gs2[:, :LANES] + gs2[:, LANES:]
    ys_ref[...] = jnp.zeros_like(ys_ref)

    bounds = [jnp.int32(0)] + [_lane_scalar(start, g).astype(jnp.int32) for g in range(1, N_GROUPS)]
    bounds.append(jnp.int32(tm))
    for j in range(tm // c):
        rows = slice(j * c, (j + 1) * c)
        for g in range(N_GROUPS):
            @pl.when((bounds[g] < (j + 1) * c) & (bounds[g + 1] > j * c))
            def _():
                xj = xs_ref[rows, :]
                gsj = gs_ref[rows, :]
                lane_c = lax.broadcasted_iota(jnp.int32, gsj.shape, 1)
                hids = []
                for k in range(EXPERTS_PER_GROUP):
                    e = g * EXPERTS_PER_GROUP + k
                    up = jnp.dot(xj, w13_ref[e], preferred_element_type=F32)
                    u1 = up[:, :D_EXPERT]
                    ge = jnp.sum(jnp.where(lane_c == EXPERT_LANE0 + e, gsj, 0.0), axis=1, keepdims=True)
                    hids.append(((u1 * jax.nn.sigmoid(u1)) * up[:, D_EXPERT:] * ge).astype(BF16))
                ys_ref[rows, :] += jnp.dot(jnp.concatenate(hids, axis=1), w2_ref[g],
                                           preferred_element_type=F32)

    ffn = jnp.dot(pt_ref[...], ys_ref[...].astype(BF16), preferred_element_type=F32)
    o_ref[...] = _layer_norm(DEEPNORM_ALPHA * h1 + ffn, g2_ref[...], b2_ref[...])


def _moe(h1, route, tril, w13, w2, g2, b2):
    T = h1.shape[0]
    tm = TM_MOE
    gw = EXPERTS_PER_GROUP * D_EXPERT
    const2 = lambda i: (0, 0)
    const3 = lambda i: (0, 0, 0)
    resident = pl.Buffered(1)
    return pl.pallas_call(
        _moe_kernel,
        grid=(T // tm,),
        in_specs=[
            pl.BlockSpec((tm, D_MODEL), lambda i: (i, 0)),
            pl.BlockSpec((tm, ROUTER_PAD), lambda i: (i, 0)),
            pl.BlockSpec((tm // 2, tm // 2), const2, pipeline_mode=resident),
            pl.BlockSpec((N_EXPERTS, D_MODEL, 2 * D_EXPERT), const3, pipeline_mode=resident),
            pl.BlockSpec((N_GROUPS, gw, D_MODEL), const3, pipeline_mode=resident),
            pl.BlockSpec((1, D_MODEL), const2),
            pl.BlockSpec((1, D_MODEL), const2),
        ],
        out_specs=pl.BlockSpec((tm, D_MODEL), lambda i: (i, 0)),
        out_shape=jax.ShapeDtypeStruct((T, D_MODEL), F32),
        scratch_shapes=[pltpu.VMEM((tm, tm), BF16), pltpu.VMEM((tm, D_MODEL), BF16),
                        pltpu.VMEM((tm, ROUTER_PAD), F32), pltpu.VMEM((tm, D_MODEL), F32)],
        compiler_params=pltpu.CompilerParams(
            dimension_semantics=("parallel",), vmem_limit_bytes=VMEM_LIMIT),
        name="moe",
    )(h1, route, tril, w13, w2, g2, b2)


def _rope_tables(S):
    t = jnp.arange(S, dtype=jnp.int32)
    row = (t // GRID_W).astype(F32)
    col = (t % GRID_W).astype(F32)
    half = ROPE_SECTION // 2
    inv_freq = ROPE_THETA ** (-jnp.arange(half, dtype=F32) / half)
    ang_r = row[:, None] * inv_freq
    ang_c = col[:, None] * inv_freq
    cos_h = jnp.concatenate([jnp.cos(ang_r)] * 2 + [jnp.cos(ang_c)] * 2, axis=1)
    sin_h = jnp.concatenate([-jnp.sin(ang_r), jnp.sin(ang_r), -jnp.sin(ang_c), jnp.sin(ang_c)], axis=1)
    reps = KV_WIDTH // HEAD_DIM
    return jnp.tile(cos_h, (1, reps)), jnp.tile(sin_h, (1, reps))


def _block_diag(w):
    H, d, _ = w.shape
    eye = jnp.eye(H, dtype=w.dtype)
    return (eye[:, None, :, None] * w[:, :, None, :]).reshape(H * d, H * d)


def _lru_gate_weights(wa, ba, wx, bx):
    C = LRU_CH
    ncg = D_LRU // C
    hpg = C // LRU_HEAD_DIM
    ws, bs = [], []
    for c in range(ncg):
        hs = slice(c * hpg, (c + 1) * hpg)
        cols, bias = [], []
        for d in range(2):
            cols += [_block_diag(wa[d, hs]), _block_diag(wx[d, hs])]
            bias += [ba[d, hs].reshape(C), bx[d, hs].reshape(C)]
        ws.append(jnp.concatenate(cols, axis=1))
        bs.append(jnp.concatenate(bias)[None, :])
    return jnp.stack(ws).astype(BF16), jnp.stack(bs)


def _trunk(x, P):
    B, S, _ = x.shape
    x2 = x.reshape(B * S, D_MODEL)
    q, k, v, xl, yl = _in_proj(x2, B, S, P["ln_in_g"], P["ln_in_b"], P["wq"], P["wkv"], P["wx"],
                               P["wy"], P["seg"], P["qg"], P["kg"], P["cos"], P["sin"])
    attn = _attention(q, k, v, B, S)
    lru = _rg_lru(xl, yl, B, S, P["conv_w"], P["conv_b"], P["wg"], P["bg"], P["lam"])
    h1, route = _out_proj(x2, attn, lru, P["ln_in_g"], P["ln_in_b"], P["ag"], P["lg"], P["wo"],
                          P["g1"], P["b1"], P["wrh"], P["wrl"], P["br"])
    out = _moe(h1, route, P["tril"], P["w13"], P["w2"], P["g2"], P["b2"])
    return out.reshape(B, S, D_MODEL)


def kernel(x_prompt, x_sample, ln_in_g, ln_in_b, w_in, conv_w, conv_b, lru_wa, lru_ba, lru_wx, lru_bx, lru_lambda, q_norm_g, k_norm_g, attn_out_g, lru_out_g, w_out, ln1_g, ln1_b, router_wg, router_bg, router_we, router_be, exp_w1, exp_w3, exp_w2, ln2_g, ln2_b):
    assert w_in.shape[0] == DEPTH == 1
    S = x_prompt.shape[1]
    l = 0
    w = w_in[l]
    c0, c1, c2 = D_ATTN, D_ATTN + 2 * KV_WIDTH, D_ATTN + 2 * KV_WIDTH + D_LRU
    seg = _block_diag(jnp.full((N_Q_HEADS, HEAD_DIM, HEAD_DIM), 1.0 / HEAD_DIM, F32)).astype(BF16)
    cos_t, sin_t = _rope_tables(S)
    wg, bg = _lru_gate_weights(lru_wa[l], lru_ba[l], lru_wx[l], lru_bx[l])
    wr = jnp.concatenate([router_wg[l], router_we[l].reshape(D_MODEL, N_EXPERTS)], axis=1)
    wr = jnp.pad(wr, ((0, 0), (0, ROUTER_PAD - wr.shape[1])))
    wrh = wr.astype(BF16)
    wrl = (wr - wrh.astype(F32)).astype(BF16)
    br = jnp.concatenate([router_bg[l], router_be[l].reshape(N_EXPERTS)])
    br = jnp.pad(br, (0, ROUTER_PAD - br.shape[0]))[None, :]
    P = dict(
        ln_in_g=ln_in_g[None, :], ln_in_b=ln_in_b[None, :],
        wq=w[:, :c0].astype(BF16), wkv=w[:, c0:c1].astype(BF16),
        wx=w[:, c1:c2].astype(BF16), wy=w[:, c2:].astype(BF16),
        seg=seg, qg=jnp.tile(q_norm_g[l], N_Q_HEADS)[None, :],
        kg=jnp.tile(k_norm_g[l], N_KV_HEADS)[None, :], cos=cos_t, sin=sin_t,
        conv_w=conv_w[l], conv_b=conv_b[l][None, :], wg=wg, bg=bg, lam=lru_lambda[l],
        ag=attn_out_g[l][None, :], lg=lru_out_g[l][None, :], wo=w_out[l].astype(BF16),
        g1=ln1_g[l][None, :], b1=ln1_b[l][None, :], wrh=wrh, wrl=wrl, br=br,
        w13=jnp.concatenate([exp_w1[l], exp_w3[l]], axis=2).astype(BF16),
        w2=exp_w2[l].reshape(N_GROUPS, EXPERTS_PER_GROUP * D_EXPERT, D_MODEL).astype(BF16),
        g2=ln2_g[l][None, :], b2=ln2_b[l][None, :],
        tril=jnp.tril(jnp.ones((TM_MOE // 2, TM_MOE // 2), F32), -1).astype(BF16),
    )
    return (_trunk(x_prompt, P), _trunk(x_sample, P))
```

```python
import functools
import math

import jax
import jax.numpy as jnp
from jax import lax
from jax.experimental import pallas as pl
from jax.experimental.pallas import tpu as pltpu

F32 = jnp.float32
BF16 = jnp.bfloat16

D_MODEL = 1024
GRID_W = 64
D_ATTN = 512
HEAD_DIM = 64
N_Q_HEADS = 8
N_KV_HEADS = 2
Q_PER_KV = 4
KV_WIDTH = 128
ROPE_THETA = 10000.0
ROPE_SECTION = 32
QK_EPS = 1e-6
D_LRU = 512
N_LRU_HEADS = 8
LRU_HEAD_DIM = 64
CONV_WIDTH = 4
LRU_C = 8.0
N_GROUPS = 4
EXPERTS_PER_GROUP = 4
N_EXPERTS = 16
D_EXPERT = 256
LN_EPS = 1e-5
DEPTH = 1
DEEPNORM_ALPHA = (2.0 * DEPTH) ** 0.25
LOG2_E = math.log2(math.e)

LANES = 128
SUBLANES = 8
VMEM_LIMIT = 48 * 1024 * 1024

TM_IN = 1024
IN_PROJ_SPLIT = 4
TM_OUT = 1024
TQ = 1024
KEY_BLOCK = 1024
LRU_CH = 256
SEG = 256
LRU_SCAN_UNROLL = 4
LRU_COMBINE_ROWS = 128
TM_MOE = 512
MOE_CHUNK = 128
MOE_TILES_PER_STEP = 2
MOE_VMEM_LIMIT = 56 * 1024 * 1024
ROUTER_PAD = LANES
EXPERT_LANE0 = N_GROUPS


def _layer_norm(x, g, b):
    mu = jnp.mean(x, axis=-1, keepdims=True)
    xc = x - mu
    var = jnp.mean(xc * xc, axis=-1, keepdims=True)
    return xc * lax.rsqrt(var + LN_EPS) * g + b


def _rms_norm_rows(x, g, eps):
    return x * lax.rsqrt(jnp.mean(x * x, axis=-1, keepdims=True) + eps) * g


def _head_norm_rope(xf, seg, g, cos, sin):
    ms = jnp.dot((xf * xf).astype(BF16), seg, preferred_element_type=F32)
    xn = xf * lax.rsqrt(ms + QK_EPS) * g
    n = xn.shape[1]
    half = ROPE_SECTION // 2
    up = pltpu.roll(xn, n - half, axis=1)
    dn = pltpu.roll(xn, half, axis=1)
    lane = lax.broadcasted_iota(jnp.int32, xn.shape, 1)
    first = (lane % ROPE_SECTION) < half
    partner = jnp.where(first, up, dn)
    return xn * cos + partner * sin


def _in_proj_kernel(x_ref, g_ref, b_ref, wq_ref, wkv_ref, wx_ref, wy_ref, seg_ref,
                    qg_ref, kg_ref, cos_ref, sin_ref,
                    q_ref, k_ref, v_ref, xl_ref, yl_ref, st_ref):
    reps = D_ATTN // KV_WIDTH
    blk = x_ref.shape[0] // IN_PROJ_SPLIT
    for i in range(IN_PROJ_SPLIT):
        rows = slice(i * blk, (i + 1) * blk)
        x = x_ref[rows, :]
        mu = jnp.mean(x, axis=-1, keepdims=True)
        xc = x - mu
        rstd = lax.rsqrt(jnp.mean(xc * xc, axis=-1, keepdims=True) + LN_EPS)
        hb = (xc * rstd * g_ref[...] + b_ref[...]).astype(BF16)
        lane = lax.broadcasted_iota(jnp.int32, (blk, LANES), 1)
        st_ref[rows, :] = jnp.where(lane == 0, mu, jnp.where(lane == 1, rstd, 0.0))
        cos = cos_ref[rows, :]
        sin = sin_ref[rows, :]

        qf = jnp.dot(hb, wq_ref[...], preferred_element_type=F32)
        q = _head_norm_rope(qf, seg_ref[...], qg_ref[...],
                            jnp.concatenate([cos] * reps, axis=1),
                            jnp.concatenate([sin] * reps, axis=1))
        q_ref[rows, :] = (q * (HEAD_DIM ** -0.5 * LOG2_E)).astype(BF16)

        kvf = jnp.dot(hb, wkv_ref[...], preferred_element_type=F32)
        k = _head_norm_rope(kvf[:, :KV_WIDTH], seg_ref[:KV_WIDTH, :KV_WIDTH], kg_ref[...], cos, sin)
        kb = k.astype(BF16)
        vb = kvf[:, KV_WIDTH:].astype(BF16)
        for j in range(N_KV_HEADS):
            k_ref[j, rows, :] = kb[:, j * HEAD_DIM:(j + 1) * HEAD_DIM]
            v_ref[j, rows, :] = vb[:, j * HEAD_DIM:(j + 1) * HEAD_DIM]

        xl_ref[rows, :] = jnp.dot(hb, wx_ref[...], preferred_element_type=F32)
        yl_ref[rows, :] = jnp.dot(hb, wy_ref[...], preferred_element_type=F32)


def _in_proj(x2, B, S, ln_g, ln_b, wq, wkv, wx, wy, seg, qg, kg, cos_t, sin_t):
    T = B * S
    tm = TM_IN
    nt = S // tm
    const = lambda i: (0, 0)
    return pl.pallas_call(
        _in_proj_kernel,
        grid=(T // tm,),
        in_specs=[
            pl.BlockSpec((tm, D_MODEL), lambda i: (i, 0)),
            pl.BlockSpec((1, D_MODEL), const),
            pl.BlockSpec((1, D_MODEL), const),
            pl.BlockSpec((D_MODEL, D_ATTN), const),
            pl.BlockSpec((D_MODEL, 2 * KV_WIDTH), const),
            pl.BlockSpec((D_MODEL, D_LRU), const),
            pl.BlockSpec((D_MODEL, D_LRU), const),
            pl.BlockSpec((D_ATTN, D_ATTN), const),
            pl.BlockSpec((1, D_ATTN), const),
            pl.BlockSpec((1, KV_WIDTH), const),
            pl.BlockSpec((tm, KV_WIDTH), lambda i: (i % nt, 0)),
            pl.BlockSpec((tm, KV_WIDTH), lambda i: (i % nt, 0)),
        ],
        out_specs=[
            pl.BlockSpec((tm, D_ATTN), lambda i: (i, 0)),
            pl.BlockSpec((None, N_KV_HEADS, tm, HEAD_DIM), lambda i: (i // nt, 0, i % nt, 0)),
            pl.BlockSpec((None, N_KV_HEADS, tm, HEAD_DIM), lambda i: (i // nt, 0, i % nt, 0)),
            pl.BlockSpec((tm, D_LRU), lambda i: (i, 0)),
            pl.BlockSpec((tm, D_LRU), lambda i: (i, 0)),
            pl.BlockSpec((tm, LANES), lambda i: (i, 0)),
        ],
        out_shape=[
            jax.ShapeDtypeStruct((T, D_ATTN), BF16),
            jax.ShapeDtypeStruct((B, N_KV_HEADS, S, HEAD_DIM), BF16),
            jax.ShapeDtypeStruct((B, N_KV_HEADS, S, HEAD_DIM), BF16),
            jax.ShapeDtypeStruct((T, D_LRU), F32),
            jax.ShapeDtypeStruct((T, D_LRU), F32),
            jax.ShapeDtypeStruct((T, LANES), F32),
        ],
        compiler_params=pltpu.CompilerParams(
            dimension_semantics=("parallel",), vmem_limit_bytes=VMEM_LIMIT),
        name="in_proj",
    )(x2, ln_g, ln_b, wq, wkv, wx, wy, seg, qg, kg, cos_t, sin_t)


def _attn_kernel(q_ref, k_ref, v_ref, o_ref):
    nblk = k_ref.shape[0] // KEY_BLOCK
    outs = []
    for h in range(Q_PER_KV):
        qh = q_ref[:, h * HEAD_DIM:(h + 1) * HEAD_DIM]
        m = l = acc = None
        for b in range(nblk):
            kb = k_ref[b * KEY_BLOCK:(b + 1) * KEY_BLOCK, :]
            vb = v_ref[b * KEY_BLOCK:(b + 1) * KEY_BLOCK, :]
            s = lax.dot_general(qh, kb, (((1,), (1,)), ((), ())), preferred_element_type=F32)
            mb = jnp.max(s, axis=1, keepdims=True)
            if b == 0:
                m = mb
                p = jnp.exp2(s - m)
                l = jnp.sum(p, axis=1, keepdims=True)
                acc = jnp.dot(p.astype(BF16), vb, preferred_element_type=F32)
            else:
                m_new = jnp.maximum(m, mb)
                alpha = jnp.exp2(m - m_new)
                p = jnp.exp2(s - m_new)
                l = alpha * l + jnp.sum(p, axis=1, keepdims=True)
                acc = alpha * acc + jnp.dot(p.astype(BF16), vb, preferred_element_type=F32)
                m = m_new
        outs.append(acc / l)
    o_ref[...] = jnp.concatenate(outs, axis=1)


def _attention(q, k, v, B, S):
    T = B * S
    nq = S // TQ
    width = Q_PER_KV * HEAD_DIM
    return pl.pallas_call(
        _attn_kernel,
        grid=(B, N_KV_HEADS, nq),
        in_specs=[
            pl.BlockSpec((TQ, width), lambda b, j, t: (b * nq + t, j)),
            pl.BlockSpec((None, None, S, HEAD_DIM), lambda b, j, t: (b, j, 0, 0)),
            pl.BlockSpec((None, None, S, HEAD_DIM), lambda b, j, t: (b, j, 0, 0)),
        ],
        out_specs=pl.BlockSpec((TQ, width), lambda b, j, t: (b * nq + t, j)),
        out_shape=jax.ShapeDtypeStruct((T, D_ATTN), F32),
        compiler_params=pltpu.CompilerParams(
            dimension_semantics=("parallel", "parallel", "parallel"),
            vmem_limit_bytes=VMEM_LIMIT),
        name="attention",
    )(q, k, v)


def _gelu_tanh(x):
    c = math.sqrt(2.0 / math.pi)
    return 0.5 * x * (1.0 + jnp.tanh(c * (x + 0.044715 * (x * x * x))))


def _rg_lru_kernel(x_ref, y_ref, cw_ref, cb_ref, wg_ref, bg_ref, lam_ref, o_ref,
                   af_ref, uf_ref, ab_ref, ub_ref, hf_ref, pf_ref, hb_ref, pb_ref, cf_ref, cbk_ref):
    S, C = x_ref.shape
    nslab = C // LANES
    nseg = S // SEG
    assert nseg == SUBLANES
    lam = lam_ref[...]
    neg_c_softplus = -LRU_C * jnp.logaddexp(-lam, 0.0)
    cw = cw_ref[...]
    cb = cb_ref[...]
    bg = bg_ref[...]

    def gate_block(s, carry):
        start = pl.multiple_of(s * SEG, SEG)
        seg_rows = pl.ds(s, SEG, stride=SUBLANES)
        cur = x_ref[pl.ds(start, SEG), :]
        prev = x_ref[pl.ds(pl.multiple_of(jnp.maximum(start - SUBLANES, 0), SUBLANES), SUBLANES), :]
        prev = jnp.where(s > 0, prev, 0.0)
        nxt = x_ref[pl.ds(pl.multiple_of(jnp.minimum(start + SEG, S - SUBLANES), SUBLANES), SUBLANES), :]
        nxt = jnp.where(s < nseg - 1, nxt, 0.0)
        ext = jnp.concatenate([prev, cur, nxt], axis=0)
        xc = cb
        for tap in range(CONV_WIDTH):
            off = SUBLANES + tap - CONV_WIDTH // 2
            xc = xc + ext[off:off + SEG, :] * cw[tap:tap + 1, :]
        z = jnp.dot(xc.astype(BF16), wg_ref[...], preferred_element_type=F32) + bg
        for d, (a_ref, u_ref) in enumerate(((af_ref, uf_ref), (ab_ref, ub_ref))):
            rg = jax.nn.sigmoid(z[:, (2 * d) * C:(2 * d + 1) * C])
            ig = jax.nn.sigmoid(z[:, (2 * d + 1) * C:(2 * d + 2) * C])
            a = jnp.exp(neg_c_softplus[d:d + 1, :] * rg)
            u = jnp.sqrt(1.0 - a * a) * (ig * xc)
            for slab in range(nslab):
                lanes = slice(slab * LANES, (slab + 1) * LANES)
                a_ref[slab, seg_rows, :] = a[:, lanes]
                u_ref[slab, seg_rows, :] = u[:, lanes]
        return carry

    lax.fori_loop(0, nseg, gate_block, 0)

    def scan_step(j, carry):
        out = []
        for slab in range(nslab):
            hf, pf, hb, pb = carry[slab]
            rows = pl.ds(pl.multiple_of(j * SUBLANES, SUBLANES), SUBLANES)
            a = af_ref[slab, rows, :]
            hf = a * hf + uf_ref[slab, rows, :]
            pf = a * pf
            hf_ref[slab, rows, :] = hf
            pf_ref[slab, rows, :] = pf
            rows = pl.ds(pl.multiple_of((SEG - 1 - j) * SUBLANES, SUBLANES), SUBLANES)
            a = ab_ref[slab, rows, :]
            hb = a * hb + ub_ref[slab, rows, :]
            pb = a * pb
            hb_ref[slab, rows, :] = hb
            pb_ref[slab, rows, :] = pb
            out.append((hf, pf, hb, pb))
        return tuple(out)

    zero = jnp.zeros((SUBLANES, LANES), F32)
    one = jnp.ones((SUBLANES, LANES), F32)
    final = lax.fori_loop(0, SEG, scan_step, tuple((zero, one, zero, one) for _ in range(nslab)),
                          unroll=LRU_SCAN_UNROLL)

    for slab in range(nslab):
        hf, pf, hb, pb = final[slab]
        c = jnp.zeros((1, LANES), F32)
        fwd = [c]
        for sgm in range(1, nseg):
            c = hf[sgm - 1:sgm, :] + pf[sgm - 1:sgm, :] * c
            fwd.append(c)
        c = jnp.zeros((1, LANES), F32)
        bwd = [c]
        for sgm in range(nseg - 2, -1, -1):
            c = hb[sgm + 1:sgm + 2, :] + pb[sgm + 1:sgm + 2, :] * c
            bwd.append(c)
        cf_ref[slab] = jnp.concatenate(fwd, axis=0)
        cbk_ref[slab] = jnp.concatenate(bwd[::-1], axis=0)

    reps = LRU_COMBINE_ROWS // SUBLANES

    def combine(i, carry):
        rows = pl.ds(pl.multiple_of(i * LRU_COMBINE_ROWS, LRU_COMBINE_ROWS), LRU_COMBINE_ROWS)
        for slab in range(nslab):
            cf = jnp.concatenate([cf_ref[slab]] * reps, axis=0)
            cbk = jnp.concatenate([cbk_ref[slab]] * reps, axis=0)
            uf_ref[slab, rows, :] = ((hf_ref[slab, rows, :] + pf_ref[slab, rows, :] * cf)
                                     + (hb_ref[slab, rows, :] + pb_ref[slab, rows, :] * cbk))
        return carry

    lax.fori_loop(0, S // LRU_COMBINE_ROWS, combine, 0)

    def out_block(s, carry):
        start = pl.multiple_of(s * SEG, SEG)
        for slab in range(nslab):
            lanes = slice(slab * LANES, (slab + 1) * LANES)
            h = uf_ref[slab, pl.ds(s, SEG, stride=SUBLANES), :]
            o_ref[pl.ds(start, SEG), lanes] = h * _gelu_tanh(y_ref[pl.ds(start, SEG), lanes])
        return carry

    lax.fori_loop(0, nseg, out_block, 0)


def _rg_lru(xl, yl, B, S, conv_w, conv_b, wg, bg, lam):
    C = LRU_CH
    ncg = D_LRU // C
    nslab = C // LANES
    xl3 = xl.reshape(B, S, D_LRU)
    yl3 = yl.reshape(B, S, D_LRU)
    seg_buf = pltpu.VMEM((nslab, S, LANES), F32)
    carry_buf = pltpu.VMEM((nslab, S // SEG, LANES), F32)
    out = pl.pallas_call(
        _rg_lru_kernel,
        grid=(B, ncg),
        in_specs=[
            pl.BlockSpec((None, S, C), lambda b, c: (b, 0, c)),
            pl.BlockSpec((None, S, C), lambda b, c: (b, 0, c)),
            pl.BlockSpec((CONV_WIDTH, C), lambda b, c: (0, c)),
            pl.BlockSpec((1, C), lambda b, c: (0, c)),
            pl.BlockSpec((None, C, 4 * C), lambda b, c: (c, 0, 0)),
            pl.BlockSpec((None, 1, 4 * C), lambda b, c: (c, 0, 0)),
            pl.BlockSpec((2, C), lambda b, c: (0, c)),
        ],
        out_specs=pl.BlockSpec((None, S, C), lambda b, c: (b, 0, c)),
        out_shape=jax.ShapeDtypeStruct((B, S, D_LRU), F32),
        scratch_shapes=[seg_buf] * 8 + [carry_buf] * 2,
        compiler_params=pltpu.CompilerParams(
            dimension_semantics=("parallel", "parallel"), vmem_limit_bytes=VMEM_LIMIT),
        name="rg_lru",
    )(xl3, yl3, conv_w, conv_b, wg, bg, lam)
    return out.reshape(B * S, D_LRU)


def _router_gates(logits):
    lane = lax.broadcasted_iota(jnp.int32, logits.shape, 1).astype(F32)
    neg = -jnp.inf
    big = float(2 * LANES)
    gl = jnp.where(lane < N_GROUPS, logits, neg)
    gmax = jnp.max(gl, axis=1, keepdims=True)
    gidx = jnp.min(jnp.where(gl == gmax, lane, big), axis=1, keepdims=True)
    g_sel_prob = 1.0 / jnp.sum(jnp.exp(gl - gmax), axis=1, keepdims=True)
    lo = EXPERT_LANE0 + EXPERTS_PER_GROUP * gidx
    el = jnp.where((lane >= lo) & (lane < lo + EXPERTS_PER_GROUP), logits, neg)
    v1 = jnp.max(el, axis=1, keepdims=True)
    i1 = jnp.min(jnp.where(el == v1, lane, big), axis=1, keepdims=True)
    el2 = jnp.where(lane == i1, neg, el)
    v2 = jnp.max(el2, axis=1, keepdims=True)
    i2 = jnp.min(jnp.where(el2 == v2, lane, big), axis=1, keepdims=True)
    e2 = jnp.exp(v2 - v1)
    w1 = g_sel_prob / (1.0 + e2)
    w2 = g_sel_prob * e2 / (1.0 + e2)
    gates = jnp.where(lane == i1, w1, jnp.where(lane == i2, w2, 0.0))
    return jnp.where(lane == 0.0, gidx, gates)


def _out_proj_kernel(x_ref, st_ref, attn_ref, lru_ref, lng_ref, lnb_ref, ag_ref, lg_ref, wo_ref,
                     g1_ref, b1_ref, wrh_ref, wrl_ref, br_ref, h1_ref, gate_ref):
    h = (x_ref[...] - st_ref[:, 0:1]) * st_ref[:, 1:2] * lng_ref[...] + lnb_ref[...]
    merged = jnp.concatenate(
        [_rms_norm_rows(attn_ref[...], ag_ref[...], LN_EPS),
         _rms_norm_rows(lru_ref[...], lg_ref[...], LN_EPS)], axis=1).astype(BF16)
    mix = jnp.dot(merged, wo_ref[...], preferred_element_type=F32)
    h1 = _layer_norm(DEEPNORM_ALPHA * h + mix, g1_ref[...], b1_ref[...])
    h1_ref[...] = h1
    hi = h1.astype(BF16)
    lo = (h1 - hi.astype(F32)).astype(BF16)
    wrh = wrh_ref[...]
    logits = (jnp.dot(hi, wrh, preferred_element_type=F32)
              + jnp.dot(lo, wrh, preferred_element_type=F32)
              + jnp.dot(hi, wrl_ref[...], preferred_element_type=F32)) + br_ref[...]
    gate_ref[...] = _router_gates(logits)


def _out_proj(x2, stats, attn, lru, ln_g, ln_b, ag, lg, wo, g1, b1, wrh, wrl, br):
    T = x2.shape[0]
    tm = TM_OUT
    const = lambda i: (0, 0)
    row = lambda i: (i, 0)
    return pl.pallas_call(
        _out_proj_kernel,
        grid=(T // tm,),
        in_specs=[
            pl.BlockSpec((tm, D_MODEL), row),
            pl.BlockSpec((tm, LANES), row),
            pl.BlockSpec((tm, D_ATTN), row),
            pl.BlockSpec((tm, D_LRU), row),
            pl.BlockSpec((1, D_MODEL), const),
            pl.BlockSpec((1, D_MODEL), const),
            pl.BlockSpec((1, D_ATTN), const),
            pl.BlockSpec((1, D_LRU), const),
            pl.BlockSpec((D_MODEL, D_MODEL), const),
            pl.BlockSpec((1, D_MODEL), const),
            pl.BlockSpec((1, D_MODEL), const),
            pl.BlockSpec((D_MODEL, ROUTER_PAD), const),
            pl.BlockSpec((D_MODEL, ROUTER_PAD), const),
            pl.BlockSpec((1, ROUTER_PAD), const),
        ],
        out_specs=[pl.BlockSpec((tm, D_MODEL), row), pl.BlockSpec((tm, ROUTER_PAD), row)],
        out_shape=[jax.ShapeDtypeStruct((T, D_MODEL), F32),
                   jax.ShapeDtypeStruct((T, ROUTER_PAD), F32)],
        compiler_params=pltpu.CompilerParams(
            dimension_semantics=("parallel",), vmem_limit_bytes=VMEM_LIMIT),
        name="out_proj",
    )(x2, stats, attn, lru, ln_g, ln_b, ag, lg, wo, g1, b1, wrh, wrl, br)


def _lane_scalar(row, idx):
    lane = lax.broadcasted_iota(jnp.int32, row.shape, 1)
    return jnp.sum(jnp.where(lane == idx, row, 0.0))


def _moe_kernel(h1_ref, route_ref, tril_ref, w13_ref, w2_ref, g2_ref, b2_ref, o_ref,
                pt_ref, xs_ref, gs_ref, ys_ref):
    for t in range(MOE_TILES_PER_STEP):
        rows = pl.ds(t * TM_MOE, TM_MOE)
        _moe_tile(h1_ref.at[rows, :], route_ref.at[rows, :], tril_ref, w13_ref, w2_ref, g2_ref, b2_ref,
                  o_ref.at[rows, :], pt_ref.at[t], xs_ref.at[t], gs_ref.at[t], ys_ref.at[t])


def _moe_tile(h1_ref, route_ref, tril_ref, w13_ref, w2_ref, g2_ref, b2_ref, o_ref,
              pt_ref, xs_ref, gs_ref, ys_ref):
    tm = h1_ref.shape[0]
    c = MOE_CHUNK
    h1 = h1_ref[...]
    route = route_ref[...]
    lane = lax.broadcasted_iota(jnp.int32, route.shape, 1)
    gid = jnp.sum(jnp.where(lane == 0, route, 0.0), axis=1, keepdims=True)
    onehot = jnp.where(lane.astype(F32) == gid, 1.0, 0.0)
    half = tm // 2
    ohb = onehot.astype(BF16)
    tril = tril_ref[...]
    cnt_top = jnp.sum(onehot[:half], axis=0, keepdims=True)
    before = jnp.concatenate(
        [jnp.dot(tril, ohb[:half], preferred_element_type=F32),
         jnp.dot(tril, ohb[half:], preferred_element_type=F32) + cnt_top], axis=0)
    cnt = jnp.broadcast_to(cnt_top + jnp.sum(onehot[half:], axis=0, keepdims=True), (SUBLANES, LANES))
    start = pltpu.roll(cnt, 1, axis=1) + pltpu.roll(cnt, 2, axis=1) + pltpu.roll(cnt, 3, axis=1)
    start = start[0:1, :]
    dest = jnp.sum(onehot * (before + start), axis=1, keepdims=True)
    col = lax.broadcasted_iota(jnp.int32, (tm, tm), 1).astype(F32)
    pt_ref[...] = jnp.where(col == dest, 1.0, 0.0).astype(BF16)
    dest_row = jnp.broadcast_to(dest, (tm, LANES)).T[0:1, :]
    row = lax.broadcasted_iota(jnp.int32, (tm, tm), 0).astype(F32)
    p = jnp.where(row == dest_row, 1.0, 0.0).astype(BF16)

    xs_ref[...] = jnp.dot(p, h1.astype(BF16), preferred_element_type=F32).astype(BF16)
    rhi = route.astype(BF16)
    rlo = (route - rhi.astype(F32)).astype(BF16)
    gs2 = jnp.dot(p, jnp.concatenate([rhi, rlo], axis=1), preferred_element_type=F32)
    gs_ref[...] = gs2[:, :LANES] + gs2[:, LANES:]
    ys_ref[...] = jnp.zeros_like(ys_ref)

    bounds = [jnp.int32(0)] + [_lane_scalar(start, g).astype(jnp.int32) for g in range(1, N_GROUPS)]
    bounds.append(jnp.int32(tm))
    nchunk = tm // c
    nslot = nchunk + N_GROUPS - 1
    slot_j = [jnp.int32(0)] * nslot
    slot_g = [jnp.int32(0)] * nslot
    count = jnp.int32(0)
    for j in range(nchunk):
        for g in range(N_GROUPS):
            active = ((bounds[g + 1] > bounds[g]) & (bounds[g] < (j + 1) * c) & (bounds[g + 1] > j * c))
            for s in range(nslot):
                hit = active & (count == s)
                slot_j[s] = jnp.where(hit, j, slot_j[s])
                slot_g[s] = jnp.where(hit, g, slot_g[s])
            count = count + active.astype(jnp.int32)
    for s in range(nslot):
        g = slot_g[s]
        rows = pl.ds(pl.multiple_of(slot_j[s] * c, c), c)
        xj = xs_ref[rows, :]
        gsj = gs_ref[rows, :] * (s < count).astype(F32)
        lane_c = lax.broadcasted_iota(jnp.int32, gsj.shape, 1)
        hids = []
        for k in range(EXPERTS_PER_GROUP):
            e = g * EXPERTS_PER_GROUP + k
            up = jnp.dot(xj, w13_ref[e], preferred_element_type=F32)
            u1 = up[:, :D_EXPERT]
            ge = jnp.sum(jnp.where(lane_c == EXPERT_LANE0 + e, gsj, 0.0), axis=1, keepdims=True)
            hids.append(((u1 * jax.nn.sigmoid(u1)) * up[:, D_EXPERT:] * ge).astype(BF16))
        ys_ref[rows, :] += jnp.dot(jnp.concatenate(hids, axis=1), w2_ref[g],
                                   preferred_element_type=F32)

    ffn = jnp.dot(pt_ref[...], ys_ref[...].astype(BF16), preferred_element_type=F32)
    o_ref[...] = _layer_norm(DEEPNORM_ALPHA * h1 + ffn, g2_ref[...], b2_ref[...])


def _moe(h1, route, tril, w13, w2, g2, b2):
    T = h1.shape[0]
    tm = TM_MOE
    nt = MOE_TILES_PER_STEP
    gw = EXPERTS_PER_GROUP * D_EXPERT
    const2 = lambda i: (0, 0)
    const3 = lambda i: (0, 0, 0)
    resident = pl.Buffered(1)
    return pl.pallas_call(
        _moe_kernel,
        grid=(T // (nt * tm),),
        in_specs=[
            pl.BlockSpec((nt * tm, D_MODEL), lambda i: (i, 0)),
            pl.BlockSpec((nt * tm, ROUTER_PAD), lambda i: (i, 0)),
            pl.BlockSpec((tm // 2, tm // 2), const2, pipeline_mode=resident),
            pl.BlockSpec((N_EXPERTS, D_MODEL, 2 * D_EXPERT), const3, pipeline_mode=resident),
            pl.BlockSpec((N_GROUPS, gw, D_MODEL), const3, pipeline_mode=resident),
            pl.BlockSpec((1, D_MODEL), const2),
            pl.BlockSpec((1, D_MODEL), const2),
        ],
        out_specs=pl.BlockSpec((nt * tm, D_MODEL), lambda i: (i, 0)),
        out_shape=jax.ShapeDtypeStruct((T, D_MODEL), F32),
        scratch_shapes=[pltpu.VMEM((nt, tm, tm), BF16), pltpu.VMEM((nt, tm, D_MODEL), BF16),
                        pltpu.VMEM((nt, tm, ROUTER_PAD), F32), pltpu.VMEM((nt, tm, D_MODEL), F32)],
        compiler_params=pltpu.CompilerParams(
            dimension_semantics=("parallel",), vmem_limit_bytes=MOE_VMEM_LIMIT),
        name="moe",
    )(h1, route, tril, w13, w2, g2, b2)


def _rope_tables(S):
    t = jnp.arange(S, dtype=jnp.int32)
    row = (t // GRID_W).astype(F32)
    col = (t % GRID_W).astype(F32)
    half = ROPE_SECTION // 2
    inv_freq = ROPE_THETA ** (-jnp.arange(half, dtype=F32) / half)
    ang_r = row[:, None] * inv_freq
    ang_c = col[:, None] * inv_freq
    cos_h = jnp.concatenate([jnp.cos(ang_r)] * 2 + [jnp.cos(ang_c)] * 2, axis=1)
    sin_h = jnp.concatenate([-jnp.sin(ang_r), jnp.sin(ang_r), -jnp.sin(ang_c), jnp.sin(ang_c)], axis=1)
    reps = KV_WIDTH // HEAD_DIM
    return jnp.tile(cos_h, (1, reps)), jnp.tile(sin_h, (1, reps))


def _block_diag(w):
    H, d, _ = w.shape
    eye = jnp.eye(H, dtype=w.dtype)
    return (eye[:, None, :, None] * w[:, :, None, :]).reshape(H * d, H * d)


def _lru_gate_weights(wa, ba, wx, bx):
    C = LRU_CH
    ncg = D_LRU // C
    hpg = C // LRU_HEAD_DIM
    ws, bs = [], []
    for c in range(ncg):
        hs = slice(c * hpg, (c + 1) * hpg)
        cols, bias = [], []
        for d in range(2):
            cols += [_block_diag(wa[d, hs]), _block_diag(wx[d, hs])]
            bias += [ba[d, hs].reshape(C), bx[d, hs].reshape(C)]
        ws.append(jnp.concatenate(cols, axis=1))
        bs.append(jnp.concatenate(bias)[None, :])
    return jnp.stack(ws).astype(BF16), jnp.stack(bs)


def _trunk(x, P):
    B, S, _ = x.shape
    x2 = x.reshape(B * S, D_MODEL)
    q, k, v, xl, yl, stats = _in_proj(x2, B, S, P["ln_in_g"], P["ln_in_b"], P["wq"], P["wkv"], P["wx"],
                               P["wy"], P["seg"], P["qg"], P["kg"], P["cos"], P["sin"])
    attn = _attention(q, k, v, B, S)
    lru = _rg_lru(xl, yl, B, S, P["conv_w"], P["conv_b"], P["wg"], P["bg"], P["lam"])
    h1, route = _out_proj(x2, stats, attn, lru, P["ln_in_g"], P["ln_in_b"], P["ag"], P["lg"], P["wo"],
                          P["g1"], P["b1"], P["wrh"], P["wrl"], P["br"])
    out = _moe(h1, route, P["tril"], P["w13"], P["w2"], P["g2"], P["b2"])
    return out.reshape(B, S, D_MODEL)


def kernel(x_prompt, x_sample, ln_in_g, ln_in_b, w_in, conv_w, conv_b, lru_wa, lru_ba, lru_wx, lru_bx, lru_lambda, q_norm_g, k_norm_g, attn_out_g, lru_out_g, w_out, ln1_g, ln1_b, router_wg, router_bg, router_we, router_be, exp_w1, exp_w3, exp_w2, ln2_g, ln2_b):
    assert w_in.shape[0] == DEPTH == 1
    S = x_prompt.shape[1]
    l = 0
    w = w_in[l]
    c0, c1, c2 = D_ATTN, D_ATTN + 2 * KV_WIDTH, D_ATTN + 2 * KV_WIDTH + D_LRU
    seg = _block_diag(jnp.full((N_Q_HEADS, HEAD_DIM, HEAD_DIM), 1.0 / HEAD_DIM, F32)).astype(BF16)
    cos_t, sin_t = _rope_tables(S)
    wg, bg = _lru_gate_weights(lru_wa[l], lru_ba[l], lru_wx[l], lru_bx[l])
    wr = jnp.concatenate([router_wg[l], router_we[l].reshape(D_MODEL, N_EXPERTS)], axis=1)
    wr = jnp.pad(wr, ((0, 0), (0, ROUTER_PAD - wr.shape[1])))
    wrh = wr.astype(BF16)
    wrl = (wr - wrh.astype(F32)).astype(BF16)
    br = jnp.concatenate([router_bg[l], router_be[l].reshape(N_EXPERTS)])
    br = jnp.pad(br, (0, ROUTER_PAD - br.shape[0]))[None, :]
    P = dict(
        ln_in_g=ln_in_g[None, :], ln_in_b=ln_in_b[None, :],
        wq=w[:, :c0].astype(BF16), wkv=w[:, c0:c1].astype(BF16),
        wx=w[:, c1:c2].astype(BF16), wy=w[:, c2:].astype(BF16),
        seg=seg, qg=jnp.tile(q_norm_g[l], N_Q_HEADS)[None, :],
        kg=jnp.tile(k_norm_g[l], N_KV_HEADS)[None, :], cos=cos_t, sin=sin_t,
        conv_w=conv_w[l], conv_b=conv_b[l][None, :], wg=wg, bg=bg, lam=lru_lambda[l],
        ag=attn_out_g[l][None, :], lg=lru_out_g[l][None, :], wo=w_out[l].astype(BF16),
        g1=ln1_g[l][None, :], b1=ln1_b[l][None, :], wrh=wrh, wrl=wrl, br=br,
        w13=jnp.concatenate([exp_w1[l], exp_w3[l]], axis=2).astype(BF16),
        w2=exp_w2[l].reshape(N_GROUPS, EXPERTS_PER_GROUP * D_EXPERT, D_MODEL).astype(BF16),
        g2=ln2_g[l][None, :], b2=ln2_b[l][None, :],
        tril=jnp.tril(jnp.ones((TM_MOE // 2, TM_MOE // 2), F32), -1).astype(BF16),
    )
    return (_trunk(x_prompt, P), _trunk(x_sample, P))
```

```python
import functools
import math

import jax
import jax.numpy as jnp
from jax import lax
from jax.experimental import pallas as pl
from jax.experimental.pallas import tpu as pltpu

F32 = jnp.float32
BF16 = jnp.bfloat16

D_MODEL = 1024
GRID_W = 64
D_ATTN = 512
HEAD_DIM = 64
N_Q_HEADS = 8
N_KV_HEADS = 2
Q_PER_KV = 4
KV_WIDTH = 128
ROPE_THETA = 10000.0
ROPE_SECTION = 32
QK_EPS = 1e-6
D_LRU = 512
N_LRU_HEADS = 8
LRU_HEAD_DIM = 64
CONV_WIDTH = 4
LRU_C = 8.0
N_GROUPS = 4
EXPERTS_PER_GROUP = 4
N_EXPERTS = 16
D_EXPERT = 256
LN_EPS = 1e-5
DEPTH = 1
DEEPNORM_ALPHA = (2.0 * DEPTH) ** 0.25
LOG2_E = math.log2(math.e)

LANES = 128
SUBLANES = 8
VMEM_LIMIT = 48 * 1024 * 1024

TM_IN = 1024
IN_PROJ_SPLIT = 4
TM_OUT = 1024
TQ = 1024
KEY_BLOCK = 1024
LRU_CH = 256
SEG = 256
LRU_SCAN_UNROLL = 4
LRU_COMBINE_ROWS = 128
TM_MOE = 512
MOE_CHUNK = 128
MOE_TILES_PER_STEP = 2
MOE_VMEM_LIMIT = 56 * 1024 * 1024
ROUTER_PAD = LANES
EXPERT_LANE0 = N_GROUPS


def _layer_norm(x, g, b):
    mu = jnp.mean(x, axis=-1, keepdims=True)
    xc = x - mu
    var = jnp.mean(xc * xc, axis=-1, keepdims=True)
    return xc * lax.rsqrt(var + LN_EPS) * g + b


def _rms_norm_rows(x, g, eps):
    return x * lax.rsqrt(jnp.mean(x * x, axis=-1, keepdims=True) + eps) * g


def _head_norm_rope(xf, seg, g, cos, sin):
    ms = jnp.dot((xf * xf).astype(BF16), seg, preferred_element_type=F32)
    xn = xf * lax.rsqrt(ms + QK_EPS) * g
    n = xn.shape[1]
    half = ROPE_SECTION // 2
    up = pltpu.roll(xn, n - half, axis=1)
    dn = pltpu.roll(xn, half, axis=1)
    lane = lax.broadcasted_iota(jnp.int32, xn.shape, 1)
    first = (lane % ROPE_SECTION) < half
    partner = jnp.where(first, up, dn)
    return xn * cos + partner * sin


def _in_proj_kernel(x_ref, g_ref, b_ref, wq_ref, wkv_ref, wx_ref, wy_ref, seg_ref,
                    qg_ref, kg_ref, cos_ref, sin_ref,
                    q_ref, k_ref, v_ref, xl_ref, yl_ref, st_ref):
    reps = D_ATTN // KV_WIDTH
    blk = x_ref.shape[0] // IN_PROJ_SPLIT
    for i in range(IN_PROJ_SPLIT):
        rows = slice(i * blk, (i + 1) * blk)
        x = x_ref[rows, :]
        mu = jnp.mean(x, axis=-1, keepdims=True)
        xc = x - mu
        rstd = lax.rsqrt(jnp.mean(xc * xc, axis=-1, keepdims=True) + LN_EPS)
        hb = (xc * rstd * g_ref[...] + b_ref[...]).astype(BF16)
        lane = lax.broadcasted_iota(jnp.int32, (blk, LANES), 1)
        st_ref[rows, :] = jnp.where(lane == 0, mu, jnp.where(lane == 1, rstd, 0.0))
        cos = cos_ref[rows, :]
        sin = sin_ref[rows, :]

        qf = jnp.dot(hb, wq_ref[...], preferred_element_type=F32)
        q = _head_norm_rope(qf, seg_ref[...], qg_ref[...],
                            jnp.concatenate([cos] * reps, axis=1),
                            jnp.concatenate([sin] * reps, axis=1))
        q_ref[rows, :] = (q * (HEAD_DIM ** -0.5 * LOG2_E)).astype(BF16)

        kvf = jnp.dot(hb, wkv_ref[...], preferred_element_type=F32)
        k = _head_norm_rope(kvf[:, :KV_WIDTH], seg_ref[:KV_WIDTH, :KV_WIDTH], kg_ref[...], cos, sin)
        kb = k.astype(BF16)
        vb = kvf[:, KV_WIDTH:].astype(BF16)
        ones = jnp.ones((blk, HEAD_DIM), BF16)
        for j in range(N_KV_HEADS):
            k_ref[j, rows, :] = kb[:, j * HEAD_DIM:(j + 1) * HEAD_DIM]
            v_ref[j, rows, :] = jnp.concatenate([vb[:, j * HEAD_DIM:(j + 1) * HEAD_DIM], ones], axis=1)

        xl_ref[rows, :] = jnp.dot(hb, wx_ref[...], preferred_element_type=F32)
        yl_ref[rows, :] = jnp.dot(hb, wy_ref[...], preferred_element_type=F32)


def _in_proj(x2, B, S, ln_g, ln_b, wq, wkv, wx, wy, seg, qg, kg, cos_t, sin_t):
    T = B * S
    tm = TM_IN
    nt = S // tm
    const = lambda i: (0, 0)
    return pl.pallas_call(
        _in_proj_kernel,
        grid=(T // tm,),
        in_specs=[
            pl.BlockSpec((tm, D_MODEL), lambda i: (i, 0)),
            pl.BlockSpec((1, D_MODEL), const),
            pl.BlockSpec((1, D_MODEL), const),
            pl.BlockSpec((D_MODEL, D_ATTN), const),
            pl.BlockSpec((D_MODEL, 2 * KV_WIDTH), const),
            pl.BlockSpec((D_MODEL, D_LRU), const),
            pl.BlockSpec((D_MODEL, D_LRU), const),
            pl.BlockSpec((D_ATTN, D_ATTN), const),
            pl.BlockSpec((1, D_ATTN), const),
            pl.BlockSpec((1, KV_WIDTH), const),
            pl.BlockSpec((tm, KV_WIDTH), lambda i: (i % nt, 0)),
            pl.BlockSpec((tm, KV_WIDTH), lambda i: (i % nt, 0)),
        ],
        out_specs=[
            pl.BlockSpec((tm, D_ATTN), lambda i: (i, 0)),
            pl.BlockSpec((None, N_KV_HEADS, tm, HEAD_DIM), lambda i: (i // nt, 0, i % nt, 0)),
            pl.BlockSpec((None, N_KV_HEADS, tm, 2 * HEAD_DIM), lambda i: (i // nt, 0, i % nt, 0)),
            pl.BlockSpec((tm, D_LRU), lambda i: (i, 0)),
            pl.BlockSpec((tm, D_LRU), lambda i: (i, 0)),
            pl.BlockSpec((tm, LANES), lambda i: (i, 0)),
        ],
        out_shape=[
            jax.ShapeDtypeStruct((T, D_ATTN), BF16),
            jax.ShapeDtypeStruct((B, N_KV_HEADS, S, HEAD_DIM), BF16),
            jax.ShapeDtypeStruct((B, N_KV_HEADS, S, 2 * HEAD_DIM), BF16),
            jax.ShapeDtypeStruct((T, D_LRU), F32),
            jax.ShapeDtypeStruct((T, D_LRU), F32),
            jax.ShapeDtypeStruct((T, LANES), F32),
        ],
        compiler_params=pltpu.CompilerParams(
            dimension_semantics=("parallel",), vmem_limit_bytes=VMEM_LIMIT),
        name="in_proj",
    )(x2, ln_g, ln_b, wq, wkv, wx, wy, seg, qg, kg, cos_t, sin_t)


def _attn_kernel(q_ref, k_ref, v_ref, o_ref):
    nblk = k_ref.shape[0] // KEY_BLOCK
    outs = []
    for h in range(Q_PER_KV):
        qh = q_ref[:, h * HEAD_DIM:(h + 1) * HEAD_DIM]
        m = acc = None
        for b in range(nblk):
            kb = k_ref[b * KEY_BLOCK:(b + 1) * KEY_BLOCK, :]
            vb = v_ref[b * KEY_BLOCK:(b + 1) * KEY_BLOCK, :]
            s = lax.dot_general(qh, kb, (((1,), (1,)), ((), ())), preferred_element_type=F32)
            mb = jnp.max(s, axis=1, keepdims=True)
            if b == 0:
                m = mb
                acc = jnp.dot(jnp.exp2(s - m).astype(BF16), vb, preferred_element_type=F32)
            else:
                m_new = jnp.maximum(m, mb)
                acc = (jnp.exp2(m - m_new) * acc
                       + jnp.dot(jnp.exp2(s - m_new).astype(BF16), vb, preferred_element_type=F32))
                m = m_new
        outs.append(acc[:, :HEAD_DIM] / acc[:, HEAD_DIM:HEAD_DIM + 1])
    o_ref[...] = jnp.concatenate(outs, axis=1)


def _attention(q, k, v, B, S):
    T = B * S
    nq = S // TQ
    width = Q_PER_KV * HEAD_DIM
    return pl.pallas_call(
        _attn_kernel,
        grid=(B, N_KV_HEADS, nq),
        in_specs=[
            pl.BlockSpec((TQ, width), lambda b, j, t: (b * nq + t, j)),
            pl.BlockSpec((None, None, S, HEAD_DIM), lambda b, j, t: (b, j, 0, 0)),
            pl.BlockSpec((None, None, S, 2 * HEAD_DIM), lambda b, j, t: (b, j, 0, 0)),
        ],
        out_specs=pl.BlockSpec((TQ, width), lambda b, j, t: (b * nq + t, j)),
        out_shape=jax.ShapeDtypeStruct((T, D_ATTN), F32),
        compiler_params=pltpu.CompilerParams(
            dimension_semantics=("parallel", "parallel", "parallel"),
            vmem_limit_bytes=VMEM_LIMIT),
        name="attention",
    )(q, k, v)


def _gelu_tanh(x):
    c = math.sqrt(2.0 / math.pi)
    return 0.5 * x * (1.0 + jnp.tanh(c * (x + 0.044715 * (x * x * x))))


def _rg_lru_kernel(x_ref, y_ref, cw_ref, cb_ref, wg_ref, bg_ref, lam_ref, o_ref,
                   af_ref, uf_ref, ab_ref, ub_ref, hf_ref, pf_ref, hb_ref, pb_ref, cf_ref, cbk_ref):
    S, C = x_ref.shape
    nslab = C // LANES
    nseg = S // SEG
    assert nseg == SUBLANES
    lam = lam_ref[...]
    log2_a_per_r = (-LRU_C * LOG2_E) * jnp.logaddexp(-lam, 0.0)
    cw = cw_ref[...]
    cb = cb_ref[...]
    bg = bg_ref[...]

    def gate_block(s, carry):
        start = pl.multiple_of(s * SEG, SEG)
        seg_rows = pl.ds(s, SEG, stride=SUBLANES)
        cur = x_ref[pl.ds(start, SEG), :]
        prev = x_ref[pl.ds(pl.multiple_of(jnp.maximum(start - SUBLANES, 0), SUBLANES), SUBLANES), :]
        prev = jnp.where(s > 0, prev, 0.0)
        nxt = x_ref[pl.ds(pl.multiple_of(jnp.minimum(start + SEG, S - SUBLANES), SUBLANES), SUBLANES), :]
        nxt = jnp.where(s < nseg - 1, nxt, 0.0)
        ext = jnp.concatenate([prev, cur, nxt], axis=0)
        xc = cb
        for tap in range(CONV_WIDTH):
            shift = CONV_WIDTH // 2 - tap
            src = ext if shift == 0 else pltpu.roll(ext, shift % ext.shape[0], axis=0)
            xc = xc + src[SUBLANES:SUBLANES + SEG, :] * cw[tap:tap + 1, :]
        z = jnp.dot(xc.astype(BF16), wg_ref[...], preferred_element_type=F32) + bg
        for d, (a_ref, u_ref) in enumerate(((af_ref, uf_ref), (ab_ref, ub_ref))):
            rg = 1.0 / (1.0 + jnp.exp2(z[:, (2 * d) * C:(2 * d + 1) * C]))
            ig = 1.0 / (1.0 + jnp.exp2(z[:, (2 * d + 1) * C:(2 * d + 2) * C]))
            a = jnp.exp2(log2_a_per_r[d:d + 1, :] * rg)
            s1 = 1.0 - a * a
            u = jnp.where(s1 > 0.0, s1 * lax.rsqrt(s1), 0.0) * (ig * xc)
            for slab in range(nslab):
                lanes = slice(slab * LANES, (slab + 1) * LANES)
                a_ref[slab, seg_rows, :] = a[:, lanes]
                u_ref[slab, seg_rows, :] = u[:, lanes]
        return carry

    lax.fori_loop(0, nseg, gate_block, 0, unroll=2)

    def scan_step(j, carry):
        out = []
        for slab in range(nslab):
            hf, pf, hb, pb = carry[slab]
            rows = pl.ds(pl.multiple_of(j * SUBLANES, SUBLANES), SUBLANES)
            a = af_ref[slab, rows, :]
            hf = a * hf + uf_ref[slab, rows, :]
            pf = a * pf
            hf_ref[slab, rows, :] = hf
            pf_ref[slab, rows, :] = pf
            rows = pl.ds(pl.multiple_of((SEG - 1 - j) * SUBLANES, SUBLANES), SUBLANES)
            a = ab_ref[slab, rows, :]
            hb = a * hb + ub_ref[slab, rows, :]
            pb = a * pb
            hb_ref[slab, rows, :] = hb
            pb_ref[slab, rows, :] = pb
            out.append((hf, pf, hb, pb))
        return tuple(out)

    zero = jnp.zeros((SUBLANES, LANES), F32)
    one = jnp.ones((SUBLANES, LANES), F32)
    final = lax.fori_loop(0, SEG, scan_step, tuple((zero, one, zero, one) for _ in range(nslab)),
                          unroll=LRU_SCAN_UNROLL)

    for slab in range(nslab):
        hf, pf, hb, pb = final[slab]
        c = jnp.zeros((1, LANES), F32)
        fwd = [c]
        for sgm in range(1, nseg):
            c = hf[sgm - 1:sgm, :] + pf[sgm - 1:sgm, :] * c
            fwd.append(c)
        c = jnp.zeros((1, LANES), F32)
        bwd = [c]
        for sgm in range(nseg - 2, -1, -1):
            c = hb[sgm + 1:sgm + 2, :] + pb[sgm + 1:sgm + 2, :] * c
            bwd.append(c)
        cf_ref[slab] = jnp.concatenate(fwd, axis=0)
        cbk_ref[slab] = jnp.concatenate(bwd[::-1], axis=0)

    reps = LRU_COMBINE_ROWS // SUBLANES

    def combine(i, carry):
        rows = pl.ds(pl.multiple_of(i * LRU_COMBINE_ROWS, LRU_COMBINE_ROWS), LRU_COMBINE_ROWS)
        for slab in range(nslab):
            cf = jnp.concatenate([cf_ref[slab]] * reps, axis=0)
            cbk = jnp.concatenate([cbk_ref[slab]] * reps, axis=0)
            uf_ref[slab, rows, :] = ((hf_ref[slab, rows, :] + pf_ref[slab, rows, :] * cf)
                                     + (hb_ref[slab, rows, :] + pb_ref[slab, rows, :] * cbk))
        return carry

    lax.fori_loop(0, S // LRU_COMBINE_ROWS, combine, 0)

    def out_block(s, carry):
        start = pl.multiple_of(s * SEG, SEG)
        for slab in range(nslab):
            lanes = slice(slab * LANES, (slab + 1) * LANES)
            h = uf_ref[slab, pl.ds(s, SEG, stride=SUBLANES), :]
            o_ref[pl.ds(start, SEG), lanes] = h * _gelu_tanh(y_ref[pl.ds(start, SEG), lanes])
        return carry

    lax.fori_loop(0, nseg, out_block, 0)


def _rg_lru(xl, yl, B, S, conv_w, conv_b, wg, bg, lam):
    C = LRU_CH
    ncg = D_LRU // C
    nslab = C // LANES
    xl3 = xl.reshape(B, S, D_LRU)
    yl3 = yl.reshape(B, S, D_LRU)
    seg_buf = pltpu.VMEM((nslab, S, LANES), F32)
    carry_buf = pltpu.VMEM((nslab, S // SEG, LANES), F32)
    out = pl.pallas_call(
        _rg_lru_kernel,
        grid=(B, ncg),
        in_specs=[
            pl.BlockSpec((None, S, C), lambda b, c: (b, 0, c)),
            pl.BlockSpec((None, S, C), lambda b, c: (b, 0, c)),
            pl.BlockSpec((CONV_WIDTH, C), lambda b, c: (0, c)),
            pl.BlockSpec((1, C), lambda b, c: (0, c)),
            pl.BlockSpec((None, C, 4 * C), lambda b, c: (c, 0, 0)),
            pl.BlockSpec((None, 1, 4 * C), lambda b, c: (c, 0, 0)),
            pl.BlockSpec((2, C), lambda b, c: (0, c)),
        ],
        out_specs=pl.BlockSpec((None, S, C), lambda b, c: (b, 0, c)),
        out_shape=jax.ShapeDtypeStruct((B, S, D_LRU), F32),
        scratch_shapes=[seg_buf] * 8 + [carry_buf] * 2,
        compiler_params=pltpu.CompilerParams(
            dimension_semantics=("parallel", "parallel"), vmem_limit_bytes=VMEM_LIMIT),
        name="rg_lru",
    )(xl3, yl3, conv_w, conv_b, wg, bg, lam)
    return out.reshape(B * S, D_LRU)


ROUTER_ROWS = 24


def _router_gates_t(logits):
    row = lax.broadcasted_iota(jnp.int32, logits.shape, 0).astype(F32)
    neg = -jnp.inf
    big = float(2 * LANES)
    gl = jnp.where(row < N_GROUPS, logits, neg)
    gmax = jnp.max(gl, axis=0, keepdims=True)
    gidx = jnp.min(jnp.where(gl == gmax, row, big), axis=0, keepdims=True)
    g_sel_prob = 1.0 / jnp.sum(jnp.exp(gl - gmax), axis=0, keepdims=True)
    lo = EXPERT_LANE0 + EXPERTS_PER_GROUP * gidx
    el = jnp.where((row >= lo) & (row < lo + EXPERTS_PER_GROUP), logits, neg)
    v1 = jnp.max(el, axis=0, keepdims=True)
    i1 = jnp.min(jnp.where(el == v1, row, big), axis=0, keepdims=True)
    el2 = jnp.where(row == i1, neg, el)
    v2 = jnp.max(el2, axis=0, keepdims=True)
    i2 = jnp.min(jnp.where(el2 == v2, row, big), axis=0, keepdims=True)
    e2 = jnp.exp(v2 - v1)
    w1 = g_sel_prob / (1.0 + e2)
    w2 = g_sel_prob * e2 / (1.0 + e2)
    gates = jnp.where(row == i1, w1, jnp.where(row == i2, w2, 0.0))
    return jnp.where(row == 0.0, gidx, gates)


def _out_proj_kernel(x_ref, st_ref, attn_ref, lru_ref, lng_ref, lnb_ref, ag_ref, lg_ref, wo_ref,
                     g1_ref, b1_ref, wr_ref, br_ref, h1_ref, gate_ref):
    h = (x_ref[...] - st_ref[:, 0:1]) * st_ref[:, 1:2] * lng_ref[...] + lnb_ref[...]
    merged = jnp.concatenate(
        [_rms_norm_rows(attn_ref[...], ag_ref[...], LN_EPS),
         _rms_norm_rows(lru_ref[...], lg_ref[...], LN_EPS)], axis=1).astype(BF16)
    mix = jnp.dot(merged, wo_ref[...], preferred_element_type=F32)
    h1 = _layer_norm(DEEPNORM_ALPHA * h + mix, g1_ref[...], b1_ref[...])
    h1_ref[...] = h1
    hi = h1.astype(BF16)
    lo = (h1 - hi.astype(F32)).astype(BF16)
    nt = (((1,), (1,)), ((), ()))
    t_hi = lax.dot_general(wr_ref[...], hi, nt, preferred_element_type=F32)
    t_lo = lax.dot_general(wr_ref[:ROUTER_PAD, :], lo, nt, preferred_element_type=F32)
    logits_t = (t_hi[:ROUTER_ROWS] + t_hi[ROUTER_PAD:ROUTER_PAD + ROUTER_ROWS]
                + t_lo[:ROUTER_ROWS] + br_ref[...])
    route_t = _router_gates_t(logits_t)
    pad = jnp.zeros((ROUTER_PAD - ROUTER_ROWS, route_t.shape[1]), F32)
    gate_ref[...] = jnp.concatenate([route_t, pad], axis=0).T


def _out_proj(x2, stats, attn, lru, ln_g, ln_b, ag, lg, wo, g1, b1, wr, br):
    T = x2.shape[0]
    tm = TM_OUT
    const = lambda i: (0, 0)
    row = lambda i: (i, 0)
    return pl.pallas_call(
        _out_proj_kernel,
        grid=(T // tm,),
        in_specs=[
            pl.BlockSpec((tm, D_MODEL), row),
            pl.BlockSpec((tm, LANES), row),
            pl.BlockSpec((tm, D_ATTN), row),
            pl.BlockSpec((tm, D_LRU), row),
            pl.BlockSpec((1, D_MODEL), const),
            pl.BlockSpec((1, D_MODEL), const),
            pl.BlockSpec((1, D_ATTN), const),
            pl.BlockSpec((1, D_LRU), const),
            pl.BlockSpec((D_MODEL, D_MODEL), const),
            pl.BlockSpec((1, D_MODEL), const),
            pl.BlockSpec((1, D_MODEL), const),
            pl.BlockSpec((2 * ROUTER_PAD, D_MODEL), const),
            pl.BlockSpec((ROUTER_ROWS, 1), const),
        ],
        out_specs=[pl.BlockSpec((tm, D_MODEL), row), pl.BlockSpec((tm, ROUTER_PAD), row)],
        out_shape=[jax.ShapeDtypeStruct((T, D_MODEL), F32),
                   jax.ShapeDtypeStruct((T, ROUTER_PAD), F32)],
        compiler_params=pltpu.CompilerParams(
            dimension_semantics=("parallel",), vmem_limit_bytes=VMEM_LIMIT),
        name="out_proj",
    )(x2, stats, attn, lru, ln_g, ln_b, ag, lg, wo, g1, b1, wr, br)


def _lane_scalar(row, idx):
    lane = lax.broadcasted_iota(jnp.int32, row.shape, 1)
    return jnp.sum(jnp.where(lane == idx, row, 0.0))


def _moe_kernel(h1_ref, route_ref, tril_ref, w13_ref, w2_ref, g2_ref, b2_ref, o_ref,
                pt_ref, xs_ref, gs_ref, ys_ref):
    for t in range(MOE_TILES_PER_STEP):
        rows = pl.ds(t * TM_MOE, TM_MOE)
        _moe_tile(h1_ref.at[rows, :], route_ref.at[rows, :], tril_ref, w13_ref, w2_ref, g2_ref, b2_ref,
                  o_ref.at[rows, :], pt_ref.at[t], xs_ref.at[t], gs_ref.at[t], ys_ref.at[t])


def _moe_tile(h1_ref, route_ref, tril_ref, w13_ref, w2_ref, g2_ref, b2_ref, o_ref,
              pt_ref, xs_ref, gs_ref, ys_ref):
    tm = h1_ref.shape[0]
    c = MOE_CHUNK
    h1 = h1_ref[...]
    route = route_ref[...]
    lane = lax.broadcasted_iota(jnp.int32, route.shape, 1)
    gid = jnp.sum(jnp.where(lane == 0, route, 0.0), axis=1, keepdims=True)
    onehot = jnp.where(lane.astype(F32) == gid, 1.0, 0.0)
    half = tm // 2
    ohb = onehot.astype(BF16)
    tril = tril_ref[...]
    cnt_top = jnp.sum(onehot[:half], axis=0, keepdims=True)
    before = jnp.concatenate(
        [jnp.dot(tril, ohb[:half], preferred_element_type=F32),
         jnp.dot(tril, ohb[half:], preferred_element_type=F32) + cnt_top], axis=0)
    cnt = jnp.broadcast_to(cnt_top + jnp.sum(onehot[half:], axis=0, keepdims=True), (SUBLANES, LANES))
    start = pltpu.roll(cnt, 1, axis=1) + pltpu.roll(cnt, 2, axis=1) + pltpu.roll(cnt, 3, axis=1)
    start = start[0:1, :]
    dest = jnp.sum(onehot * (before + start), axis=1, keepdims=True)
    col = lax.broadcasted_iota(jnp.int32, (tm, tm), 1).astype(F32)
    pt_ref[...] = jnp.where(col == dest, 1.0, 0.0).astype(BF16)
    dest_row = jnp.broadcast_to(dest, (tm, LANES)).T[0:1, :]
    row = lax.broadcasted_iota(jnp.int32, (tm, tm), 0).astype(F32)
    p = jnp.where(row == dest_row, 1.0, 0.0).astype(BF16)

    xs_ref[...] = jnp.dot(p, h1.astype(BF16), preferred_element_type=F32).astype(BF16)
    rhi = route.astype(BF16)
    rlo = (route - rhi.astype(F32)).astype(BF16)
    gs2 = jnp.dot(p, jnp.concatenate([rhi, rlo], axis=1), preferred_element_type=F32)
    gs_ref[...] = gs2[:, :LANES] + gs2[:, LANES:]
    ys_ref[...] = jnp.zeros_like(ys_ref)

    bounds = [jnp.int32(0)] + [_lane_scalar(start, g).astype(jnp.int32) for g in range(1, N_GROUPS)]
    bounds.append(jnp.int32(tm))
    nchunk = tm // c
    nslot = nchunk + N_GROUPS - 1
    slot_j = [jnp.int32(0)] * nslot
    slot_g = [jnp.int32(0)] * nslot
    count = jnp.int32(0)
    for j in range(nchunk):
        for g in range(N_GROUPS):
            active = ((bounds[g + 1] > bounds[g]) & (bounds[g] < (j + 1) * c) & (bounds[g + 1] > j * c))
            for s in range(nslot):
                hit = active & (count == s)
                slot_j[s] = jnp.where(hit, j, slot_j[s])
                slot_g[s] = jnp.where(hit, g, slot_g[s])
            count = count + active.astype(jnp.int32)
    for s in range(nslot):
        g = slot_g[s]
        rows = pl.ds(pl.multiple_of(slot_j[s] * c, c), c)
        xj = xs_ref[rows, :]
        gsj = gs_ref[rows, :] * (s < count).astype(F32)
        lane_c = lax.broadcasted_iota(jnp.int32, gsj.shape, 1)
        hids = []
        for k in range(EXPERTS_PER_GROUP):
            e = g * EXPERTS_PER_GROUP + k
            up = jnp.dot(xj, w13_ref[e], preferred_element_type=F32)
            u1 = up[:, :D_EXPERT]
            ge = jnp.sum(jnp.where(lane_c == EXPERT_LANE0 + e, gsj, 0.0), axis=1, keepdims=True)
            hids.append(((u1 * jax.nn.sigmoid(u1)) * up[:, D_EXPERT:] * ge).astype(BF16))
        ys_ref[rows, :] += jnp.dot(jnp.concatenate(hids, axis=1), w2_ref[g],
                                   preferred_element_type=F32)

    ffn = jnp.dot(pt_ref[...], ys_ref[...].astype(BF16), preferred_element_type=F32)
    o_ref[...] = _layer_norm(DEEPNORM_ALPHA * h1 + ffn, g2_ref[...], b2_ref[...])


def _moe(h1, route, tril, w13, w2, g2, b2):
    T = h1.shape[0]
    tm = TM_MOE
    nt = MOE_TILES_PER_STEP
    gw = EXPERTS_PER_GROUP * D_EXPERT
    const2 = lambda i: (0, 0)
    const3 = lambda i: (0, 0, 0)
    resident = pl.Buffered(1)
    return pl.pallas_call(
        _moe_kernel,
        grid=(T // (nt * tm),),
        in_specs=[
            pl.BlockSpec((nt * tm, D_MODEL), lambda i: (i, 0)),
            pl.BlockSpec((nt * tm, ROUTER_PAD), lambda i: (i, 0)),
            pl.BlockSpec((tm // 2, tm // 2), const2, pipeline_mode=resident),
            pl.BlockSpec((N_EXPERTS, D_MODEL, 2 * D_EXPERT), const3, pipeline_mode=resident),
            pl.BlockSpec((N_GROUPS, gw, D_MODEL), const3, pipeline_mode=resident),
            pl.BlockSpec((1, D_MODEL), const2),
            pl.BlockSpec((1, D_MODEL), const2),
        ],
        out_specs=pl.BlockSpec((nt * tm, D_MODEL), lambda i: (i, 0)),
        out_shape=jax.ShapeDtypeStruct((T, D_MODEL), F32),
        scratch_shapes=[pltpu.VMEM((nt, tm, tm), BF16), pltpu.VMEM((nt, tm, D_MODEL), BF16),
                        pltpu.VMEM((nt, tm, ROUTER_PAD), F32), pltpu.VMEM((nt, tm, D_MODEL), F32)],
        compiler_params=pltpu.CompilerParams(
            dimension_semantics=("parallel",), vmem_limit_bytes=MOE_VMEM_LIMIT),
        name="moe",
    )(h1, route, tril, w13, w2, g2, b2)


def _rope_tables(S):
    t = jnp.arange(S, dtype=jnp.int32)
    row = (t // GRID_W).astype(F32)
    col = (t % GRID_W).astype(F32)
    half = ROPE_SECTION // 2
    inv_freq = ROPE_THETA ** (-jnp.arange(half, dtype=F32) / half)
    ang_r = row[:, None] * inv_freq
    ang_c = col[:, None] * inv_freq
    cos_h = jnp.concatenate([jnp.cos(ang_r)] * 2 + [jnp.cos(ang_c)] * 2, axis=1)
    sin_h = jnp.concatenate([-jnp.sin(ang_r), jnp.sin(ang_r), -jnp.sin(ang_c), jnp.sin(ang_c)], axis=1)
    reps = KV_WIDTH // HEAD_DIM
    return jnp.tile(cos_h, (1, reps)), jnp.tile(sin_h, (1, reps))


def _block_diag(w):
    H, d, _ = w.shape
    eye = jnp.eye(H, dtype=w.dtype)
    return (eye[:, None, :, None] * w[:, :, None, :]).reshape(H * d, H * d)


def _lru_gate_weights(wa, ba, wx, bx):
    C = LRU_CH
    ncg = D_LRU // C
    hpg = C // LRU_HEAD_DIM
    ws, bs = [], []
    for c in range(ncg):
        hs = slice(c * hpg, (c + 1) * hpg)
        cols, bias = [], []
        for d in range(2):
            cols += [_block_diag(wa[d, hs]), _block_diag(wx[d, hs])]
            bias += [ba[d, hs].reshape(C), bx[d, hs].reshape(C)]
        ws.append(jnp.concatenate(cols, axis=1))
        bs.append(jnp.concatenate(bias)[None, :])
    return (-LOG2_E * jnp.stack(ws)).astype(BF16), -LOG2_E * jnp.stack(bs)


def _trunk(x, P):
    B, S, _ = x.shape
    x2 = x.reshape(B * S, D_MODEL)
    q, k, v, xl, yl, stats = _in_proj(x2, B, S, P["ln_in_g"], P["ln_in_b"], P["wq"], P["wkv"], P["wx"],
                               P["wy"], P["seg"], P["qg"], P["kg"], P["cos"], P["sin"])
    attn = _attention(q, k, v, B, S)
    lru = _rg_lru(xl, yl, B, S, P["conv_w"], P["conv_b"], P["wg"], P["bg"], P["lam"])
    h1, route = _out_proj(x2, stats, attn, lru, P["ln_in_g"], P["ln_in_b"], P["ag"], P["lg"], P["wo"],
                          P["g1"], P["b1"], P["wr"], P["br"])
    out = _moe(h1, route, P["tril"], P["w13"], P["w2"], P["g2"], P["b2"])
    return out.reshape(B, S, D_MODEL)


def kernel(x_prompt, x_sample, ln_in_g, ln_in_b, w_in, conv_w, conv_b, lru_wa, lru_ba, lru_wx, lru_bx, lru_lambda, q_norm_g, k_norm_g, attn_out_g, lru_out_g, w_out, ln1_g, ln1_b, router_wg, router_bg, router_we, router_be, exp_w1, exp_w3, exp_w2, ln2_g, ln2_b):
    assert w_in.shape[0] == DEPTH == 1
    S = x_prompt.shape[1]
    l = 0
    w = w_in[l]
    c0, c1, c2 = D_ATTN, D_ATTN + 2 * KV_WIDTH, D_ATTN + 2 * KV_WIDTH + D_LRU
    seg = _block_diag(jnp.full((N_Q_HEADS, HEAD_DIM, HEAD_DIM), 1.0 / HEAD_DIM, F32)).astype(BF16)
    cos_t, sin_t = _rope_tables(S)
    wg, bg = _lru_gate_weights(lru_wa[l], lru_ba[l], lru_wx[l], lru_bx[l])
    wr = jnp.concatenate([router_wg[l], router_we[l].reshape(D_MODEL, N_EXPERTS)], axis=1)
    wr = jnp.pad(wr, ((0, 0), (0, ROUTER_PAD - wr.shape[1])))
    wrh = wr.astype(BF16)
    wrl = (wr - wrh.astype(F32)).astype(BF16)
    wr_t = jnp.concatenate([wrh.T, wrl.T], axis=0)
    br = jnp.concatenate([router_bg[l], router_be[l].reshape(N_EXPERTS)])
    br = jnp.pad(br, (0, ROUTER_ROWS - br.shape[0]))[:, None]
    P = dict(
        ln_in_g=ln_in_g[None, :], ln_in_b=ln_in_b[None, :],
        wq=w[:, :c0].astype(BF16), wkv=w[:, c0:c1].astype(BF16),
        wx=w[:, c1:c2].astype(BF16), wy=w[:, c2:].astype(BF16),
        seg=seg, qg=jnp.tile(q_norm_g[l], N_Q_HEADS)[None, :],
        kg=jnp.tile(k_norm_g[l], N_KV_HEADS)[None, :], cos=cos_t, sin=sin_t,
        conv_w=conv_w[l], conv_b=conv_b[l][None, :], wg=wg, bg=bg, lam=lru_lambda[l],
        ag=attn_out_g[l][None, :], lg=lru_out_g[l][None, :], wo=w_out[l].astype(BF16),
        g1=ln1_g[l][None, :], b1=ln1_b[l][None, :], wr=wr_t, br=br,
        w13=jnp.concatenate([exp_w1[l], exp_w3[l]], axis=2).astype(BF16),
        w2=exp_w2[l].reshape(N_GROUPS, EXPERTS_PER_GROUP * D_EXPERT, D_MODEL).astype(BF16),
        g2=ln2_g[l][None, :], b2=ln2_b[l][None, :],
        tril=jnp.tril(jnp.ones((TM_MOE // 2, TM_MOE // 2), F32), -1).astype(BF16),
    )
    return (_trunk(x_prompt, P), _trunk(x_sample, P))
```

```python
import functools
import math

import jax
import jax.numpy as jnp
from jax import lax
from jax.experimental import pallas as pl
from jax.experimental.pallas import tpu as pltpu

F32 = jnp.float32
BF16 = jnp.bfloat16

D_MODEL = 1024
GRID_W = 64
D_ATTN = 512
HEAD_DIM = 64
N_Q_HEADS = 8
N_KV_HEADS = 2
Q_PER_KV = 4
KV_WIDTH = 128
ROPE_THETA = 10000.0
ROPE_SECTION = 32
QK_EPS = 1e-6
D_LRU = 512
N_LRU_HEADS = 8
LRU_HEAD_DIM = 64
CONV_WIDTH = 4
LRU_C = 8.0
N_GROUPS = 4
EXPERTS_PER_GROUP = 4
N_EXPERTS = 16
D_EXPERT = 256
LN_EPS = 1e-5
DEPTH = 1
DEEPNORM_ALPHA = (2.0 * DEPTH) ** 0.25
LOG2_E = math.log2(math.e)

LANES = 128
SUBLANES = 8
VMEM_LIMIT = 48 * 1024 * 1024

TM_IN = 1024
IN_PROJ_SPLIT = 4
TM_OUT = 1024
TQ = 1024
KEY_BLOCK = 1024
LRU_CH = 256
SEG = 256
LRU_SCAN_UNROLL = 4
LRU_COMBINE_ROWS = 128
TM_MOE = 512
MOE_CHUNK = 144
MOE_ALIGN_LOG2 = 4
MOE_ALIGN = 1 << MOE_ALIGN_LOG2
MOE_TILES_PER_STEP = 1
MOE_VMEM_LIMIT = 56 * 1024 * 1024
ROUTER_PAD = LANES
EXPERT_LANE0 = N_GROUPS


def _layer_norm(x, g, b):
    mu = jnp.mean(x, axis=-1, keepdims=True)
    xc = x - mu
    var = jnp.mean(xc * xc, axis=-1, keepdims=True)
    return xc * lax.rsqrt(var + LN_EPS) * g + b


def _rms_norm_rows(x, g, eps):
    return x * lax.rsqrt(jnp.mean(x * x, axis=-1, keepdims=True) + eps) * g


def _head_norm_rope(xf, seg, g, cos, sin):
    ms = jnp.dot((xf * xf).astype(BF16), seg, preferred_element_type=F32)
    xn = xf * lax.rsqrt(ms + QK_EPS) * g
    n = xn.shape[1]
    half = ROPE_SECTION // 2
    up = pltpu.roll(xn, n - half, axis=1)
    dn = pltpu.roll(xn, half, axis=1)
    lane = lax.broadcasted_iota(jnp.int32, xn.shape, 1)
    first = (lane % ROPE_SECTION) < half
    partner = jnp.where(first, up, dn)
    return xn * cos + partner * sin


def _in_proj_kernel(x_ref, g_ref, b_ref, wq_ref, wkv_ref, wx_ref, wy_ref, seg_ref,
                    qg_ref, kg_ref, cos_ref, sin_ref,
                    q_ref, k_ref, v_ref, xl_ref, yl_ref, st_ref):
    reps = D_ATTN // KV_WIDTH
    blk = x_ref.shape[0] // IN_PROJ_SPLIT
    for i in range(IN_PROJ_SPLIT):
        rows = slice(i * blk, (i + 1) * blk)
        x = x_ref[rows, :]
        mu = jnp.mean(x, axis=-1, keepdims=True)
        xc = x - mu
        rstd = lax.rsqrt(jnp.mean(xc * xc, axis=-1, keepdims=True) + LN_EPS)
        hb = (xc * rstd * g_ref[...] + b_ref[...]).astype(BF16)
        lane = lax.broadcasted_iota(jnp.int32, (blk, LANES), 1)
        st_ref[rows, :] = jnp.where(lane == 0, mu, jnp.where(lane == 1, rstd, 0.0))
        cos = cos_ref[rows, :]
        sin = sin_ref[rows, :]

        qf = jnp.dot(hb, wq_ref[...], preferred_element_type=F32)
        q = _head_norm_rope(qf, seg_ref[...], qg_ref[...],
                            jnp.concatenate([cos] * reps, axis=1),
                            jnp.concatenate([sin] * reps, axis=1))
        q_ref[rows, :] = (q * (HEAD_DIM ** -0.5 * LOG2_E)).astype(BF16)

        kvf = jnp.dot(hb, wkv_ref[...], preferred_element_type=F32)
        k = _head_norm_rope(kvf[:, :KV_WIDTH], seg_ref[:KV_WIDTH, :KV_WIDTH], kg_ref[...], cos, sin)
        kb = k.astype(BF16)
        vb = kvf[:, KV_WIDTH:].astype(BF16)
        ones = jnp.ones((blk, HEAD_DIM), BF16)
        for j in range(N_KV_HEADS):
            k_ref[j, rows, :] = kb[:, j * HEAD_DIM:(j + 1) * HEAD_DIM]
            v_ref[j, rows, :] = jnp.concatenate([vb[:, j * HEAD_DIM:(j + 1) * HEAD_DIM], ones], axis=1)

        xl_ref[rows, :] = jnp.dot(hb, wx_ref[...], preferred_element_type=F32)
        yl_ref[rows, :] = jnp.dot(hb, wy_ref[...], preferred_element_type=F32)


def _in_proj(x2, B, S, ln_g, ln_b, wq, wkv, wx, wy, seg, qg, kg, cos_t, sin_t):
    T = B * S
    tm = TM_IN
    nt = S // tm
    const = lambda i: (0, 0)
    return pl.pallas_call(
        _in_proj_kernel,
        grid=(T // tm,),
        in_specs=[
            pl.BlockSpec((tm, D_MODEL), lambda i: (i, 0)),
            pl.BlockSpec((1, D_MODEL), const),
            pl.BlockSpec((1, D_MODEL), const),
            pl.BlockSpec((D_MODEL, D_ATTN), const),
            pl.BlockSpec((D_MODEL, 2 * KV_WIDTH), const),
            pl.BlockSpec((D_MODEL, D_LRU), const),
            pl.BlockSpec((D_MODEL, D_LRU), const),
            pl.BlockSpec((D_ATTN, D_ATTN), const),
            pl.BlockSpec((1, D_ATTN), const),
            pl.BlockSpec((1, KV_WIDTH), const),
            pl.BlockSpec((tm, KV_WIDTH), lambda i: (i % nt, 0)),
            pl.BlockSpec((tm, KV_WIDTH), lambda i: (i % nt, 0)),
        ],
        out_specs=[
            pl.BlockSpec((tm, D_ATTN), lambda i: (i, 0)),
            pl.BlockSpec((None, N_KV_HEADS, tm, HEAD_DIM), lambda i: (i // nt, 0, i % nt, 0)),
            pl.BlockSpec((None, N_KV_HEADS, tm, 2 * HEAD_DIM), lambda i: (i // nt, 0, i % nt, 0)),
            pl.BlockSpec((tm, D_LRU), lambda i: (i, 0)),
            pl.BlockSpec((tm, D_LRU), lambda i: (i, 0)),
            pl.BlockSpec((tm, LANES), lambda i: (i, 0)),
        ],
        out_shape=[
            jax.ShapeDtypeStruct((T, D_ATTN), BF16),
            jax.ShapeDtypeStruct((B, N_KV_HEADS, S, HEAD_DIM), BF16),
            jax.ShapeDtypeStruct((B, N_KV_HEADS, S, 2 * HEAD_DIM), BF16),
            jax.ShapeDtypeStruct((T, D_LRU), F32),
            jax.ShapeDtypeStruct((T, D_LRU), F32),
            jax.ShapeDtypeStruct((T, LANES), F32),
        ],
        compiler_params=pltpu.CompilerParams(
            dimension_semantics=("parallel",), vmem_limit_bytes=VMEM_LIMIT),
        name="in_proj",
    )(x2, ln_g, ln_b, wq, wkv, wx, wy, seg, qg, kg, cos_t, sin_t)


def _attn_kernel(q_ref, k_ref, v_ref, o_ref):
    nblk = k_ref.shape[0] // KEY_BLOCK
    outs = []
    for h in range(Q_PER_KV):
        qh = q_ref[:, h * HEAD_DIM:(h + 1) * HEAD_DIM]
        m = acc = None
        for b in range(nblk):
            kb = k_ref[b * KEY_BLOCK:(b + 1) * KEY_BLOCK, :]
            vb = v_ref[b * KEY_BLOCK:(b + 1) * KEY_BLOCK, :]
            s = lax.dot_general(qh, kb, (((1,), (1,)), ((), ())), preferred_element_type=F32)
            mb = jnp.max(s, axis=1, keepdims=True)
            if b == 0:
                m = mb
                acc = jnp.dot(jnp.exp2(s - m).astype(BF16), vb, preferred_element_type=F32)
            else:
                m_new = jnp.maximum(m, mb)
                acc = (jnp.exp2(m - m_new) * acc
                       + jnp.dot(jnp.exp2(s - m_new).astype(BF16), vb, preferred_element_type=F32))
                m = m_new
        outs.append(acc[:, :HEAD_DIM] / acc[:, HEAD_DIM:HEAD_DIM + 1])
    o_ref[...] = jnp.concatenate(outs, axis=1)


def _attention(q, k, v, B, S):
    T = B * S
    nq = S // TQ
    width = Q_PER_KV * HEAD_DIM
    return pl.pallas_call(
        _attn_kernel,
        grid=(B, N_KV_HEADS, nq),
        in_specs=[
            pl.BlockSpec((TQ, width), lambda b, j, t: (b * nq + t, j)),
            pl.BlockSpec((None, None, S, HEAD_DIM), lambda b, j, t: (b, j, 0, 0)),
            pl.BlockSpec((None, None, S, 2 * HEAD_DIM), lambda b, j, t: (b, j, 0, 0)),
        ],
        out_specs=pl.BlockSpec((TQ, width), lambda b, j, t: (b * nq + t, j)),
        out_shape=jax.ShapeDtypeStruct((T, D_ATTN), F32),
        compiler_params=pltpu.CompilerParams(
            dimension_semantics=("parallel", "parallel", "parallel"),
            vmem_limit_bytes=VMEM_LIMIT),
        name="attention",
    )(q, k, v)


def _gelu_tanh(x):
    c = math.sqrt(2.0 / math.pi)
    return 0.5 * x * (1.0 + jnp.tanh(c * (x + 0.044715 * (x * x * x))))


def _rg_lru_kernel(x_ref, y_ref, cw_ref, cb_ref, wg_ref, bg_ref, lam_ref, o_ref,
                   af_ref, uf_ref, ab_ref, ub_ref, hf_ref, pf_ref, hb_ref, pb_ref, cf_ref, cbk_ref):
    S, C = x_ref.shape
    nslab = C // LANES
    nseg = S // SEG
    assert nseg == SUBLANES
    lam = lam_ref[...]
    log2_a_per_r = (-LRU_C * LOG2_E) * jnp.logaddexp(-lam, 0.0)
    cw = cw_ref[...]
    cb = cb_ref[...]
    bg = bg_ref[...]

    def gate_block(s, carry):
        start = pl.multiple_of(s * SEG, SEG)
        seg_rows = pl.ds(s, SEG, stride=SUBLANES)
        cur = x_ref[pl.ds(start, SEG), :]
        prev = x_ref[pl.ds(pl.multiple_of(jnp.maximum(start - SUBLANES, 0), SUBLANES), SUBLANES), :]
        prev = jnp.where(s > 0, prev, 0.0)
        nxt = x_ref[pl.ds(pl.multiple_of(jnp.minimum(start + SEG, S - SUBLANES), SUBLANES), SUBLANES), :]
        nxt = jnp.where(s < nseg - 1, nxt, 0.0)
        ext = jnp.concatenate([prev, cur, nxt], axis=0)
        xc = cb
        for tap in range(CONV_WIDTH):
            shift = CONV_WIDTH // 2 - tap
            src = ext if shift == 0 else pltpu.roll(ext, shift % ext.shape[0], axis=0)
            xc = xc + src[SUBLANES:SUBLANES + SEG, :] * cw[tap:tap + 1, :]
        z = jnp.dot(xc.astype(BF16), wg_ref[...], preferred_element_type=F32) + bg
        for d, (a_ref, u_ref) in enumerate(((af_ref, uf_ref), (ab_ref, ub_ref))):
            rg = 1.0 / (1.0 + jnp.exp2(z[:, (2 * d) * C:(2 * d + 1) * C]))
            ig = 1.0 / (1.0 + jnp.exp2(z[:, (2 * d + 1) * C:(2 * d + 2) * C]))
            a = jnp.exp2(log2_a_per_r[d:d + 1, :] * rg)
            s1 = 1.0 - a * a
            u = jnp.where(s1 > 0.0, s1 * lax.rsqrt(s1), 0.0) * (ig * xc)
            for slab in range(nslab):
                lanes = slice(slab * LANES, (slab + 1) * LANES)
                a_ref[slab, seg_rows, :] = a[:, lanes]
                u_ref[slab, seg_rows, :] = u[:, lanes]
        return carry

    lax.fori_loop(0, nseg, gate_block, 0, unroll=2)

    def scan_step(j, carry):
        out = []
        for slab in range(nslab):
            hf, pf, hb, pb = carry[slab]
            rows = pl.ds(pl.multiple_of(j * SUBLANES, SUBLANES), SUBLANES)
            a = af_ref[slab, rows, :]
            hf = a * hf + uf_ref[slab, rows, :]
            pf = a * pf
            hf_ref[slab, rows, :] = hf
            pf_ref[slab, rows, :] = pf
            rows = pl.ds(pl.multiple_of((SEG - 1 - j) * SUBLANES, SUBLANES), SUBLANES)
            a = ab_ref[slab, rows, :]
            hb = a * hb + ub_ref[slab, rows, :]
            pb = a * pb
            hb_ref[slab, rows, :] = hb
            pb_ref[slab, rows, :] = pb
            out.append((hf, pf, hb, pb))
        return tuple(out)

    zero = jnp.zeros((SUBLANES, LANES), F32)
    one = jnp.ones((SUBLANES, LANES), F32)
    final = lax.fori_loop(0, SEG, scan_step, tuple((zero, one, zero, one) for _ in range(nslab)),
                          unroll=LRU_SCAN_UNROLL)

    for slab in range(nslab):
        hf, pf, hb, pb = final[slab]
        c = jnp.zeros((1, LANES), F32)
        fwd = [c]
        for sgm in range(1, nseg):
            c = hf[sgm - 1:sgm, :] + pf[sgm - 1:sgm, :] * c
            fwd.append(c)
        c = jnp.zeros((1, LANES), F32)
        bwd = [c]
        for sgm in range(nseg - 2, -1, -1):
            c = hb[sgm + 1:sgm + 2, :] + pb[sgm + 1:sgm + 2, :] * c
            bwd.append(c)
        cf_ref[slab] = jnp.concatenate(fwd, axis=0)
        cbk_ref[slab] = jnp.concatenate(bwd[::-1], axis=0)

    reps = LRU_COMBINE_ROWS // SUBLANES

    def combine(i, carry):
        rows = pl.ds(pl.multiple_of(i * LRU_COMBINE_ROWS, LRU_COMBINE_ROWS), LRU_COMBINE_ROWS)
        for slab in range(nslab):
            cf = jnp.concatenate([cf_ref[slab]] * reps, axis=0)
            cbk = jnp.concatenate([cbk_ref[slab]] * reps, axis=0)
            uf_ref[slab, rows, :] = ((hf_ref[slab, rows, :] + pf_ref[slab, rows, :] * cf)
                                     + (hb_ref[slab, rows, :] + pb_ref[slab, rows, :] * cbk))
        return carry

    lax.fori_loop(0, S // LRU_COMBINE_ROWS, combine, 0)

    def out_block(s, carry):
        start = pl.multiple_of(s * SEG, SEG)
        for slab in range(nslab):
            lanes = slice(slab * LANES, (slab + 1) * LANES)
            h = uf_ref[slab, pl.ds(s, SEG, stride=SUBLANES), :]
            o_ref[pl.ds(start, SEG), lanes] = h * _gelu_tanh(y_ref[pl.ds(start, SEG), lanes])
        return carry

    lax.fori_loop(0, nseg, out_block, 0)


def _rg_lru(xl, yl, B, S, conv_w, conv_b, wg, bg, lam):
    C = LRU_CH
    ncg = D_LRU // C
    nslab = C // LANES
    xl3 = xl.reshape(B, S, D_LRU)
    yl3 = yl.reshape(B, S, D_LRU)
    seg_buf = pltpu.VMEM((nslab, S, LANES), F32)
    carry_buf = pltpu.VMEM((nslab, S // SEG, LANES), F32)
    out = pl.pallas_call(
        _rg_lru_kernel,
        grid=(B, ncg),
        in_specs=[
            pl.BlockSpec((None, S, C), lambda b, c: (b, 0, c)),
            pl.BlockSpec((None, S, C), lambda b, c: (b, 0, c)),
            pl.BlockSpec((CONV_WIDTH, C), lambda b, c: (0, c)),
            pl.BlockSpec((1, C), lambda b, c: (0, c)),
            pl.BlockSpec((None, C, 4 * C), lambda b, c: (c, 0, 0)),
            pl.BlockSpec((None, 1, 4 * C), lambda b, c: (c, 0, 0)),
            pl.BlockSpec((2, C), lambda b, c: (0, c)),
        ],
        out_specs=pl.BlockSpec((None, S, C), lambda b, c: (b, 0, c)),
        out_shape=jax.ShapeDtypeStruct((B, S, D_LRU), F32),
        scratch_shapes=[seg_buf] * 8 + [carry_buf] * 2,
        compiler_params=pltpu.CompilerParams(
            dimension_semantics=("parallel", "parallel"), vmem_limit_bytes=VMEM_LIMIT),
        name="rg_lru",
    )(xl3, yl3, conv_w, conv_b, wg, bg, lam)
    return out.reshape(B * S, D_LRU)


ROUTER_ROWS = 24


def _router_gates_t(logits):
    row = lax.broadcasted_iota(jnp.int32, logits.shape, 0).astype(F32)
    neg = -jnp.inf
    big = float(2 * LANES)
    gl = jnp.where(row < N_GROUPS, logits, neg)
    gmax = jnp.max(gl, axis=0, keepdims=True)
    gidx = jnp.min(jnp.where(gl == gmax, row, big), axis=0, keepdims=True)
    g_sel_prob = 1.0 / jnp.sum(jnp.exp(gl - gmax), axis=0, keepdims=True)
    lo = EXPERT_LANE0 + EXPERTS_PER_GROUP * gidx
    el = jnp.where((row >= lo) & (row < lo + EXPERTS_PER_GROUP), logits, neg)
    v1 = jnp.max(el, axis=0, keepdims=True)
    i1 = jnp.min(jnp.where(el == v1, row, big), axis=0, keepdims=True)
    el2 = jnp.where(row == i1, neg, el)
    v2 = jnp.max(el2, axis=0, keepdims=True)
    i2 = jnp.min(jnp.where(el2 == v2, row, big), axis=0, keepdims=True)
    e2 = jnp.exp(v2 - v1)
    w1 = g_sel_prob / (1.0 + e2)
    w2 = g_sel_prob * e2 / (1.0 + e2)
    gates = jnp.where(row == i1, w1, jnp.where(row == i2, w2, 0.0))
    return jnp.where(row == 0.0, gidx, gates)


def _out_proj_kernel(x_ref, st_ref, attn_ref, lru_ref, lng_ref, lnb_ref, ag_ref, lg_ref, wo_ref,
                     g1_ref, b1_ref, wr_ref, br_ref, h1_ref, gate_ref):
    h = (x_ref[...] - st_ref[:, 0:1]) * st_ref[:, 1:2] * lng_ref[...] + lnb_ref[...]
    merged = jnp.concatenate(
        [_rms_norm_rows(attn_ref[...], ag_ref[...], LN_EPS),
         _rms_norm_rows(lru_ref[...], lg_ref[...], LN_EPS)], axis=1).astype(BF16)
    mix = jnp.dot(merged, wo_ref[...], preferred_element_type=F32)
    h1 = _layer_norm(DEEPNORM_ALPHA * h + mix, g1_ref[...], b1_ref[...])
    h1_ref[...] = h1
    hi = h1.astype(BF16)
    lo = (h1 - hi.astype(F32)).astype(BF16)
    nt = (((1,), (1,)), ((), ()))
    t_hi = lax.dot_general(wr_ref[...], hi, nt, preferred_element_type=F32)
    t_lo = lax.dot_general(wr_ref[:ROUTER_PAD, :], lo, nt, preferred_element_type=F32)
    logits_t = (t_hi[:ROUTER_ROWS] + t_hi[ROUTER_PAD:ROUTER_PAD + ROUTER_ROWS]
                + t_lo[:ROUTER_ROWS] + br_ref[...])
    route_t = _router_gates_t(logits_t)
    pad = jnp.zeros((ROUTER_PAD - ROUTER_ROWS, route_t.shape[1]), F32)
    gate_ref[...] = jnp.concatenate([route_t, pad], axis=0).T


def _out_proj(x2, stats, attn, lru, ln_g, ln_b, ag, lg, wo, g1, b1, wr, br):
    T = x2.shape[0]
    tm = TM_OUT
    const = lambda i: (0, 0)
    row = lambda i: (i, 0)
    return pl.pallas_call(
        _out_proj_kernel,
        grid=(T // tm,),
        in_specs=[
            pl.BlockSpec((tm, D_MODEL), row),
            pl.BlockSpec((tm, LANES), row),
            pl.BlockSpec((tm, D_ATTN), row),
            pl.BlockSpec((tm, D_LRU), row),
            pl.BlockSpec((1, D_MODEL), const),
            pl.BlockSpec((1, D_MODEL), const),
            pl.BlockSpec((1, D_ATTN), const),
            pl.BlockSpec((1, D_LRU), const),
            pl.BlockSpec((D_MODEL, D_MODEL), const),
            pl.BlockSpec((1, D_MODEL), const),
            pl.BlockSpec((1, D_MODEL), const),
            pl.BlockSpec((2 * ROUTER_PAD, D_MODEL), const),
            pl.BlockSpec((ROUTER_ROWS, 1), const),
        ],
        out_specs=[pl.BlockSpec((tm, D_MODEL), row), pl.BlockSpec((tm, ROUTER_PAD), row)],
        out_shape=[jax.ShapeDtypeStruct((T, D_MODEL), F32),
                   jax.ShapeDtypeStruct((T, ROUTER_PAD), F32)],
        compiler_params=pltpu.CompilerParams(
            dimension_semantics=("parallel",), vmem_limit_bytes=VMEM_LIMIT),
        name="out_proj",
    )(x2, stats, attn, lru, ln_g, ln_b, ag, lg, wo, g1, b1, wr, br)


def _lane_scalar(row, idx):
    lane = lax.broadcasted_iota(jnp.int32, row.shape, 1)
    return jnp.sum(jnp.where(lane == idx, row, 0.0))


def _moe_kernel(h1_ref, route_ref, tril_ref, w13_ref, w2_ref, g2_ref, b2_ref, o_ref,
                pt_ref, xs_ref, gs_ref, ys_ref):
    for t in range(MOE_TILES_PER_STEP):
        rows = pl.ds(t * TM_MOE, TM_MOE)
        _moe_tile(h1_ref.at[rows, :], route_ref.at[rows, :], tril_ref, w13_ref, w2_ref, g2_ref, b2_ref,
                  o_ref.at[rows, :], pt_ref.at[t], xs_ref.at[t], gs_ref.at[t], ys_ref.at[t])


def _moe_tile(h1_ref, route_ref, tril_ref, w13_ref, w2_ref, g2_ref, b2_ref, o_ref,
              pt_ref, xs_ref, gs_ref, ys_ref):
    tm = h1_ref.shape[0]
    rs = xs_ref.shape[0]
    c = MOE_CHUNK
    h1 = h1_ref[...]
    route = route_ref[...]
    lane = lax.broadcasted_iota(jnp.int32, route.shape, 1)
    gid = jnp.sum(jnp.where(lane == 0, route, 0.0), axis=1, keepdims=True)
    onehot = jnp.where(lane.astype(F32) == gid, 1.0, 0.0)
    half = tm // 2
    ohb = onehot.astype(BF16)
    tril = tril_ref[...]
    cnt_top = jnp.sum(onehot[:half], axis=0, keepdims=True)
    before = jnp.concatenate(
        [jnp.dot(tril, ohb[:half], preferred_element_type=F32),
         jnp.dot(tril, ohb[half:], preferred_element_type=F32) + cnt_top], axis=0)
    cnt_row = cnt_top + jnp.sum(onehot[half:], axis=0, keepdims=True)
    cnt = [_lane_scalar(cnt_row, g).astype(jnp.int32) for g in range(N_GROUPS)]
    begin = [jnp.int32(0)]
    for g in range(1, N_GROUPS):
        end = begin[g - 1] + cnt[g - 1] + (MOE_ALIGN - 1)
        shift = jnp.int32(MOE_ALIGN_LOG2)
        begin.append(lax.shift_left(lax.shift_right_logical(end, shift), shift))
    lane1 = lax.broadcasted_iota(jnp.int32, cnt_row.shape, 1)
    begin_row = sum(jnp.where(lane1 == g, begin[g].astype(F32), 0.0) for g in range(N_GROUPS))
    dest = jnp.sum(onehot * (before + begin_row), axis=1, keepdims=True)
    col = lax.broadcasted_iota(jnp.int32, (tm, rs), 1).astype(F32)
    pt_ref[...] = jnp.where(col == dest, 1.0, 0.0).astype(BF16)
    dest_row = jnp.broadcast_to(dest, (tm, LANES)).T[0:1, :]
    row = lax.broadcasted_iota(jnp.int32, (rs, tm), 0).astype(F32)
    p = jnp.where(row == dest_row, 1.0, 0.0).astype(BF16)

    xs_ref[...] = jnp.dot(p, h1.astype(BF16), preferred_element_type=F32).astype(BF16)
    rhi = route.astype(BF16)
    rlo = (route - rhi.astype(F32)).astype(BF16)
    gs2 = jnp.dot(p, jnp.concatenate([rhi, rlo], axis=1), preferred_element_type=F32)
    gs_ref[...] = gs2[:, :LANES] + gs2[:, LANES:]
    ys_ref[...] = jnp.zeros_like(ys_ref)

    def run_chunk(g, first_row):
        rows = pl.ds(pl.multiple_of(first_row, MOE_ALIGN), c)
        xj = xs_ref[rows, :]
        gsj = gs_ref[rows, :]
        lane_c = lax.broadcasted_iota(jnp.int32, gsj.shape, 1)
        hids = []
        for k in range(EXPERTS_PER_GROUP):
            e = g * EXPERTS_PER_GROUP + k
            up = jnp.dot(xj, w13_ref[e], preferred_element_type=F32)
            u1 = up[:, :D_EXPERT]
            ge = jnp.sum(jnp.where(lane_c == EXPERT_LANE0 + e, gsj, 0.0), axis=1, keepdims=True)
            hids.append(((u1 * jax.nn.sigmoid(u1)) * up[:, D_EXPERT:] * ge).astype(BF16))
        ys_ref[rows, :] += jnp.dot(jnp.concatenate(hids, axis=1), w2_ref[g],
                                   preferred_element_type=F32)

    for g in range(N_GROUPS):
        run_chunk(g, begin[g])

    extra = [(g, k) for g in range(N_GROUPS) for k in range(1, N_GROUPS)]
    n_extra = sum((cnt[g] > k * c).astype(jnp.int32) for g, k in extra)

    def extra_chunk(s, carry):
        seen = jnp.int32(0)
        g_sel = jnp.int32(0)
        k_sel = jnp.int32(0)
        for g, k in extra:
            active = cnt[g] > k * c
            hit = active & (seen == s)
            g_sel = jnp.where(hit, g, g_sel)
            k_sel = jnp.where(hit, k, k_sel)
            seen = seen + active.astype(jnp.int32)
        first = sum(jnp.where(g_sel == g, begin[g], 0) for g in range(N_GROUPS)) + k_sel * c
        run_chunk(g_sel, first)
        return carry

    lax.fori_loop(0, n_extra, extra_chunk, 0)

    ffn = jnp.dot(pt_ref[...], ys_ref[...].astype(BF16), preferred_element_type=F32)
    o_ref[...] = _layer_norm(DEEPNORM_ALPHA * h1 + ffn, g2_ref[...], b2_ref[...])


def _moe(h1, route, tril, w13, w2, g2, b2):
    T = h1.shape[0]
    tm = TM_MOE
    nt = MOE_TILES_PER_STEP
    assert tm <= MOE_CHUNK * N_GROUPS and MOE_CHUNK % MOE_ALIGN == 0
    rs = -(-(tm + (N_GROUPS - 1) * (MOE_ALIGN - 1)) // MOE_ALIGN) * MOE_ALIGN + MOE_CHUNK
    gw = EXPERTS_PER_GROUP * D_EXPERT
    const2 = lambda i: (0, 0)
    const3 = lambda i: (0, 0, 0)
    resident = pl.Buffered(1)
    return pl.pallas_call(
        _moe_kernel,
        grid=(T // (nt * tm),),
        in_specs=[
            pl.BlockSpec((nt * tm, D_MODEL), lambda i: (i, 0)),
            pl.BlockSpec((nt * tm, ROUTER_PAD), lambda i: (i, 0)),
            pl.BlockSpec((tm // 2, tm // 2), const2, pipeline_mode=resident),
            pl.BlockSpec((N_EXPERTS, D_MODEL, 2 * D_EXPERT), const3, pipeline_mode=resident),
            pl.BlockSpec((N_GROUPS, gw, D_MODEL), const3, pipeline_mode=resident),
            pl.BlockSpec((1, D_MODEL), const2),
            pl.BlockSpec((1, D_MODEL), const2),
        ],
        out_specs=pl.BlockSpec((nt * tm, D_MODEL), lambda i: (i, 0)),
        out_shape=jax.ShapeDtypeStruct((T, D_MODEL), F32),
        scratch_shapes=[pltpu.VMEM((nt, tm, rs), BF16), pltpu.VMEM((nt, rs, D_MODEL), BF16),
                        pltpu.VMEM((nt, rs, ROUTER_PAD), F32), pltpu.VMEM((nt, rs, D_MODEL), F32)],
        compiler_params=pltpu.CompilerParams(
            dimension_semantics=("parallel",), vmem_limit_bytes=MOE_VMEM_LIMIT),
        name="moe",
    )(h1, route, tril, w13, w2, g2, b2)


def _rope_tables(S):
    t = jnp.arange(S, dtype=jnp.int32)
    row = (t // GRID_W).astype(F32)
    col = (t % GRID_W).astype(F32)
    half = ROPE_SECTION // 2
    inv_freq = ROPE_THETA ** (-jnp.arange(half, dtype=F32) / half)
    ang_r = row[:, None] * inv_freq
    ang_c = col[:, None] * inv_freq
    cos_h = jnp.concatenate([jnp.cos(ang_r)] * 2 + [jnp.cos(ang_c)] * 2, axis=1)
    sin_h = jnp.concatenate([-jnp.sin(ang_r), jnp.sin(ang_r), -jnp.sin(ang_c), jnp.sin(ang_c)], axis=1)
    reps = KV_WIDTH // HEAD_DIM
    return jnp.tile(cos_h, (1, reps)), jnp.tile(sin_h, (1, reps))


def _block_diag(w):
    H, d, _ = w.shape
    eye = jnp.eye(H, dtype=w.dtype)
    return (eye[:, None, :, None] * w[:, :, None, :]).reshape(H * d, H * d)


def _lru_gate_weights(wa, ba, wx, bx):
    C = LRU_CH
    ncg = D_LRU // C
    hpg = C // LRU_HEAD_DIM
    ws, bs = [], []
    for c in range(ncg):
        hs = slice(c * hpg, (c + 1) * hpg)
        cols, bias = [], []
        for d in range(2):
            cols += [_block_diag(wa[d, hs]), _block_diag(wx[d, hs])]
            bias += [ba[d, hs].reshape(C), bx[d, hs].reshape(C)]
        ws.append(jnp.concatenate(cols, axis=1))
        bs.append(jnp.concatenate(bias)[None, :])
    return (-LOG2_E * jnp.stack(ws)).astype(BF16), -LOG2_E * jnp.stack(bs)


def _trunk(x, P):
    B, S, _ = x.shape
    x2 = x.reshape(B * S, D_MODEL)
    q, k, v, xl, yl, stats = _in_proj(x2, B, S, P["ln_in_g"], P["ln_in_b"], P["wq"], P["wkv"], P["wx"],
                               P["wy"], P["seg"], P["qg"], P["kg"], P["cos"], P["sin"])
    attn = _attention(q, k, v, B, S)
    lru = _rg_lru(xl, yl, B, S, P["conv_w"], P["conv_b"], P["wg"], P["bg"], P["lam"])
    h1, route = _out_proj(x2, stats, attn, lru, P["ln_in_g"], P["ln_in_b"], P["ag"], P["lg"], P["wo"],
                          P["g1"], P["b1"], P["wr"], P["br"])
    out = _moe(h1, route, P["tril"], P["w13"], P["w2"], P["g2"], P["b2"])
    return out.reshape(B, S, D_MODEL)


def kernel(x_prompt, x_sample, ln_in_g, ln_in_b, w_in, conv_w, conv_b, lru_wa, lru_ba, lru_wx, lru_bx, lru_lambda, q_norm_g, k_norm_g, attn_out_g, lru_out_g, w_out, ln1_g, ln1_b, router_wg, router_bg, router_we, router_be, exp_w1, exp_w3, exp_w2, ln2_g, ln2_b):
    assert w_in.shape[0] == DEPTH == 1
    S = x_prompt.shape[1]
    l = 0
    w = w_in[l]
    c0, c1, c2 = D_ATTN, D_ATTN + 2 * KV_WIDTH, D_ATTN + 2 * KV_WIDTH + D_LRU
    seg = _block_diag(jnp.full((N_Q_HEADS, HEAD_DIM, HEAD_DIM), 1.0 / HEAD_DIM, F32)).astype(BF16)
    cos_t, sin_t = _rope_tables(S)
    wg, bg = _lru_gate_weights(lru_wa[l], lru_ba[l], lru_wx[l], lru_bx[l])
    wr = jnp.concatenate([router_wg[l], router_we[l].reshape(D_MODEL, N_EXPERTS)], axis=1)
    wr = jnp.pad(wr, ((0, 0), (0, ROUTER_PAD - wr.shape[1])))
    wrh = wr.astype(BF16)
    wrl = (wr - wrh.astype(F32)).astype(BF16)
    wr_t = jnp.concatenate([wrh.T, wrl.T], axis=0)
    br = jnp.concatenate([router_bg[l], router_be[l].reshape(N_EXPERTS)])
    br = jnp.pad(br, (0, ROUTER_ROWS - br.shape[0]))[:, None]
    P = dict(
        ln_in_g=ln_in_g[None, :], ln_in_b=ln_in_b[None, :],
        wq=w[:, :c0].astype(BF16), wkv=w[:, c0:c1].astype(BF16),
        wx=w[:, c1:c2].astype(BF16), wy=w[:, c2:].astype(BF16),
        seg=seg, qg=jnp.tile(q_norm_g[l], N_Q_HEADS)[None, :],
        kg=jnp.tile(k_norm_g[l], N_KV_HEADS)[None, :], cos=cos_t, sin=sin_t,
        conv_w=conv_w[l], conv_b=conv_b[l][None, :], wg=wg, bg=bg, lam=lru_lambda[l],
        ag=attn_out_g[l][None, :], lg=lru_out_g[l][None, :], wo=w_out[l].astype(BF16),
        g1=ln1_g[l][None, :], b1=ln1_b[l][None, :], wr=wr_t, br=br,
        w13=jnp.concatenate([exp_w1[l], exp_w3[l]], axis=2).astype(BF16),
        w2=exp_w2[l].reshape(N_GROUPS, EXPERTS_PER_GROUP * D_EXPERT, D_MODEL).astype(BF16),
        g2=ln2_g[l][None, :], b2=ln2_b[l][None, :],
        tril=jnp.tril(jnp.ones((TM_MOE // 2, TM_MOE // 2), F32), -1).astype(BF16),
    )
    return (_trunk(x_prompt, P), _trunk(x_sample, P))
```

```python
import functools
import math

import jax
import jax.numpy as jnp
from jax import lax
from jax.experimental import pallas as pl
from jax.experimental.pallas import tpu as pltpu

F32 = jnp.float32
BF16 = jnp.bfloat16

D_MODEL = 1024
GRID_W = 64
D_ATTN = 512
HEAD_DIM = 64
N_Q_HEADS = 8
N_KV_HEADS = 2
Q_PER_KV = 4
KV_WIDTH = 128
ROPE_THETA = 10000.0
ROPE_SECTION = 32
QK_EPS = 1e-6
D_LRU = 512
N_LRU_HEADS = 8
LRU_HEAD_DIM = 64
CONV_WIDTH = 4
LRU_C = 8.0
N_GROUPS = 4
EXPERTS_PER_GROUP = 4
N_EXPERTS = 16
D_EXPERT = 256
LN_EPS = 1e-5
DEPTH = 1
DEEPNORM_ALPHA = (2.0 * DEPTH) ** 0.25
LOG2_E = math.log2(math.e)

LANES = 128
SUBLANES = 8
VMEM_LIMIT = 48 * 1024 * 1024

TM_IN = 1024
IN_PROJ_SPLIT = 4
TM_OUT = 1024
TQ = 1024
KEY_BLOCK = 1024
LRU_CH = 256
SEG = 256
LRU_SCAN_UNROLL = 4
LRU_COMBINE_ROWS = 128
TM_MOE = 512
MOE_CHUNK = 144
MOE_ALIGN_LOG2 = 4
MOE_ALIGN = 1 << MOE_ALIGN_LOG2
MOE_TILES_PER_STEP = 1
MOE_VMEM_LIMIT = 56 * 1024 * 1024
ROUTER_PAD = LANES
EXPERT_LANE0 = N_GROUPS


def _layer_norm(x, g, b):
    mu = jnp.mean(x, axis=-1, keepdims=True)
    xc = x - mu
    var = jnp.mean(xc * xc, axis=-1, keepdims=True)
    return xc * lax.rsqrt(var + LN_EPS) * g + b


def _rms_norm_rows(x, g, eps):
    return x * lax.rsqrt(jnp.mean(x * x, axis=-1, keepdims=True) + eps) * g


def _head_norm_rope(xf, seg, g, cos, sin):
    ms = jnp.dot((xf * xf).astype(BF16), seg, preferred_element_type=F32)
    xn = xf * lax.rsqrt(ms + QK_EPS) * g
    n = xn.shape[1]
    half = ROPE_SECTION // 2
    up = pltpu.roll(xn, n - half, axis=1)
    dn = pltpu.roll(xn, half, axis=1)
    lane = lax.broadcasted_iota(jnp.int32, xn.shape, 1)
    first = (lane % ROPE_SECTION) < half
    partner = jnp.where(first, up, dn)
    return xn * cos + partner * sin


def _in_proj_kernel(x_ref, g_ref, b_ref, wq_ref, wkv_ref, wx_ref, wy_ref, seg_ref,
                    qg_ref, kg_ref, cos_ref, sin_ref,
                    q_ref, k_ref, v_ref, xl_ref, yl_ref, st_ref):
    reps = D_ATTN // KV_WIDTH
    blk = x_ref.shape[0] // IN_PROJ_SPLIT
    for i in range(IN_PROJ_SPLIT):
        rows = slice(i * blk, (i + 1) * blk)
        x = x_ref[rows, :]
        mu = jnp.mean(x, axis=-1, keepdims=True)
        xc = x - mu
        rstd = lax.rsqrt(jnp.mean(xc * xc, axis=-1, keepdims=True) + LN_EPS)
        hb = (xc * rstd * g_ref[...] + b_ref[...]).astype(BF16)
        lane = lax.broadcasted_iota(jnp.int32, (blk, LANES), 1)
        st_ref[rows, :] = jnp.where(lane == 0, mu, jnp.where(lane == 1, rstd, 0.0))
        cos = cos_ref[rows, :]
        sin = sin_ref[rows, :]

        qf = jnp.dot(hb, wq_ref[...], preferred_element_type=F32)
        q = _head_norm_rope(qf, seg_ref[...], qg_ref[...],
                            jnp.concatenate([cos] * reps, axis=1),
                            jnp.concatenate([sin] * reps, axis=1))
        q_ref[rows, :] = (q * (HEAD_DIM ** -0.5 * LOG2_E)).astype(BF16)

        kvf = jnp.dot(hb, wkv_ref[...], preferred_element_type=F32)
        k = _head_norm_rope(kvf[:, :KV_WIDTH], seg_ref[:KV_WIDTH, :KV_WIDTH], kg_ref[...], cos, sin)
        kb = k.astype(BF16)
        vb = kvf[:, KV_WIDTH:].astype(BF16)
        ones = jnp.ones((blk, HEAD_DIM), BF16)
        for j in range(N_KV_HEADS):
            k_ref[j, rows, :] = kb[:, j * HEAD_DIM:(j + 1) * HEAD_DIM]
            v_ref[j, rows, :] = jnp.concatenate([vb[:, j * HEAD_DIM:(j + 1) * HEAD_DIM], ones], axis=1)

        xl_ref[rows, :] = jnp.dot(hb, wx_ref[...], preferred_element_type=F32)
        yl_ref[rows, :] = jnp.dot(hb, wy_ref[...], preferred_element_type=F32)


def _in_proj(x2, B, S, ln_g, ln_b, wq, wkv, wx, wy, seg, qg, kg, cos_t, sin_t):
    T = B * S
    tm = TM_IN
    nt = S // tm
    const = lambda i: (0, 0)
    return pl.pallas_call(
        _in_proj_kernel,
        grid=(T // tm,),
        in_specs=[
            pl.BlockSpec((tm, D_MODEL), lambda i: (i, 0)),
            pl.BlockSpec((1, D_MODEL), const),
            pl.BlockSpec((1, D_MODEL), const),
            pl.BlockSpec((D_MODEL, D_ATTN), const),
            pl.BlockSpec((D_MODEL, 2 * KV_WIDTH), const),
            pl.BlockSpec((D_MODEL, D_LRU), const),
            pl.BlockSpec((D_MODEL, D_LRU), const),
            pl.BlockSpec((D_ATTN, D_ATTN), const),
            pl.BlockSpec((1, D_ATTN), const),
            pl.BlockSpec((1, KV_WIDTH), const),
            pl.BlockSpec((tm, KV_WIDTH), lambda i: (i % nt, 0)),
            pl.BlockSpec((tm, KV_WIDTH), lambda i: (i % nt, 0)),
        ],
        out_specs=[
            pl.BlockSpec((tm, D_ATTN), lambda i: (i, 0)),
            pl.BlockSpec((None, N_KV_HEADS, tm, HEAD_DIM), lambda i: (i // nt, 0, i % nt, 0)),
            pl.BlockSpec((None, N_KV_HEADS, tm, 2 * HEAD_DIM), lambda i: (i // nt, 0, i % nt, 0)),
            pl.BlockSpec((tm, D_LRU), lambda i: (i, 0)),
            pl.BlockSpec((tm, D_LRU), lambda i: (i, 0)),
            pl.BlockSpec((tm, LANES), lambda i: (i, 0)),
        ],
        out_shape=[
            jax.ShapeDtypeStruct((T, D_ATTN), BF16),
            jax.ShapeDtypeStruct((B, N_KV_HEADS, S, HEAD_DIM), BF16),
            jax.ShapeDtypeStruct((B, N_KV_HEADS, S, 2 * HEAD_DIM), BF16),
            jax.ShapeDtypeStruct((T, D_LRU), F32),
            jax.ShapeDtypeStruct((T, D_LRU), F32),
            jax.ShapeDtypeStruct((T, LANES), F32),
        ],
        compiler_params=pltpu.CompilerParams(
            dimension_semantics=("parallel",), vmem_limit_bytes=VMEM_LIMIT),
        name="in_proj",
    )(x2, ln_g, ln_b, wq, wkv, wx, wy, seg, qg, kg, cos_t, sin_t)


def _attn_kernel(q_ref, k_ref, v_ref, o_ref):
    nblk = k_ref.shape[0] // KEY_BLOCK
    outs = []
    for h in range(Q_PER_KV):
        qh = q_ref[:, h * HEAD_DIM:(h + 1) * HEAD_DIM]
        m = acc = None
        for b in range(nblk):
            kb = k_ref[b * KEY_BLOCK:(b + 1) * KEY_BLOCK, :]
            vb = v_ref[b * KEY_BLOCK:(b + 1) * KEY_BLOCK, :]
            s = lax.dot_general(qh, kb, (((1,), (1,)), ((), ())), preferred_element_type=F32)
            mb = jnp.max(s, axis=1, keepdims=True)
            if b == 0:
                m = mb
                acc = jnp.dot(jnp.exp2(s - m).astype(BF16), vb, preferred_element_type=F32)
            else:
                m_new = jnp.maximum(m, mb)
                acc = (jnp.exp2(m - m_new) * acc
                       + jnp.dot(jnp.exp2(s - m_new).astype(BF16), vb, preferred_element_type=F32))
                m = m_new
        outs.append(acc[:, :HEAD_DIM] / acc[:, HEAD_DIM:HEAD_DIM + 1])
    o_ref[...] = jnp.concatenate(outs, axis=1)


def _attention(q, k, v, B, S):
    T = B * S
    nq = S // TQ
    width = Q_PER_KV * HEAD_DIM
    return pl.pallas_call(
        _attn_kernel,
        grid=(B, N_KV_HEADS, nq),
        in_specs=[
            pl.BlockSpec((TQ, width), lambda b, j, t: (b * nq + t, j)),
            pl.BlockSpec((None, None, S, HEAD_DIM), lambda b, j, t: (b, j, 0, 0)),
            pl.BlockSpec((None, None, S, 2 * HEAD_DIM), lambda b, j, t: (b, j, 0, 0)),
        ],
        out_specs=pl.BlockSpec((TQ, width), lambda b, j, t: (b * nq + t, j)),
        out_shape=jax.ShapeDtypeStruct((T, D_ATTN), F32),
        compiler_params=pltpu.CompilerParams(
            dimension_semantics=("parallel", "parallel", "parallel"),
            vmem_limit_bytes=VMEM_LIMIT),
        name="attention",
    )(q, k, v)


def _gelu_tanh(x):
    c = math.sqrt(2.0 / math.pi)
    return 0.5 * x * (1.0 + jnp.tanh(c * (x + 0.044715 * (x * x * x))))


def _rg_lru_kernel(x_ref, y_ref, cw_ref, cb_ref, wg_ref, bg_ref, lam_ref, o_ref,
                   af_ref, uf_ref, ab_ref, ub_ref, hf_ref, pf_ref, hb_ref, pb_ref, cf_ref, cbk_ref):
    S, C = x_ref.shape
    nslab = C // LANES
    nseg = S // SEG
    assert nseg == SUBLANES
    lam = lam_ref[...]
    log2_a_per_r = (-LRU_C * LOG2_E) * jnp.logaddexp(-lam, 0.0)
    cw = cw_ref[...]
    cb = cb_ref[...]
    bg = bg_ref[...]

    def gate_block(s, carry):
        start = pl.multiple_of(s * SEG, SEG)
        seg_rows = pl.ds(s, SEG, stride=SUBLANES)
        cur = x_ref[pl.ds(start, SEG), :]
        prev = x_ref[pl.ds(pl.multiple_of(jnp.maximum(start - SUBLANES, 0), SUBLANES), SUBLANES), :]
        prev = jnp.where(s > 0, prev, 0.0)
        nxt = x_ref[pl.ds(pl.multiple_of(jnp.minimum(start + SEG, S - SUBLANES), SUBLANES), SUBLANES), :]
        nxt = jnp.where(s < nseg - 1, nxt, 0.0)
        ext = jnp.concatenate([prev, cur, nxt], axis=0)
        xc = cb
        for tap in range(CONV_WIDTH):
            shift = CONV_WIDTH // 2 - tap
            src = ext if shift == 0 else pltpu.roll(ext, shift % ext.shape[0], axis=0)
            xc = xc + src[SUBLANES:SUBLANES + SEG, :] * cw[tap:tap + 1, :]
        z = jnp.dot(xc.astype(BF16), wg_ref[...], preferred_element_type=F32) + bg
        for d, (a_ref, u_ref) in enumerate(((af_ref, uf_ref), (ab_ref, ub_ref))):
            rg = 1.0 / (1.0 + jnp.exp2(z[:, (2 * d) * C:(2 * d + 1) * C]))
            ig = 1.0 / (1.0 + jnp.exp2(z[:, (2 * d + 1) * C:(2 * d + 2) * C]))
            a = jnp.exp2(log2_a_per_r[d:d + 1, :] * rg)
            s1 = 1.0 - a * a
            u = jnp.where(s1 > 0.0, s1 * lax.rsqrt(s1), 0.0) * (ig * xc)
            for slab in range(nslab):
                lanes = slice(slab * LANES, (slab + 1) * LANES)
                a_ref[slab, seg_rows, :] = a[:, lanes]
                u_ref[slab, seg_rows, :] = u[:, lanes]
        return carry

    lax.fori_loop(0, nseg, gate_block, 0, unroll=2)

    def two_steps(a_ref, u_ref, h_ref, p_ref, slab, j0, j1, h, p):
        r0 = pl.ds(pl.multiple_of(j0 * SUBLANES, SUBLANES), SUBLANES)
        r1 = pl.ds(pl.multiple_of(j1 * SUBLANES, SUBLANES), SUBLANES)
        a0, u0 = a_ref[slab, r0, :], u_ref[slab, r0, :]
        a1, u1 = a_ref[slab, r1, :], u_ref[slab, r1, :]
        a01 = a1 * a0
        u01 = a1 * u0 + u1
        h_ref[slab, r0, :] = a0 * h + u0
        p_ref[slab, r0, :] = a0 * p
        h = a01 * h + u01
        p = a01 * p
        h_ref[slab, r1, :] = h
        p_ref[slab, r1, :] = p
        return h, p

    def scan_pair(i, carry):
        out = []
        for slab in range(nslab):
            hf, pf, hb, pb = carry[slab]
            hf, pf = two_steps(af_ref, uf_ref, hf_ref, pf_ref, slab, 2 * i, 2 * i + 1, hf, pf)
            hb, pb = two_steps(ab_ref, ub_ref, hb_ref, pb_ref, slab,
                               SEG - 1 - 2 * i, SEG - 2 - 2 * i, hb, pb)
            out.append((hf, pf, hb, pb))
        return tuple(out)

    zero = jnp.zeros((SUBLANES, LANES), F32)
    one = jnp.ones((SUBLANES, LANES), F32)
    final = lax.fori_loop(0, SEG // 2, scan_pair, tuple((zero, one, zero, one) for _ in range(nslab)),
                          unroll=LRU_SCAN_UNROLL)

    for slab in range(nslab):
        hf, pf, hb, pb = final[slab]
        c = jnp.zeros((1, LANES), F32)
        fwd = [c]
        for sgm in range(1, nseg):
            c = hf[sgm - 1:sgm, :] + pf[sgm - 1:sgm, :] * c
            fwd.append(c)
        c = jnp.zeros((1, LANES), F32)
        bwd = [c]
        for sgm in range(nseg - 2, -1, -1):
            c = hb[sgm + 1:sgm + 2, :] + pb[sgm + 1:sgm + 2, :] * c
            bwd.append(c)
        cf_ref[slab] = jnp.concatenate(fwd, axis=0)
        cbk_ref[slab] = jnp.concatenate(bwd[::-1], axis=0)

    reps = LRU_COMBINE_ROWS // SUBLANES

    def combine(i, carry):
        rows = pl.ds(pl.multiple_of(i * LRU_COMBINE_ROWS, LRU_COMBINE_ROWS), LRU_COMBINE_ROWS)
        for slab in range(nslab):
            cf = jnp.concatenate([cf_ref[slab]] * reps, axis=0)
            cbk = jnp.concatenate([cbk_ref[slab]] * reps, axis=0)
            uf_ref[slab, rows, :] = ((hf_ref[slab, rows, :] + pf_ref[slab, rows, :] * cf)
                                     + (hb_ref[slab, rows, :] + pb_ref[slab, rows, :] * cbk))
        return carry

    lax.fori_loop(0, S // LRU_COMBINE_ROWS, combine, 0)

    def out_block(s, carry):
        start = pl.multiple_of(s * SEG, SEG)
        for slab in range(nslab):
            lanes = slice(slab * LANES, (slab + 1) * LANES)
            h = uf_ref[slab, pl.ds(s, SEG, stride=SUBLANES), :]
            o_ref[pl.ds(start, SEG), lanes] = h * _gelu_tanh(y_ref[pl.ds(start, SEG), lanes])
        return carry

    lax.fori_loop(0, nseg, out_block, 0)


def _rg_lru(xl, yl, B, S, conv_w, conv_b, wg, bg, lam):
    C = LRU_CH
    ncg = D_LRU // C
    nslab = C // LANES
    xl3 = xl.reshape(B, S, D_LRU)
    yl3 = yl.reshape(B, S, D_LRU)
    seg_buf = pltpu.VMEM((nslab, S, LANES), F32)
    carry_buf = pltpu.VMEM((nslab, S // SEG, LANES), F32)
    out = pl.pallas_call(
        _rg_lru_kernel,
        grid=(B, ncg),
        in_specs=[
            pl.BlockSpec((None, S, C), lambda b, c: (b, 0, c)),
            pl.BlockSpec((None, S, C), lambda b, c: (b, 0, c)),
            pl.BlockSpec((CONV_WIDTH, C), lambda b, c: (0, c)),
            pl.BlockSpec((1, C), lambda b, c: (0, c)),
            pl.BlockSpec((None, C, 4 * C), lambda b, c: (c, 0, 0)),
            pl.BlockSpec((None, 1, 4 * C), lambda b, c: (c, 0, 0)),
            pl.BlockSpec((2, C), lambda b, c: (0, c)),
        ],
        out_specs=pl.BlockSpec((None, S, C), lambda b, c: (b, 0, c)),
        out_shape=jax.ShapeDtypeStruct((B, S, D_LRU), F32),
        scratch_shapes=[seg_buf] * 8 + [carry_buf] * 2,
        compiler_params=pltpu.CompilerParams(
            dimension_semantics=("parallel", "parallel"), vmem_limit_bytes=VMEM_LIMIT),
        name="rg_lru",
    )(xl3, yl3, conv_w, conv_b, wg, bg, lam)
    return out.reshape(B * S, D_LRU)


ROUTER_ROWS = 24


def _router_gates_t(logits):
    row = lax.broadcasted_iota(jnp.int32, logits.shape, 0).astype(F32)
    neg = -jnp.inf
    big = float(2 * LANES)
    gl = jnp.where(row < N_GROUPS, logits, neg)
    gmax = jnp.max(gl, axis=0, keepdims=True)
    gidx = jnp.min(jnp.where(gl == gmax, row, big), axis=0, keepdims=True)
    g_sel_prob = 1.0 / jnp.sum(jnp.exp(gl - gmax), axis=0, keepdims=True)
    lo = EXPERT_LANE0 + EXPERTS_PER_GROUP * gidx
    el = jnp.where((row >= lo) & (row < lo + EXPERTS_PER_GROUP), logits, neg)
    v1 = jnp.max(el, axis=0, keepdims=True)
    i1 = jnp.min(jnp.where(el == v1, row, big), axis=0, keepdims=True)
    el2 = jnp.where(row == i1, neg, el)
    v2 = jnp.max(el2, axis=0, keepdims=True)
    i2 = jnp.min(jnp.where(el2 == v2, row, big), axis=0, keepdims=True)
    e2 = jnp.exp(v2 - v1)
    w1 = g_sel_prob / (1.0 + e2)
    w2 = g_sel_prob * e2 / (1.0 + e2)
    gates = jnp.where(row == i1, w1, jnp.where(row == i2, w2, 0.0))
    return jnp.where(row == 0.0, gidx, gates)


def _out_proj_kernel(x_ref, st_ref, attn_ref, lru_ref, lng_ref, lnb_ref, ag_ref, lg_ref, wo_ref,
                     g1_ref, b1_ref, wr_ref, br_ref, h1_ref, gate_ref):
    h = (x_ref[...] - st_ref[:, 0:1]) * st_ref[:, 1:2] * lng_ref[...] + lnb_ref[...]
    merged = jnp.concatenate(
        [_rms_norm_rows(attn_ref[...], ag_ref[...], LN_EPS),
         _rms_norm_rows(lru_ref[...], lg_ref[...], LN_EPS)], axis=1).astype(BF16)
    mix = jnp.dot(merged, wo_ref[...], preferred_element_type=F32)
    h1 = _layer_norm(DEEPNORM_ALPHA * h + mix, g1_ref[...], b1_ref[...])
    h1_ref[...] = h1
    hi = h1.astype(BF16)
    lo = (h1 - hi.astype(F32)).astype(BF16)
    nt = (((1,), (1,)), ((), ()))
    t_hi = lax.dot_general(wr_ref[...], hi, nt, preferred_element_type=F32)
    t_lo = lax.dot_general(wr_ref[:ROUTER_PAD, :], lo, nt, preferred_element_type=F32)
    logits_t = (t_hi[:ROUTER_ROWS] + t_hi[ROUTER_PAD:ROUTER_PAD + ROUTER_ROWS]
                + t_lo[:ROUTER_ROWS] + br_ref[...])
    route_t = _router_gates_t(logits_t)
    pad = jnp.zeros((ROUTER_PAD - ROUTER_ROWS, route_t.shape[1]), F32)
    gate_ref[...] = jnp.concatenate([route_t, pad], axis=0).T


def _out_proj(x2, stats, attn, lru, ln_g, ln_b, ag, lg, wo, g1, b1, wr, br):
    T = x2.shape[0]
    tm = TM_OUT
    const = lambda i: (0, 0)
    row = lambda i: (i, 0)
    return pl.pallas_call(
        _out_proj_kernel,
        grid=(T // tm,),
        in_specs=[
            pl.BlockSpec((tm, D_MODEL), row),
            pl.BlockSpec((tm, LANES), row),
            pl.BlockSpec((tm, D_ATTN), row),
            pl.BlockSpec((tm, D_LRU), row),
            pl.BlockSpec((1, D_MODEL), const),
            pl.BlockSpec((1, D_MODEL), const),
            pl.BlockSpec((1, D_ATTN), const),
            pl.BlockSpec((1, D_LRU), const),
            pl.BlockSpec((D_MODEL, D_MODEL), const),
            pl.BlockSpec((1, D_MODEL), const),
            pl.BlockSpec((1, D_MODEL), const),
            pl.BlockSpec((2 * ROUTER_PAD, D_MODEL), const),
            pl.BlockSpec((ROUTER_ROWS, 1), const),
        ],
        out_specs=[pl.BlockSpec((tm, D_MODEL), row), pl.BlockSpec((tm, ROUTER_PAD), row)],
        out_shape=[jax.ShapeDtypeStruct((T, D_MODEL), F32),
                   jax.ShapeDtypeStruct((T, ROUTER_PAD), F32)],
        compiler_params=pltpu.CompilerParams(
            dimension_semantics=("parallel",), vmem_limit_bytes=VMEM_LIMIT),
        name="out_proj",
    )(x2, stats, attn, lru, ln_g, ln_b, ag, lg, wo, g1, b1, wr, br)


def _lane_scalar(row, idx):
    lane = lax.broadcasted_iota(jnp.int32, row.shape, 1)
    return jnp.sum(jnp.where(lane == idx, row, 0.0))


def _moe_kernel(h1_ref, route_ref, tril_ref, w13_ref, w2_ref, g2_ref, b2_ref, o_ref,
                pt_ref, xs_ref, gs_ref, ys_ref):
    for t in range(MOE_TILES_PER_STEP):
        rows = pl.ds(t * TM_MOE, TM_MOE)
        _moe_tile(h1_ref.at[rows, :], route_ref.at[rows, :], tril_ref, w13_ref, w2_ref, g2_ref, b2_ref,
                  o_ref.at[rows, :], pt_ref.at[t], xs_ref.at[t], gs_ref.at[t], ys_ref.at[t])


def _moe_tile(h1_ref, route_ref, tril_ref, w13_ref, w2_ref, g2_ref, b2_ref, o_ref,
              pt_ref, xs_ref, gs_ref, ys_ref):
    tm = h1_ref.shape[0]
    rs = xs_ref.shape[0]
    c = MOE_CHUNK
    h1 = h1_ref[...]
    route = route_ref[...]
    lane = lax.broadcasted_iota(jnp.int32, route.shape, 1)
    gid = jnp.sum(jnp.where(lane == 0, route, 0.0), axis=1, keepdims=True)
    onehot = jnp.where(lane.astype(F32) == gid, 1.0, 0.0)
    half = tm // 2
    ohb = onehot.astype(BF16)
    tril = tril_ref[...]
    cnt_top = jnp.sum(onehot[:half], axis=0, keepdims=True)
    before = jnp.concatenate(
        [jnp.dot(tril, ohb[:half], preferred_element_type=F32),
         jnp.dot(tril, ohb[half:], preferred_element_type=F32) + cnt_top], axis=0)
    cnt_row = cnt_top + jnp.sum(onehot[half:], axis=0, keepdims=True)
    cnt = [_lane_scalar(cnt_row, g).astype(jnp.int32) for g in range(N_GROUPS)]
    begin = [jnp.int32(0)]
    for g in range(1, N_GROUPS):
        end = begin[g - 1] + cnt[g - 1] + (MOE_ALIGN - 1)
        shift = jnp.int32(MOE_ALIGN_LOG2)
        begin.append(lax.shift_left(lax.shift_right_logical(end, shift), shift))
    lane1 = lax.broadcasted_iota(jnp.int32, cnt_row.shape, 1)
    begin_row = sum(jnp.where(lane1 == g, begin[g].astype(F32), 0.0) for g in range(N_GROUPS))
    dest = jnp.sum(onehot * (before + begin_row), axis=1, keepdims=True)
    col = lax.broadcasted_iota(jnp.int32, (tm, rs), 1).astype(F32)
    pt_ref[...] = jnp.where(col == dest, 1.0, 0.0).astype(BF16)
    dest_row = jnp.broadcast_to(dest, (tm, LANES)).T[0:1, :]
    row = lax.broadcasted_iota(jnp.int32, (rs, tm), 0).astype(F32)
    p = jnp.where(row == dest_row, 1.0, 0.0).astype(BF16)

    xs_ref[...] = jnp.dot(p, h1.astype(BF16), preferred_element_type=F32).astype(BF16)
    rhi = route.astype(BF16)
    rlo = (route - rhi.astype(F32)).astype(BF16)
    gs2 = jnp.dot(p, jnp.concatenate([rhi, rlo], axis=1), preferred_element_type=F32)
    gs_ref[...] = gs2[:, :LANES] + gs2[:, LANES:]
    ys_ref[...] = jnp.zeros_like(ys_ref)

    def run_chunk(g, first_row):
        rows = pl.ds(pl.multiple_of(first_row, MOE_ALIGN), c)
        xj = xs_ref[rows, :]
        gsj = gs_ref[rows, :]
        lane_c = lax.broadcasted_iota(jnp.int32, gsj.shape, 1)
        hids = []
        for k in range(EXPERTS_PER_GROUP):
            e = g * EXPERTS_PER_GROUP + k
            up = jnp.dot(xj, w13_ref[e], preferred_element_type=F32)
            u1 = up[:, :D_EXPERT]
            ge = jnp.sum(jnp.where(lane_c == EXPERT_LANE0 + e, gsj, 0.0), axis=1, keepdims=True)
            hids.append(((u1 * jax.nn.sigmoid(u1)) * up[:, D_EXPERT:] * ge).astype(BF16))
        ys_ref[rows, :] += jnp.dot(jnp.concatenate(hids, axis=1), w2_ref[g],
                                   preferred_element_type=F32)

    for g in range(N_GROUPS):
        run_chunk(g, begin[g])

    extra = [(g, k) for g in range(N_GROUPS) for k in range(1, N_GROUPS)]
    n_extra = sum((cnt[g] > k * c).astype(jnp.int32) for g, k in extra)

    def extra_chunk(s, carry):
        seen = jnp.int32(0)
        g_sel = jnp.int32(0)
        k_sel = jnp.int32(0)
        for g, k in extra:
            active = cnt[g] > k * c
            hit = active & (seen == s)
            g_sel = jnp.where(hit, g, g_sel)
            k_sel = jnp.where(hit, k, k_sel)
            seen = seen + active.astype(jnp.int32)
        first = sum(jnp.where(g_sel == g, begin[g], 0) for g in range(N_GROUPS)) + k_sel * c
        run_chunk(g_sel, first)
        return carry

    def finish():
        ffn = jnp.dot(pt_ref[...], ys_ref[...].astype(BF16), preferred_element_type=F32)
        o_ref[...] = _layer_norm(DEEPNORM_ALPHA * h1_ref[...] + ffn, g2_ref[...], b2_ref[...])

    finish()

    @pl.when(n_extra > 0)
    def _():
        lax.fori_loop(0, n_extra, extra_chunk, 0)
        finish()


def _moe(h1, route, tril, w13, w2, g2, b2):
    T = h1.shape[0]
    tm = TM_MOE
    nt = MOE_TILES_PER_STEP
    assert tm <= MOE_CHUNK * N_GROUPS and MOE_CHUNK % MOE_ALIGN == 0
    rs = -(-(tm + (N_GROUPS - 1) * (MOE_ALIGN - 1)) // MOE_ALIGN) * MOE_ALIGN + MOE_CHUNK
    gw = EXPERTS_PER_GROUP * D_EXPERT
    const2 = lambda i: (0, 0)
    const3 = lambda i: (0, 0, 0)
    resident = pl.Buffered(1)
    return pl.pallas_call(
        _moe_kernel,
        grid=(T // (nt * tm),),
        in_specs=[
            pl.BlockSpec((nt * tm, D_MODEL), lambda i: (i, 0)),
            pl.BlockSpec((nt * tm, ROUTER_PAD), lambda i: (i, 0)),
            pl.BlockSpec((tm // 2, tm // 2), const2, pipeline_mode=resident),
            pl.BlockSpec((N_EXPERTS, D_MODEL, 2 * D_EXPERT), const3, pipeline_mode=resident),
            pl.BlockSpec((N_GROUPS, gw, D_MODEL), const3, pipeline_mode=resident),
            pl.BlockSpec((1, D_MODEL), const2),
            pl.BlockSpec((1, D_MODEL), const2),
        ],
        out_specs=pl.BlockSpec((nt * tm, D_MODEL), lambda i: (i, 0)),
        out_shape=jax.ShapeDtypeStruct((T, D_MODEL), F32),
        scratch_shapes=[pltpu.VMEM((nt, tm, rs), BF16), pltpu.VMEM((nt, rs, D_MODEL), BF16),
                        pltpu.VMEM((nt, rs, ROUTER_PAD), F32), pltpu.VMEM((nt, rs, D_MODEL), F32)],
        compiler_params=pltpu.CompilerParams(
            dimension_semantics=("parallel",), vmem_limit_bytes=MOE_VMEM_LIMIT),
        name="moe",
    )(h1, route, tril, w13, w2, g2, b2)


def _rope_tables(S):
    t = jnp.arange(S, dtype=jnp.int32)
    row = (t // GRID_W).astype(F32)
    col = (t % GRID_W).astype(F32)
    half = ROPE_SECTION // 2
    inv_freq = ROPE_THETA ** (-jnp.arange(half, dtype=F32) / half)
    ang_r = row[:, None] * inv_freq
    ang_c = col[:, None] * inv_freq
    cos_h = jnp.concatenate([jnp.cos(ang_r)] * 2 + [jnp.cos(ang_c)] * 2, axis=1)
    sin_h = jnp.concatenate([-jnp.sin(ang_r), jnp.sin(ang_r), -jnp.sin(ang_c), jnp.sin(ang_c)], axis=1)
    reps = KV_WIDTH // HEAD_DIM
    return jnp.tile(cos_h, (1, reps)), jnp.tile(sin_h, (1, reps))


def _block_diag(w):
    H, d, _ = w.shape
    eye = jnp.eye(H, dtype=w.dtype)
    return (eye[:, None, :, None] * w[:, :, None, :]).reshape(H * d, H * d)


def _lru_gate_weights(wa, ba, wx, bx):
    C = LRU_CH
    ncg = D_LRU // C
    hpg = C // LRU_HEAD_DIM
    ws, bs = [], []
    for c in range(ncg):
        hs = slice(c * hpg, (c + 1) * hpg)
        cols, bias = [], []
        for d in range(2):
            cols += [_block_diag(wa[d, hs]), _block_diag(wx[d, hs])]
            bias += [ba[d, hs].reshape(C), bx[d, hs].reshape(C)]
        ws.append(jnp.concatenate(cols, axis=1))
        bs.append(jnp.concatenate(bias)[None, :])
    return (-LOG2_E * jnp.stack(ws)).astype(BF16), -LOG2_E * jnp.stack(bs)


def _trunk(x, P):
    B, S, _ = x.shape
    x2 = x.reshape(B * S, D_MODEL)
    q, k, v, xl, yl, stats = _in_proj(x2, B, S, P["ln_in_g"], P["ln_in_b"], P["wq"], P["wkv"], P["wx"],
                               P["wy"], P["seg"], P["qg"], P["kg"], P["cos"], P["sin"])
    attn = _attention(q, k, v, B, S)
    lru = _rg_lru(xl, yl, B, S, P["conv_w"], P["conv_b"], P["wg"], P["bg"], P["lam"])
    h1, route = _out_proj(x2, stats, attn, lru, P["ln_in_g"], P["ln_in_b"], P["ag"], P["lg"], P["wo"],
                          P["g1"], P["b1"], P["wr"], P["br"])
    out = _moe(h1, route, P["tril"], P["w13"], P["w2"], P["g2"], P["b2"])
    return out.reshape(B, S, D_MODEL)


def kernel(x_prompt, x_sample, ln_in_g, ln_in_b, w_in, conv_w, conv_b, lru_wa, lru_ba, lru_wx, lru_bx, lru_lambda, q_norm_g, k_norm_g, attn_out_g, lru_out_g, w_out, ln1_g, ln1_b, router_wg, router_bg, router_we, router_be, exp_w1, exp_w3, exp_w2, ln2_g, ln2_b):
    assert w_in.shape[0] == DEPTH == 1
    S = x_prompt.shape[1]
    l = 0
    w = w_in[l]
    c0, c1, c2 = D_ATTN, D_ATTN + 2 * KV_WIDTH, D_ATTN + 2 * KV_WIDTH + D_LRU
    seg = _block_diag(jnp.full((N_Q_HEADS, HEAD_DIM, HEAD_DIM), 1.0 / HEAD_DIM, F32)).astype(BF16)
    cos_t, sin_t = _rope_tables(S)
    wg, bg = _lru_gate_weights(lru_wa[l], lru_ba[l], lru_wx[l], lru_bx[l])
    wr = jnp.concatenate([router_wg[l], router_we[l].reshape(D_MODEL, N_EXPERTS)], axis=1)
    wr = jnp.pad(wr, ((0, 0), (0, ROUTER_PAD - wr.shape[1])))
    wrh = wr.astype(BF16)
    wrl = (wr - wrh.astype(F32)).astype(BF16)
    wr_t = jnp.concatenate([wrh.T, wrl.T], axis=0)
    br = jnp.concatenate([router_bg[l], router_be[l].reshape(N_EXPERTS)])
    br = jnp.pad(br, (0, ROUTER_ROWS - br.shape[0]))[:, None]
    P = dict(
        ln_in_g=ln_in_g[None, :], ln_in_b=ln_in_b[None, :],
        wq=w[:, :c0].astype(BF16), wkv=w[:, c0:c1].astype(BF16),
        wx=w[:, c1:c2].astype(BF16), wy=w[:, c2:].astype(BF16),
        seg=seg, qg=jnp.tile(q_norm_g[l], N_Q_HEADS)[None, :],
        kg=jnp.tile(k_norm_g[l], N_KV_HEADS)[None, :], cos=cos_t, sin=sin_t,
        conv_w=conv_w[l], conv_b=conv_b[l][None, :], wg=wg, bg=bg, lam=lru_lambda[l],
        ag=attn_out_g[l][None, :], lg=lru_out_g[l][None, :], wo=w_out[l].astype(BF16),
        g1=ln1_g[l][None, :], b1=ln1_b[l][None, :], wr=wr_t, br=br,
        w13=jnp.concatenate([exp_w1[l], exp_w3[l]], axis=2).astype(BF16),
        w2=exp_w2[l].reshape(N_GROUPS, EXPERTS_PER_GROUP * D_EXPERT, D_MODEL).astype(BF16),
        g2=ln2_g[l][None, :], b2=ln2_b[l][None, :],
        tril=jnp.tril(jnp.ones((TM_MOE // 2, TM_MOE // 2), F32), -1).astype(BF16),
    )
    return (_trunk(x_prompt, P), _trunk(x_sample, P))
```

```python
import math

import jax
import jax.numpy as jnp
from jax import lax
from jax.experimental import pallas as pl
from jax.experimental.pallas import tpu as pltpu

F32 = jnp.float32
BF16 = jnp.bfloat16

D_MODEL = 1024
GRID_W = 64
D_ATTN = 512
HEAD_DIM = 64
N_Q_HEADS = 8
N_KV_HEADS = 2
Q_PER_KV = 4
KV_WIDTH = 128
ROPE_THETA = 10000.0
ROPE_SECTION = 32
QK_EPS = 1e-6
D_LRU = 512
N_LRU_HEADS = 8
LRU_HEAD_DIM = 64
CONV_WIDTH = 4
LRU_C = 8.0
N_GROUPS = 4
EXPERTS_PER_GROUP = 4
N_EXPERTS = 16
D_EXPERT = 256
LN_EPS = 1e-5
DEPTH = 1
DEEPNORM_ALPHA = (2.0 * DEPTH) ** 0.25
LOG2_E = math.log2(math.e)

LANES = 128
SUBLANES = 8
VMEM_LIMIT = 48 * 1024 * 1024

TM_IN = 1024
IN_PROJ_SPLIT = 4
TM_OUT = 1024
TQ = 1024
KEY_BLOCK = 1024
LRU_CH = 256
SEG = 256
LRU_SCAN_UNROLL = 4
LRU_COMBINE_ROWS = 128
TM_MOE = 512
MOE_CHUNK = 144
MOE_ALIGN_LOG2 = 4
MOE_ALIGN = 1 << MOE_ALIGN_LOG2
MOE_TILES_PER_STEP = 1
MOE_VMEM_LIMIT = 56 * 1024 * 1024
ROUTER_PAD = LANES
EXPERT_LANE0 = N_GROUPS


def _layer_norm(x, g, b):
    mu = jnp.mean(x, axis=-1, keepdims=True)
    xc = x - mu
    var = jnp.mean(xc * xc, axis=-1, keepdims=True)
    return xc * lax.rsqrt(var + LN_EPS) * g + b


def _rms_norm_rows(x, g, eps):
    return x * lax.rsqrt(jnp.mean(x * x, axis=-1, keepdims=True) + eps) * g


def _head_norm_rope(xf, seg, g, cos, sin):
    ms = jnp.dot((xf * xf).astype(BF16), seg, preferred_element_type=F32)
    xn = xf * lax.rsqrt(ms + QK_EPS) * g
    n = xn.shape[1]
    half = ROPE_SECTION // 2
    up = pltpu.roll(xn, n - half, axis=1)
    dn = pltpu.roll(xn, half, axis=1)
    lane = lax.broadcasted_iota(jnp.int32, xn.shape, 1)
    first = (lane % ROPE_SECTION) < half
    partner = jnp.where(first, up, dn)
    return xn * cos + partner * sin


def _in_proj_kernel(x_ref, g_ref, b_ref, wq_ref, wkv_ref, wx_ref, wy_ref, seg_ref,
                    qg_ref, kg_ref, cos_ref, sin_ref,
                    q_ref, k_ref, v_ref, xl_ref, yl_ref, st_ref):
    reps = D_ATTN // KV_WIDTH
    blk = x_ref.shape[0] // IN_PROJ_SPLIT
    for i in range(IN_PROJ_SPLIT):
        rows = slice(i * blk, (i + 1) * blk)
        x = x_ref[rows, :]
        mu = jnp.mean(x, axis=-1, keepdims=True)
        xc = x - mu
        rstd = lax.rsqrt(jnp.mean(xc * xc, axis=-1, keepdims=True) + LN_EPS)
        hb = (xc * rstd * g_ref[...] + b_ref[...]).astype(BF16)
        lane = lax.broadcasted_iota(jnp.int32, (blk, LANES), 1)
        st_ref[rows, :] = jnp.where(lane == 0, mu, jnp.where(lane == 1, rstd, 0.0))
        cos = cos_ref[rows, :]
        sin = sin_ref[rows, :]

        qf = jnp.dot(hb, wq_ref[...], preferred_element_type=F32)
        q = _head_norm_rope(qf, seg_ref[...], qg_ref[...],
                            jnp.concatenate([cos] * reps, axis=1),
                            jnp.concatenate([sin] * reps, axis=1))
        q_ref[rows, :] = (q * (HEAD_DIM ** -0.5 * LOG2_E)).astype(BF16)

        kvf = jnp.dot(hb, wkv_ref[...], preferred_element_type=F32)
        k = _head_norm_rope(kvf[:, :KV_WIDTH], seg_ref[:KV_WIDTH, :KV_WIDTH], kg_ref[...], cos, sin)
        kb = k.astype(BF16)
        vb = kvf[:, KV_WIDTH:].astype(BF16)
        ones = jnp.ones((blk, HEAD_DIM), BF16)
        for j in range(N_KV_HEADS):
            k_ref[j, rows, :] = kb[:, j * HEAD_DIM:(j + 1) * HEAD_DIM]
            v_ref[j, rows, :] = jnp.concatenate([vb[:, j * HEAD_DIM:(j + 1) * HEAD_DIM], ones], axis=1)

        xl_ref[rows, :] = jnp.dot(hb, wx_ref[...], preferred_element_type=F32)
        yl_ref[rows, :] = jnp.dot(hb, wy_ref[...], preferred_element_type=F32)


def _in_proj(x2, B, S, ln_g, ln_b, wq, wkv, wx, wy, seg, qg, kg, cos_t, sin_t):
    T = B * S
    tm = TM_IN
    nt = S // tm
    const = lambda i: (0, 0)
    return pl.pallas_call(
        _in_proj_kernel,
        grid=(T // tm,),
        in_specs=[
            pl.BlockSpec((tm, D_MODEL), lambda i: (i, 0)),
            pl.BlockSpec((1, D_MODEL), const),
            pl.BlockSpec((1, D_MODEL), const),
            pl.BlockSpec((D_MODEL, D_ATTN), const),
            pl.BlockSpec((D_MODEL, 2 * KV_WIDTH), const),
            pl.BlockSpec((D_MODEL, D_LRU), const),
            pl.BlockSpec((D_MODEL, D_LRU), const),
            pl.BlockSpec((D_ATTN, D_ATTN), const),
            pl.BlockSpec((1, D_ATTN), const),
            pl.BlockSpec((1, KV_WIDTH), const),
            pl.BlockSpec((tm, KV_WIDTH), lambda i: (i % nt, 0)),
            pl.BlockSpec((tm, KV_WIDTH), lambda i: (i % nt, 0)),
        ],
        out_specs=[
            pl.BlockSpec((tm, D_ATTN), lambda i: (i, 0)),
            pl.BlockSpec((None, N_KV_HEADS, tm, HEAD_DIM), lambda i: (i // nt, 0, i % nt, 0)),
            pl.BlockSpec((None, N_KV_HEADS, tm, 2 * HEAD_DIM), lambda i: (i // nt, 0, i % nt, 0)),
            pl.BlockSpec((tm, D_LRU), lambda i: (i, 0)),
            pl.BlockSpec((tm, D_LRU), lambda i: (i, 0)),
            pl.BlockSpec((tm, LANES), lambda i: (i, 0)),
        ],
        out_shape=[
            jax.ShapeDtypeStruct((T, D_ATTN), BF16),
            jax.ShapeDtypeStruct((B, N_KV_HEADS, S, HEAD_DIM), BF16),
            jax.ShapeDtypeStruct((B, N_KV_HEADS, S, 2 * HEAD_DIM), BF16),
            jax.ShapeDtypeStruct((T, D_LRU), F32),
            jax.ShapeDtypeStruct((T, D_LRU), F32),
            jax.ShapeDtypeStruct((T, LANES), F32),
        ],
        compiler_params=pltpu.CompilerParams(
            dimension_semantics=("parallel",), vmem_limit_bytes=VMEM_LIMIT),
        name="in_proj",
    )(x2, ln_g, ln_b, wq, wkv, wx, wy, seg, qg, kg, cos_t, sin_t)


def _attn_kernel(q_ref, k_ref, v_ref, o_ref):
    nblk = k_ref.shape[0] // KEY_BLOCK
    outs = []
    for h in range(Q_PER_KV):
        qh = q_ref[:, h * HEAD_DIM:(h + 1) * HEAD_DIM]
        m = acc = None
        for b in range(nblk):
            kb = k_ref[b * KEY_BLOCK:(b + 1) * KEY_BLOCK, :]
            vb = v_ref[b * KEY_BLOCK:(b + 1) * KEY_BLOCK, :]
            s = lax.dot_general(qh, kb, (((1,), (1,)), ((), ())), preferred_element_type=F32)
            mb = jnp.max(s, axis=1, keepdims=True)
            if b == 0:
                m = mb
                acc = jnp.dot(jnp.exp2(s - m).astype(BF16), vb, preferred_element_type=F32)
            else:
                m_new = jnp.maximum(m, mb)
                acc = (jnp.exp2(m - m_new) * acc
                       + jnp.dot(jnp.exp2(s - m_new).astype(BF16), vb, preferred_element_type=F32))
                m = m_new
        outs.append(acc[:, :HEAD_DIM] / acc[:, HEAD_DIM:HEAD_DIM + 1])
    o_ref[...] = jnp.concatenate(outs, axis=1)


def _attention(q, k, v, B, S):
    T = B * S
    nq = S // TQ
    width = Q_PER_KV * HEAD_DIM
    return pl.pallas_call(
        _attn_kernel,
        grid=(B, N_KV_HEADS, nq),
        in_specs=[
            pl.BlockSpec((TQ, width), lambda b, j, t: (b * nq + t, j)),
            pl.BlockSpec((None, None, S, HEAD_DIM), lambda b, j, t: (b, j, 0, 0)),
            pl.BlockSpec((None, None, S, 2 * HEAD_DIM), lambda b, j, t: (b, j, 0, 0)),
        ],
        out_specs=pl.BlockSpec((TQ, width), lambda b, j, t: (b * nq + t, j)),
        out_shape=jax.ShapeDtypeStruct((T, D_ATTN), F32),
        compiler_params=pltpu.CompilerParams(
            dimension_semantics=("parallel", "parallel", "parallel"),
            vmem_limit_bytes=VMEM_LIMIT),
        name="attention",
    )(q, k, v)


def _gelu_tanh(x):
    c = math.sqrt(2.0 / math.pi)
    return 0.5 * x * (1.0 + jnp.tanh(c * (x + 0.044715 * (x * x * x))))


def _rg_lru_kernel(x_ref, y_ref, cw_ref, cb_ref, wg_ref, bg_ref, lam_ref, o_ref,
                   af_ref, uf_ref, ab_ref, ub_ref, hf_ref, pf_ref, hb_ref, pb_ref, cf_ref, cbk_ref):
    S, C = x_ref.shape
    nslab = C // LANES
    nseg = S // SEG
    assert nseg == SUBLANES
    lam = lam_ref[...]
    log2_a_per_r = (-LRU_C * LOG2_E) * jnp.logaddexp(-lam, 0.0)
    cw = cw_ref[...]
    cb = cb_ref[...]
    bg = bg_ref[...]

    def gate_block(s, carry):
        start = pl.multiple_of(s * SEG, SEG)
        seg_rows = pl.ds(s, SEG, stride=SUBLANES)
        cur = x_ref[pl.ds(start, SEG), :]
        prev = x_ref[pl.ds(pl.multiple_of(jnp.maximum(start - SUBLANES, 0), SUBLANES), SUBLANES), :]
        prev = jnp.where(s > 0, prev, 0.0)
        nxt = x_ref[pl.ds(pl.multiple_of(jnp.minimum(start + SEG, S - SUBLANES), SUBLANES), SUBLANES), :]
        nxt = jnp.where(s < nseg - 1, nxt, 0.0)
        ext = jnp.concatenate([prev, cur, nxt], axis=0)
        xc = cb
        for tap in range(CONV_WIDTH):
            shift = CONV_WIDTH // 2 - tap
            src = ext if shift == 0 else pltpu.roll(ext, shift % ext.shape[0], axis=0)
            xc = xc + src[SUBLANES:SUBLANES + SEG, :] * cw[tap:tap + 1, :]
        z = jnp.dot(xc.astype(BF16), wg_ref[...], preferred_element_type=F32) + bg
        for d, (a_ref, u_ref) in enumerate(((af_ref, uf_ref), (ab_ref, ub_ref))):
            rg = 1.0 / (1.0 + jnp.exp2(z[:, (2 * d) * C:(2 * d + 1) * C]))
            ig = 1.0 / (1.0 + jnp.exp2(z[:, (2 * d + 1) * C:(2 * d + 2) * C]))
            a = jnp.exp2(log2_a_per_r[d:d + 1, :] * rg)
            s1 = 1.0 - a * a
            u = jnp.where(s1 > 0.0, s1 * lax.rsqrt(s1), 0.0) * (ig * xc)
            for slab in range(nslab):
                lanes = slice(slab * LANES, (slab + 1) * LANES)
                a_ref[slab, seg_rows, :] = a[:, lanes]
                u_ref[slab, seg_rows, :] = u[:, lanes]
        return carry

    lax.fori_loop(0, nseg, gate_block, 0, unroll=2)

    def two_steps(a_ref, u_ref, h_ref, p_ref, slab, j0, j1, h, p):
        r0 = pl.ds(pl.multiple_of(j0 * SUBLANES, SUBLANES), SUBLANES)
        r1 = pl.ds(pl.multiple_of(j1 * SUBLANES, SUBLANES), SUBLANES)
        a0, u0 = a_ref[slab, r0, :], u_ref[slab, r0, :]
        a1, u1 = a_ref[slab, r1, :], u_ref[slab, r1, :]
        a01 = a1 * a0
        u01 = a1 * u0 + u1
        h_ref[slab, r0, :] = a0 * h + u0
        p_ref[slab, r0, :] = a0 * p
        h = a01 * h + u01
        p = a01 * p
        h_ref[slab, r1, :] = h
        p_ref[slab, r1, :] = p
        return h, p

    def scan_pair(i, carry):
        out = []
        for slab in range(nslab):
            hf, pf, hb, pb = carry[slab]
            hf, pf = two_steps(af_ref, uf_ref, hf_ref, pf_ref, slab, 2 * i, 2 * i + 1, hf, pf)
            hb, pb = two_steps(ab_ref, ub_ref, hb_ref, pb_ref, slab,
                               SEG - 1 - 2 * i, SEG - 2 - 2 * i, hb, pb)
            out.append((hf, pf, hb, pb))
        return tuple(out)

    zero = jnp.zeros((SUBLANES, LANES), F32)
    one = jnp.ones((SUBLANES, LANES), F32)
    final = lax.fori_loop(0, SEG // 2, scan_pair, tuple((zero, one, zero, one) for _ in range(nslab)),
                          unroll=LRU_SCAN_UNROLL)

    for slab in range(nslab):
        hf, pf, hb, pb = final[slab]
        c = jnp.zeros((1, LANES), F32)
        fwd = [c]
        for sgm in range(1, nseg):
            c = hf[sgm - 1:sgm, :] + pf[sgm - 1:sgm, :] * c
            fwd.append(c)
        c = jnp.zeros((1, LANES), F32)
        bwd = [c]
        for sgm in range(nseg - 2, -1, -1):
            c = hb[sgm + 1:sgm + 2, :] + pb[sgm + 1:sgm + 2, :] * c
            bwd.append(c)
        cf_ref[slab] = jnp.concatenate(fwd, axis=0)
        cbk_ref[slab] = jnp.concatenate(bwd[::-1], axis=0)

    reps = LRU_COMBINE_ROWS // SUBLANES

    def combine(i, carry):
        rows = pl.ds(pl.multiple_of(i * LRU_COMBINE_ROWS, LRU_COMBINE_ROWS), LRU_COMBINE_ROWS)
        for slab in range(nslab):
            cf = jnp.concatenate([cf_ref[slab]] * reps, axis=0)
            cbk = jnp.concatenate([cbk_ref[slab]] * reps, axis=0)
            uf_ref[slab, rows, :] = ((hf_ref[slab, rows, :] + pf_ref[slab, rows, :] * cf)
                                     + (hb_ref[slab, rows, :] + pb_ref[slab, rows, :] * cbk))
        return carry

    lax.fori_loop(0, S // LRU_COMBINE_ROWS, combine, 0)

    def out_block(s, carry):
        start = pl.multiple_of(s * SEG, SEG)
        for slab in range(nslab):
            lanes = slice(slab * LANES, (slab + 1) * LANES)
            h = uf_ref[slab, pl.ds(s, SEG, stride=SUBLANES), :]
            o_ref[pl.ds(start, SEG), lanes] = h * _gelu_tanh(y_ref[pl.ds(start, SEG), lanes])
        return carry

    lax.fori_loop(0, nseg, out_block, 0)


def _rg_lru(xl, yl, B, S, conv_w, conv_b, wg, bg, lam):
    C = LRU_CH
    ncg = D_LRU // C
    nslab = C // LANES
    xl3 = xl.reshape(B, S, D_LRU)
    yl3 = yl.reshape(B, S, D_LRU)
    seg_buf = pltpu.VMEM((nslab, S, LANES), F32)
    carry_buf = pltpu.VMEM((nslab, S // SEG, LANES), F32)
    out = pl.pallas_call(
        _rg_lru_kernel,
        grid=(B, ncg),
        in_specs=[
            pl.BlockSpec((None, S, C), lambda b, c: (b, 0, c)),
            pl.BlockSpec((None, S, C), lambda b, c: (b, 0, c)),
            pl.BlockSpec((CONV_WIDTH, C), lambda b, c: (0, c)),
            pl.BlockSpec((1, C), lambda b, c: (0, c)),
            pl.BlockSpec((None, C, 4 * C), lambda b, c: (c, 0, 0)),
            pl.BlockSpec((None, 1, 4 * C), lambda b, c: (c, 0, 0)),
            pl.BlockSpec((2, C), lambda b, c: (0, c)),
        ],
        out_specs=pl.BlockSpec((None, S, C), lambda b, c: (b, 0, c)),
        out_shape=jax.ShapeDtypeStruct((B, S, D_LRU), F32),
        scratch_shapes=[seg_buf] * 8 + [carry_buf] * 2,
        compiler_params=pltpu.CompilerParams(
            dimension_semantics=("parallel", "parallel"), vmem_limit_bytes=VMEM_LIMIT),
        name="rg_lru",
    )(xl3, yl3, conv_w, conv_b, wg, bg, lam)
    return out.reshape(B * S, D_LRU)


ROUTER_ROWS = 24


def _router_gates_t(logits):
    row = lax.broadcasted_iota(jnp.int32, logits.shape, 0).astype(F32)
    neg = -jnp.inf
    big = float(2 * LANES)
    gl = jnp.where(row < N_GROUPS, logits, neg)
    gmax = jnp.max(gl, axis=0, keepdims=True)
    gidx = jnp.min(jnp.where(gl == gmax, row, big), axis=0, keepdims=True)
    g_sel_prob = 1.0 / jnp.sum(jnp.exp(gl - gmax), axis=0, keepdims=True)
    lo = EXPERT_LANE0 + EXPERTS_PER_GROUP * gidx
    el = jnp.where((row >= lo) & (row < lo + EXPERTS_PER_GROUP), logits, neg)
    v1 = jnp.max(el, axis=0, keepdims=True)
    i1 = jnp.min(jnp.where(el == v1, row, big), axis=0, keepdims=True)
    el2 = jnp.where(row == i1, neg, el)
    v2 = jnp.max(el2, axis=0, keepdims=True)
    i2 = jnp.min(jnp.where(el2 == v2, row, big), axis=0, keepdims=True)
    e2 = jnp.exp(v2 - v1)
    w1 = g_sel_prob / (1.0 + e2)
    w2 = g_sel_prob * e2 / (1.0 + e2)
    gates = jnp.where(row == i1, w1, jnp.where(row == i2, w2, 0.0))
    return jnp.where(row == 0.0, gidx, gates)


def _out_proj_kernel(x_ref, st_ref, attn_ref, lru_ref, lng_ref, lnb_ref, ag_ref, lg_ref, wo_ref,
                     g1_ref, b1_ref, wr_ref, br_ref, h1_ref, gate_ref):
    h = (x_ref[...] - st_ref[:, 0:1]) * st_ref[:, 1:2] * lng_ref[...] + lnb_ref[...]
    merged = jnp.concatenate(
        [_rms_norm_rows(attn_ref[...], ag_ref[...], LN_EPS),
         _rms_norm_rows(lru_ref[...], lg_ref[...], LN_EPS)], axis=1).astype(BF16)
    mix = jnp.dot(merged, wo_ref[...], preferred_element_type=F32)
    h1 = _layer_norm(DEEPNORM_ALPHA * h + mix, g1_ref[...], b1_ref[...])
    h1_ref[...] = h1
    hi = h1.astype(BF16)
    lo = (h1 - hi.astype(F32)).astype(BF16)
    nt = (((1,), (1,)), ((), ()))
    t_hi = lax.dot_general(wr_ref[...], hi, nt, preferred_element_type=F32)
    t_lo = lax.dot_general(wr_ref[:ROUTER_PAD, :], lo, nt, preferred_element_type=F32)
    logits_t = (t_hi[:ROUTER_ROWS] + t_hi[ROUTER_PAD:ROUTER_PAD + ROUTER_ROWS]
                + t_lo[:ROUTER_ROWS] + br_ref[...])
    route_t = _router_gates_t(logits_t)
    pad = jnp.zeros((ROUTER_PAD - ROUTER_ROWS, route_t.shape[1]), F32)
    gate_ref[...] = jnp.concatenate([route_t, pad], axis=0).T


def _out_proj(x2, stats, attn, lru, ln_g, ln_b, ag, lg, wo, g1, b1, wr, br):
    T = x2.shape[0]
    tm = TM_OUT
    const = lambda i: (0, 0)
    row = lambda i: (i, 0)
    return pl.pallas_call(
        _out_proj_kernel,
        grid=(T // tm,),
        in_specs=[
            pl.BlockSpec((tm, D_MODEL), row),
            pl.BlockSpec((tm, LANES), row),
            pl.BlockSpec((tm, D_ATTN), row),
            pl.BlockSpec((tm, D_LRU), row),
            pl.BlockSpec((1, D_MODEL), const),
            pl.BlockSpec((1, D_MODEL), const),
            pl.BlockSpec((1, D_ATTN), const),
            pl.BlockSpec((1, D_LRU), const),
            pl.BlockSpec((D_MODEL, D_MODEL), const),
            pl.BlockSpec((1, D_MODEL), const),
            pl.BlockSpec((1, D_MODEL), const),
            pl.BlockSpec((2 * ROUTER_PAD, D_MODEL), const),
            pl.BlockSpec((ROUTER_ROWS, 1), const),
        ],
        out_specs=[pl.BlockSpec((tm, D_MODEL), row), pl.BlockSpec((tm, ROUTER_PAD), row)],
        out_shape=[jax.ShapeDtypeStruct((T, D_MODEL), F32),
                   jax.ShapeDtypeStruct((T, ROUTER_PAD), F32)],
        compiler_params=pltpu.CompilerParams(
            dimension_semantics=("parallel",), vmem_limit_bytes=VMEM_LIMIT),
        name="out_proj",
    )(x2, stats, attn, lru, ln_g, ln_b, ag, lg, wo, g1, b1, wr, br)


def _lane_scalar(row, idx):
    lane = lax.broadcasted_iota(jnp.int32, row.shape, 1)
    return jnp.sum(jnp.where(lane == idx, row, 0.0))


def _moe_kernel(h1_ref, route_ref, tril_ref, w13_ref, w2_ref, g2_ref, b2_ref, o_ref,
                pt_ref, xs_ref, gs_ref, ys_ref):
    for t in range(MOE_TILES_PER_STEP):
        rows = pl.ds(t * TM_MOE, TM_MOE)
        _moe_tile(h1_ref.at[rows, :], route_ref.at[rows, :], tril_ref, w13_ref, w2_ref, g2_ref, b2_ref,
                  o_ref.at[rows, :], pt_ref.at[t], xs_ref.at[t], gs_ref.at[t], ys_ref.at[t])


def _moe_tile(h1_ref, route_ref, tril_ref, w13_ref, w2_ref, g2_ref, b2_ref, o_ref,
              pt_ref, xs_ref, gs_ref, ys_ref):
    tm = h1_ref.shape[0]
    rs = xs_ref.shape[0]
    c = MOE_CHUNK
    h1 = h1_ref[...]
    route = route_ref[...]
    lane = lax.broadcasted_iota(jnp.int32, route.shape, 1)
    gid = jnp.sum(jnp.where(lane == 0, route, 0.0), axis=1, keepdims=True)
    onehot = jnp.where(lane.astype(F32) == gid, 1.0, 0.0)
    half = tm // 2
    ohb = onehot.astype(BF16)
    tril = tril_ref[...]
    cnt_top = jnp.sum(onehot[:half], axis=0, keepdims=True)
    before = jnp.concatenate(
        [jnp.dot(tril, ohb[:half], preferred_element_type=F32),
         jnp.dot(tril, ohb[half:], preferred_element_type=F32) + cnt_top], axis=0)
    cnt_row = cnt_top + jnp.sum(onehot[half:], axis=0, keepdims=True)
    cnt = [_lane_scalar(cnt_row, g).astype(jnp.int32) for g in range(N_GROUPS)]
    begin = [jnp.int32(0)]
    for g in range(1, N_GROUPS):
        end = begin[g - 1] + cnt[g - 1] + (MOE_ALIGN - 1)
        shift = jnp.int32(MOE_ALIGN_LOG2)
        begin.append(lax.shift_left(lax.shift_right_logical(end, shift), shift))
    lane1 = lax.broadcasted_iota(jnp.int32, cnt_row.shape, 1)
    begin_row = sum(jnp.where(lane1 == g, begin[g].astype(F32), 0.0) for g in range(N_GROUPS))
    dest = jnp.sum(onehot * (before + begin_row), axis=1, keepdims=True)
    col = lax.broadcasted_iota(jnp.int32, (tm, rs), 1).astype(F32)
    pt_ref[...] = jnp.where(col == dest, 1.0, 0.0).astype(BF16)
    dest_row = jnp.broadcast_to(dest, (tm, LANES)).T[0:1, :]
    row = lax.broadcasted_iota(jnp.int32, (rs, tm), 0).astype(F32)
    p = jnp.where(row == dest_row, 1.0, 0.0).astype(BF16)

    xs_ref[...] = jnp.dot(p, h1.astype(BF16), preferred_element_type=F32).astype(BF16)
    rhi = route.astype(BF16)
    rlo = (route - rhi.astype(F32)).astype(BF16)
    gs2 = jnp.dot(p, jnp.concatenate([rhi, rlo], axis=1), preferred_element_type=F32)
    gs_ref[...] = gs2[:, :LANES] + gs2[:, LANES:]
    ys_ref[...] = jnp.zeros_like(ys_ref)

    def run_chunk(g, first_row):
        rows = pl.ds(pl.multiple_of(first_row, MOE_ALIGN), c)
        xj = xs_ref[rows, :]
        gsj = gs_ref[rows, :]
        lane_c = lax.broadcasted_iota(jnp.int32, gsj.shape, 1)
        hids = []
        for k in range(EXPERTS_PER_GROUP):
            e = g * EXPERTS_PER_GROUP + k
            up = jnp.dot(xj, w13_ref[e], preferred_element_type=F32)
            u1 = up[:, :D_EXPERT]
            ge = jnp.sum(jnp.where(lane_c == EXPERT_LANE0 + e, gsj, 0.0), axis=1, keepdims=True)
            hids.append(((u1 * jax.nn.sigmoid(u1)) * up[:, D_EXPERT:] * ge).astype(BF16))
        ys_ref[rows, :] += jnp.dot(jnp.concatenate(hids, axis=1), w2_ref[g],
                                   preferred_element_type=F32)

    for g in range(N_GROUPS):
        run_chunk(g, begin[g])

    extra = [(g, k) for g in range(N_GROUPS) for k in range(1, N_GROUPS)]
    n_extra = sum((cnt[g] > k * c).astype(jnp.int32) for g, k in extra)

    def extra_chunk(s, carry):
        seen = jnp.int32(0)
        g_sel = jnp.int32(0)
        k_sel = jnp.int32(0)
        for g, k in extra:
            active = cnt[g] > k * c
            hit = active & (seen == s)
            g_sel = jnp.where(hit, g, g_sel)
            k_sel = jnp.where(hit, k, k_sel)
            seen = seen + active.astype(jnp.int32)
        first = sum(jnp.where(g_sel == g, begin[g], 0) for g in range(N_GROUPS)) + k_sel * c
        run_chunk(g_sel, first)
        return carry

    lax.fori_loop(0, n_extra, extra_chunk, 0)

    ffn = jnp.dot(pt_ref[...], ys_ref[...].astype(BF16), preferred_element_type=F32)
    o_ref[...] = _layer_norm(DEEPNORM_ALPHA * h1 + ffn, g2_ref[...], b2_ref[...])


def _moe(h1, route, tril, w13, w2, g2, b2):
    T = h1.shape[0]
    tm = TM_MOE
    nt = MOE_TILES_PER_STEP
    assert tm <= MOE_CHUNK * N_GROUPS and MOE_CHUNK % MOE_ALIGN == 0
    rs = -(-(tm + (N_GROUPS - 1) * (MOE_ALIGN - 1)) // MOE_ALIGN) * MOE_ALIGN + MOE_CHUNK
    gw = EXPERTS_PER_GROUP * D_EXPERT
    const2 = lambda i: (0, 0)
    const3 = lambda i: (0, 0, 0)
    resident = pl.Buffered(1)
    return pl.pallas_call(
        _moe_kernel,
        grid=(T // (nt * tm),),
        in_specs=[
            pl.BlockSpec((nt * tm, D_MODEL), lambda i: (i, 0)),
            pl.BlockSpec((nt * tm, ROUTER_PAD), lambda i: (i, 0)),
            pl.BlockSpec((tm // 2, tm // 2), const2, pipeline_mode=resident),
            pl.BlockSpec((N_EXPERTS, D_MODEL, 2 * D_EXPERT), const3, pipeline_mode=resident),
            pl.BlockSpec((N_GROUPS, gw, D_MODEL), const3, pipeline_mode=resident),
            pl.BlockSpec((1, D_MODEL), const2),
            pl.BlockSpec((1, D_MODEL), const2),
        ],
        out_specs=pl.BlockSpec((nt * tm, D_MODEL), lambda i: (i, 0)),
        out_shape=jax.ShapeDtypeStruct((T, D_MODEL), F32),
        scratch_shapes=[pltpu.VMEM((nt, tm, rs), BF16), pltpu.VMEM((nt, rs, D_MODEL), BF16),
                        pltpu.VMEM((nt, rs, ROUTER_PAD), F32), pltpu.VMEM((nt, rs, D_MODEL), F32)],
        compiler_params=pltpu.CompilerParams(
            dimension_semantics=("parallel",), vmem_limit_bytes=MOE_VMEM_LIMIT),
        name="moe",
    )(h1, route, tril, w13, w2, g2, b2)


def _rope_tables(S):
    t = jnp.arange(S, dtype=jnp.int32)
    row = (t // GRID_W).astype(F32)
    col = (t % GRID_W).astype(F32)
    half = ROPE_SECTION // 2
    inv_freq = ROPE_THETA ** (-jnp.arange(half, dtype=F32) / half)
    ang_r = row[:, None] * inv_freq
    ang_c = col[:, None] * inv_freq
    cos_h = jnp.concatenate([jnp.cos(ang_r)] * 2 + [jnp.cos(ang_c)] * 2, axis=1)
    sin_h = jnp.concatenate([-jnp.sin(ang_r), jnp.sin(ang_r), -jnp.sin(ang_c), jnp.sin(ang_c)], axis=1)
    reps = KV_WIDTH // HEAD_DIM
    return jnp.tile(cos_h, (1, reps)), jnp.tile(sin_h, (1, reps))


def _block_diag(w):
    H, d, _ = w.shape
    eye = jnp.eye(H, dtype=w.dtype)
    return (eye[:, None, :, None] * w[:, :, None, :]).reshape(H * d, H * d)


def _lru_gate_weights(wa, ba, wx, bx):
    C = LRU_CH
    ncg = D_LRU // C
    hpg = C // LRU_HEAD_DIM
    ws, bs = [], []
    for c in range(ncg):
        hs = slice(c * hpg, (c + 1) * hpg)
        cols, bias = [], []
        for d in range(2):
            cols += [_block_diag(wa[d, hs]), _block_diag(wx[d, hs])]
            bias += [ba[d, hs].reshape(C), bx[d, hs].reshape(C)]
        ws.append(jnp.concatenate(cols, axis=1))
        bs.append(jnp.concatenate(bias)[None, :])
    return (-LOG2_E * jnp.stack(ws)).astype(BF16), -LOG2_E * jnp.stack(bs)


def _trunk(x, P):
    B, S, _ = x.shape
    x2 = x.reshape(B * S, D_MODEL)
    q, k, v, xl, yl, stats = _in_proj(x2, B, S, P["ln_in_g"], P["ln_in_b"], P["wq"], P["wkv"], P["wx"],
                               P["wy"], P["seg"], P["qg"], P["kg"], P["cos"], P["sin"])
    attn = _attention(q, k, v, B, S)
    lru = _rg_lru(xl, yl, B, S, P["conv_w"], P["conv_b"], P["wg"], P["bg"], P["lam"])
    h1, route = _out_proj(x2, stats, attn, lru, P["ln_in_g"], P["ln_in_b"], P["ag"], P["lg"], P["wo"],
                          P["g1"], P["b1"], P["wr"], P["br"])
    out = _moe(h1, route, P["tril"], P["w13"], P["w2"], P["g2"], P["b2"])
    return out.reshape(B, S, D_MODEL)


def kernel(x_prompt, x_sample, ln_in_g, ln_in_b, w_in, conv_w, conv_b, lru_wa, lru_ba, lru_wx, lru_bx, lru_lambda, q_norm_g, k_norm_g, attn_out_g, lru_out_g, w_out, ln1_g, ln1_b, router_wg, router_bg, router_we, router_be, exp_w1, exp_w3, exp_w2, ln2_g, ln2_b):
    assert w_in.shape[0] == DEPTH == 1
    S = x_prompt.shape[1]
    l = 0
    w = w_in[l]
    c0, c1, c2 = D_ATTN, D_ATTN + 2 * KV_WIDTH, D_ATTN + 2 * KV_WIDTH + D_LRU
    seg = _block_diag(jnp.full((N_Q_HEADS, HEAD_DIM, HEAD_DIM), 1.0 / HEAD_DIM, F32)).astype(BF16)
    cos_t, sin_t = _rope_tables(S)
    wg, bg = _lru_gate_weights(lru_wa[l], lru_ba[l], lru_wx[l], lru_bx[l])
    wr = jnp.concatenate([router_wg[l], router_we[l].reshape(D_MODEL, N_EXPERTS)], axis=1)
    wr = jnp.pad(wr, ((0, 0), (0, ROUTER_PAD - wr.shape[1])))
    wrh = wr.astype(BF16)
    wrl = (wr - wrh.astype(F32)).astype(BF16)
    wr_t = jnp.concatenate([wrh.T, wrl.T], axis=0)
    br = jnp.concatenate([router_bg[l], router_be[l].reshape(N_EXPERTS)])
    br = jnp.pad(br, (0, ROUTER_ROWS - br.shape[0]))[:, None]
    P = dict(
        ln_in_g=ln_in_g[None, :], ln_in_b=ln_in_b[None, :],
        wq=w[:, :c0].astype(BF16), wkv=w[:, c0:c1].astype(BF16),
        wx=w[:, c1:c2].astype(BF16), wy=w[:, c2:].astype(BF16),
        seg=seg, qg=jnp.tile(q_norm_g[l], N_Q_HEADS)[None, :],
        kg=jnp.tile(k_norm_g[l], N_KV_HEADS)[None, :], cos=cos_t, sin=sin_t,
        conv_w=conv_w[l], conv_b=conv_b[l][None, :], wg=wg, bg=bg, lam=lru_lambda[l],
        ag=attn_out_g[l][None, :], lg=lru_out_g[l][None, :], wo=w_out[l].astype(BF16),
        g1=ln1_g[l][None, :], b1=ln1_b[l][None, :], wr=wr_t, br=br,
        w13=jnp.concatenate([exp_w1[l], exp_w3[l]], axis=2).astype(BF16),
        w2=exp_w2[l].reshape(N_GROUPS, EXPERTS_PER_GROUP * D_EXPERT, D_MODEL).astype(BF16),
        g2=ln2_g[l][None, :], b2=ln2_b[l][None, :],
        tril=jnp.tril(jnp.ones((TM_MOE // 2, TM_MOE // 2), F32), -1).astype(BF16),
    )
    return (_trunk(x_prompt, P), _trunk(x_sample, P))
```

```python
import math

import jax
import jax.numpy as jnp
from jax import lax
from jax.experimental import pallas as pl
from jax.experimental.pallas import tpu as pltpu

F32 = jnp.float32
BF16 = jnp.bfloat16

D_MODEL = 1024
GRID_W = 64
D_ATTN = 512
HEAD_DIM = 64
N_Q_HEADS = 8
N_KV_HEADS = 2
Q_PER_KV = 4
KV_WIDTH = 128
ROPE_THETA = 10000.0
ROPE_SECTION = 32
QK_EPS = 1e-6
D_LRU = 512
N_LRU_HEADS = 8
LRU_HEAD_DIM = 64
CONV_WIDTH = 4
LRU_C = 8.0
N_GROUPS = 4
EXPERTS_PER_GROUP = 4
N_EXPERTS = 16
D_EXPERT = 256
LN_EPS = 1e-5
DEPTH = 1
DEEPNORM_ALPHA = (2.0 * DEPTH) ** 0.25
LOG2_E = math.log2(math.e)

LANES = 128
SUBLANES = 8
VMEM_LIMIT = 48 * 1024 * 1024

TM_IN = 1024
IN_PROJ_SPLIT = 4
TM_OUT = 1024
TQ = 1024
KEY_BLOCK = 1024
LRU_CH = 256
SEG = 256
LRU_SCAN_UNROLL = 4
LRU_COMBINE_ROWS = 128
TM_MOE = 512
MOE_CHUNK = 144
MOE_ALIGN_LOG2 = 4
MOE_ALIGN = 1 << MOE_ALIGN_LOG2
MOE_TILES_PER_STEP = 1
MOE_VMEM_LIMIT = 56 * 1024 * 1024
ROUTER_PAD = LANES
EXPERT_LANE0 = N_GROUPS


def _layer_norm(x, g, b):
    mu = jnp.mean(x, axis=-1, keepdims=True)
    xc = x - mu
    var = jnp.mean(xc * xc, axis=-1, keepdims=True)
    return xc * lax.rsqrt(var + LN_EPS) * g + b


def _rms_norm_rows(x, g, eps):
    return x * lax.rsqrt(jnp.mean(x * x, axis=-1, keepdims=True) + eps) * g


def _head_norm_rope(xf, seg, g, cos, sin):
    ms = jnp.dot((xf * xf).astype(BF16), seg, preferred_element_type=F32)
    xn = xf * lax.rsqrt(ms + QK_EPS) * g
    n = xn.shape[1]
    half = ROPE_SECTION // 2
    up = pltpu.roll(xn, n - half, axis=1)
    dn = pltpu.roll(xn, half, axis=1)
    lane = lax.broadcasted_iota(jnp.int32, xn.shape, 1)
    first = (lane % ROPE_SECTION) < half
    partner = jnp.where(first, up, dn)
    return xn * cos + partner * sin


def _in_proj_kernel(x_ref, g_ref, b_ref, wq_ref, wkv_ref, wx_ref, wy_ref, seg_ref,
                    qg_ref, kg_ref, cos_ref, sin_ref,
                    q_ref, k_ref, v_ref, xl_ref, yl_ref, st_ref):
    reps = D_ATTN // KV_WIDTH
    blk = x_ref.shape[0] // IN_PROJ_SPLIT
    for i in range(IN_PROJ_SPLIT):
        rows = slice(i * blk, (i + 1) * blk)
        x = x_ref[rows, :]
        mu = jnp.mean(x, axis=-1, keepdims=True)
        xc = x - mu
        rstd = lax.rsqrt(jnp.mean(xc * xc, axis=-1, keepdims=True) + LN_EPS)
        hb = (xc * rstd * g_ref[...] + b_ref[...]).astype(BF16)
        lane = lax.broadcasted_iota(jnp.int32, (blk, LANES), 1)
        st_ref[rows, :] = jnp.where(lane == 0, mu, jnp.where(lane == 1, rstd, 0.0))
        cos = cos_ref[rows, :]
        sin = sin_ref[rows, :]

        qf = jnp.dot(hb, wq_ref[...], preferred_element_type=F32)
        q = _head_norm_rope(qf, seg_ref[...], qg_ref[...],
                            jnp.concatenate([cos] * reps, axis=1),
                            jnp.concatenate([sin] * reps, axis=1))
        q_ref[rows, :] = (q * (HEAD_DIM ** -0.5 * LOG2_E)).astype(BF16)

        kvf = jnp.dot(hb, wkv_ref[...], preferred_element_type=F32)
        k = _head_norm_rope(kvf[:, :KV_WIDTH], seg_ref[:KV_WIDTH, :KV_WIDTH], kg_ref[...], cos, sin)
        kb = k.astype(BF16)
        vb = kvf[:, KV_WIDTH:].astype(BF16)
        ones = jnp.ones((blk, HEAD_DIM), BF16)
        for j in range(N_KV_HEADS):
            k_ref[j, rows, :] = kb[:, j * HEAD_DIM:(j + 1) * HEAD_DIM]
            v_ref[j, rows, :] = jnp.concatenate([vb[:, j * HEAD_DIM:(j + 1) * HEAD_DIM], ones], axis=1)

        xl_ref[rows, :] = jnp.dot(hb, wx_ref[...], preferred_element_type=F32)
        yl_ref[rows, :] = jnp.dot(hb, wy_ref[...], preferred_element_type=F32)


def _in_proj(x2, B, S, ln_g, ln_b, wq, wkv, wx, wy, seg, qg, kg, cos_t, sin_t):
    T = B * S
    tm = TM_IN
    nt = S // tm
    const = lambda i: (0, 0)
    return pl.pallas_call(
        _in_proj_kernel,
        grid=(T // tm,),
        in_specs=[
            pl.BlockSpec((tm, D_MODEL), lambda i: (i, 0)),
            pl.BlockSpec((1, D_MODEL), const),
            pl.BlockSpec((1, D_MODEL), const),
            pl.BlockSpec((D_MODEL, D_ATTN), const),
            pl.BlockSpec((D_MODEL, 2 * KV_WIDTH), const),
            pl.BlockSpec((D_MODEL, D_LRU), const),
            pl.BlockSpec((D_MODEL, D_LRU), const),
            pl.BlockSpec((D_ATTN, D_ATTN), const),
            pl.BlockSpec((1, D_ATTN), const),
            pl.BlockSpec((1, KV_WIDTH), const),
            pl.BlockSpec((tm, KV_WIDTH), lambda i: (i % nt, 0)),
            pl.BlockSpec((tm, KV_WIDTH), lambda i: (i % nt, 0)),
        ],
        out_specs=[
            pl.BlockSpec((tm, D_ATTN), lambda i: (i, 0)),
            pl.BlockSpec((None, N_KV_HEADS, tm, HEAD_DIM), lambda i: (i // nt, 0, i % nt, 0)),
            pl.BlockSpec((None, N_KV_HEADS, tm, 2 * HEAD_DIM), lambda i: (i // nt, 0, i % nt, 0)),
            pl.BlockSpec((tm, D_LRU), lambda i: (i, 0)),
            pl.BlockSpec((tm, D_LRU), lambda i: (i, 0)),
            pl.BlockSpec((tm, LANES), lambda i: (i, 0)),
        ],
        out_shape=[
            jax.ShapeDtypeStruct((T, D_ATTN), BF16),
            jax.ShapeDtypeStruct((B, N_KV_HEADS, S, HEAD_DIM), BF16),
            jax.ShapeDtypeStruct((B, N_KV_HEADS, S, 2 * HEAD_DIM), BF16),
            jax.ShapeDtypeStruct((T, D_LRU), F32),
            jax.ShapeDtypeStruct((T, D_LRU), F32),
            jax.ShapeDtypeStruct((T, LANES), F32),
        ],
        compiler_params=pltpu.CompilerParams(
            dimension_semantics=("parallel",), vmem_limit_bytes=VMEM_LIMIT),
        name="in_proj",
    )(x2, ln_g, ln_b, wq, wkv, wx, wy, seg, qg, kg, cos_t, sin_t)


def _attn_kernel(q_ref, k_ref, v_ref, o_ref):
    nblk = k_ref.shape[1] // KEY_BLOCK
    outs = []
    for h in range(N_Q_HEADS):
        j = h // Q_PER_KV
        qh = q_ref[:, h * HEAD_DIM:(h + 1) * HEAD_DIM]
        m = acc = None
        for b in range(nblk):
            kb = k_ref[j, b * KEY_BLOCK:(b + 1) * KEY_BLOCK, :]
            vb = v_ref[j, b * KEY_BLOCK:(b + 1) * KEY_BLOCK, :]
            s = lax.dot_general(qh, kb, (((1,), (1,)), ((), ())), preferred_element_type=F32)
            mb = jnp.max(s, axis=1, keepdims=True)
            if b == 0:
                m = mb
                acc = jnp.dot(jnp.exp2(s - m).astype(BF16), vb, preferred_element_type=F32)
            else:
                m_new = jnp.maximum(m, mb)
                acc = (jnp.exp2(m - m_new) * acc
                       + jnp.dot(jnp.exp2(s - m_new).astype(BF16), vb, preferred_element_type=F32))
                m = m_new
        outs.append(acc[:, :HEAD_DIM] / acc[:, HEAD_DIM:HEAD_DIM + 1])
    o_ref[...] = jnp.concatenate(outs, axis=1)


def _attention(q, k, v, B, S):
    T = B * S
    nq = S // TQ
    return pl.pallas_call(
        _attn_kernel,
        grid=(B, nq),
        in_specs=[
            pl.BlockSpec((TQ, D_ATTN), lambda b, t: (b * nq + t, 0)),
            pl.BlockSpec((None, N_KV_HEADS, S, HEAD_DIM), lambda b, t: (b, 0, 0, 0)),
            pl.BlockSpec((None, N_KV_HEADS, S, 2 * HEAD_DIM), lambda b, t: (b, 0, 0, 0)),
        ],
        out_specs=pl.BlockSpec((TQ, D_ATTN), lambda b, t: (b * nq + t, 0)),
        out_shape=jax.ShapeDtypeStruct((T, D_ATTN), F32),
        compiler_params=pltpu.CompilerParams(
            dimension_semantics=("parallel", "parallel"),
            vmem_limit_bytes=VMEM_LIMIT),
        name="attention",
    )(q, k, v)


def _gelu_tanh(x):
    c = math.sqrt(2.0 / math.pi)
    return 0.5 * x * (1.0 + jnp.tanh(c * (x + 0.044715 * (x * x * x))))


def _rg_lru_kernel(x_ref, y_ref, cw_ref, cb_ref, wg_ref, bg_ref, lam_ref, o_ref,
                   af_ref, uf_ref, ab_ref, ub_ref, hf_ref, pf_ref, hb_ref, pb_ref, cf_ref, cbk_ref):
    S, C = x_ref.shape
    nslab = C // LANES
    nseg = S // SEG
    assert nseg == SUBLANES
    lam = lam_ref[...]
    log2_a_per_r = (-LRU_C * LOG2_E) * jnp.logaddexp(-lam, 0.0)
    cw = cw_ref[...]
    cb = cb_ref[...]
    bg = bg_ref[...]

    def gate_block(s, carry):
        start = pl.multiple_of(s * SEG, SEG)
        seg_rows = pl.ds(s, SEG, stride=SUBLANES)
        cur = x_ref[pl.ds(start, SEG), :]
        prev = x_ref[pl.ds(pl.multiple_of(jnp.maximum(start - SUBLANES, 0), SUBLANES), SUBLANES), :]
        prev = jnp.where(s > 0, prev, 0.0)
        nxt = x_ref[pl.ds(pl.multiple_of(jnp.minimum(start + SEG, S - SUBLANES), SUBLANES), SUBLANES), :]
        nxt = jnp.where(s < nseg - 1, nxt, 0.0)
        ext = jnp.concatenate([prev, cur, nxt], axis=0)
        xc = cb
        for tap in range(CONV_WIDTH):
            shift = CONV_WIDTH // 2 - tap
            src = ext if shift == 0 else pltpu.roll(ext, shift % ext.shape[0], axis=0)
            xc = xc + src[SUBLANES:SUBLANES + SEG, :] * cw[tap:tap + 1, :]
        z = jnp.dot(xc.astype(BF16), wg_ref[...], preferred_element_type=F32) + bg
        for d, (a_ref, u_ref) in enumerate(((af_ref, uf_ref), (ab_ref, ub_ref))):
            rg = 1.0 / (1.0 + jnp.exp2(z[:, (2 * d) * C:(2 * d + 1) * C]))
            ig = 1.0 / (1.0 + jnp.exp2(z[:, (2 * d + 1) * C:(2 * d + 2) * C]))
            a = jnp.exp2(log2_a_per_r[d:d + 1, :] * rg)
            s1 = 1.0 - a * a
            u = jnp.where(s1 > 0.0, s1 * lax.rsqrt(s1), 0.0) * (ig * xc)
            for slab in range(nslab):
                lanes = slice(slab * LANES, (slab + 1) * LANES)
                a_ref[slab, seg_rows, :] = a[:, lanes]
                u_ref[slab, seg_rows, :] = u[:, lanes]
        return carry

    lax.fori_loop(0, nseg, gate_block, 0, unroll=2)

    def two_steps(a_ref, u_ref, h_ref, p_ref, slab, j0, j1, h, p):
        r0 = pl.ds(pl.multiple_of(j0 * SUBLANES, SUBLANES), SUBLANES)
        r1 = pl.ds(pl.multiple_of(j1 * SUBLANES, SUBLANES), SUBLANES)
        a0, u0 = a_ref[slab, r0, :], u_ref[slab, r0, :]
        a1, u1 = a_ref[slab, r1, :], u_ref[slab, r1, :]
        a01 = a1 * a0
        u01 = a1 * u0 + u1
        h_ref[slab, r0, :] = a0 * h + u0
        p_ref[slab, r0, :] = a0 * p
        h = a01 * h + u01
        p = a01 * p
        h_ref[slab, r1, :] = h
        p_ref[slab, r1, :] = p
        return h, p

    def scan_pair(i, carry):
        out = []
        for slab in range(nslab):
            hf, pf, hb, pb = carry[slab]
            hf, pf = two_steps(af_ref, uf_ref, hf_ref, pf_ref, slab, 2 * i, 2 * i + 1, hf, pf)
            hb, pb = two_steps(ab_ref, ub_ref, hb_ref, pb_ref, slab,
                               SEG - 1 - 2 * i, SEG - 2 - 2 * i, hb, pb)
            out.append((hf, pf, hb, pb))
        return tuple(out)

    zero = jnp.zeros((SUBLANES, LANES), F32)
    one = jnp.ones((SUBLANES, LANES), F32)
    final = lax.fori_loop(0, SEG // 2, scan_pair, tuple((zero, one, zero, one) for _ in range(nslab)),
                          unroll=LRU_SCAN_UNROLL)

    for slab in range(nslab):
        hf, pf, hb, pb = final[slab]
        c = jnp.zeros((1, LANES), F32)
        fwd = [c]
        for sgm in range(1, nseg):
            c = hf[sgm - 1:sgm, :] + pf[sgm - 1:sgm, :] * c
            fwd.append(c)
        c = jnp.zeros((1, LANES), F32)
        bwd = [c]
        for sgm in range(nseg - 2, -1, -1):
            c = hb[sgm + 1:sgm + 2, :] + pb[sgm + 1:sgm + 2, :] * c
            bwd.append(c)
        cf_ref[slab] = jnp.concatenate(fwd, axis=0)
        cbk_ref[slab] = jnp.concatenate(bwd[::-1], axis=0)

    reps = LRU_COMBINE_ROWS // SUBLANES

    def combine(i, carry):
        rows = pl.ds(pl.multiple_of(i * LRU_COMBINE_ROWS, LRU_COMBINE_ROWS), LRU_COMBINE_ROWS)
        for slab in range(nslab):
            cf = jnp.concatenate([cf_ref[slab]] * reps, axis=0)
            cbk = jnp.concatenate([cbk_ref[slab]] * reps, axis=0)
            uf_ref[slab, rows, :] = ((hf_ref[slab, rows, :] + pf_ref[slab, rows, :] * cf)
                                     + (hb_ref[slab, rows, :] + pb_ref[slab, rows, :] * cbk))
        return carry

    lax.fori_loop(0, S // LRU_COMBINE_ROWS, combine, 0)

    def out_block(s, carry):
        start = pl.multiple_of(s * SEG, SEG)
        for slab in range(nslab):
            lanes = slice(slab * LANES, (slab + 1) * LANES)
            h = uf_ref[slab, pl.ds(s, SEG, stride=SUBLANES), :]
            o_ref[pl.ds(start, SEG), lanes] = h * _gelu_tanh(y_ref[pl.ds(start, SEG), lanes])
        return carry

    lax.fori_loop(0, nseg, out_block, 0)


def _rg_lru(xl, yl, B, S, conv_w, conv_b, wg, bg, lam):
    C = LRU_CH
    ncg = D_LRU // C
    nslab = C // LANES
    xl3 = xl.reshape(B, S, D_LRU)
    yl3 = yl.reshape(B, S, D_LRU)
    seg_buf = pltpu.VMEM((nslab, S, LANES), F32)
    carry_buf = pltpu.VMEM((nslab, S // SEG, LANES), F32)
    out = pl.pallas_call(
        _rg_lru_kernel,
        grid=(B, ncg),
        in_specs=[
            pl.BlockSpec((None, S, C), lambda b, c: (b, 0, c)),
            pl.BlockSpec((None, S, C), lambda b, c: (b, 0, c)),
            pl.BlockSpec((CONV_WIDTH, C), lambda b, c: (0, c)),
            pl.BlockSpec((1, C), lambda b, c: (0, c)),
            pl.BlockSpec((None, C, 4 * C), lambda b, c: (c, 0, 0)),
            pl.BlockSpec((None, 1, 4 * C), lambda b, c: (c, 0, 0)),
            pl.BlockSpec((2, C), lambda b, c: (0, c)),
        ],
        out_specs=pl.BlockSpec((None, S, C), lambda b, c: (b, 0, c)),
        out_shape=jax.ShapeDtypeStruct((B, S, D_LRU), F32),
        scratch_shapes=[seg_buf] * 8 + [carry_buf] * 2,
        compiler_params=pltpu.CompilerParams(
            dimension_semantics=("parallel", "parallel"), vmem_limit_bytes=VMEM_LIMIT),
        name="rg_lru",
    )(xl3, yl3, conv_w, conv_b, wg, bg, lam)
    return out.reshape(B * S, D_LRU)


ROUTER_ROWS = 24


def _router_gates_t(logits):
    row = lax.broadcasted_iota(jnp.int32, logits.shape, 0).astype(F32)
    neg = -jnp.inf
    big = float(2 * LANES)
    gl = jnp.where(row < N_GROUPS, logits, neg)
    gmax = jnp.max(gl, axis=0, keepdims=True)
    gidx = jnp.min(jnp.where(gl == gmax, row, big), axis=0, keepdims=True)
    g_sel_prob = 1.0 / jnp.sum(jnp.exp(gl - gmax), axis=0, keepdims=True)
    lo = EXPERT_LANE0 + EXPERTS_PER_GROUP * gidx
    el = jnp.where((row >= lo) & (row < lo + EXPERTS_PER_GROUP), logits, neg)
    v1 = jnp.max(el, axis=0, keepdims=True)
    i1 = jnp.min(jnp.where(el == v1, row, big), axis=0, keepdims=True)
    el2 = jnp.where(row == i1, neg, el)
    v2 = jnp.max(el2, axis=0, keepdims=True)
    i2 = jnp.min(jnp.where(el2 == v2, row, big), axis=0, keepdims=True)
    e2 = jnp.exp(v2 - v1)
    w1 = g_sel_prob / (1.0 + e2)
    w2 = g_sel_prob * e2 / (1.0 + e2)
    gates = jnp.where(row == i1, w1, jnp.where(row == i2, w2, 0.0))
    return jnp.where(row == 0.0, gidx, gates)


def _out_proj_kernel(x_ref, st_ref, attn_ref, lru_ref, lng_ref, lnb_ref, ag_ref, lg_ref, wo_ref,
                     g1_ref, b1_ref, wr_ref, br_ref, h1_ref, gate_ref):
    h = (x_ref[...] - st_ref[:, 0:1]) * st_ref[:, 1:2] * lng_ref[...] + lnb_ref[...]
    merged = jnp.concatenate(
        [_rms_norm_rows(attn_ref[...], ag_ref[...], LN_EPS),
         _rms_norm_rows(lru_ref[...], lg_ref[...], LN_EPS)], axis=1).astype(BF16)
    mix = jnp.dot(merged, wo_ref[...], preferred_element_type=F32)
    h1 = _layer_norm(DEEPNORM_ALPHA * h + mix, g1_ref[...], b1_ref[...])
    h1_ref[...] = h1
    hi = h1.astype(BF16)
    lo = (h1 - hi.astype(F32)).astype(BF16)
    nt = (((1,), (1,)), ((), ()))
    t_hi = lax.dot_general(wr_ref[...], hi, nt, preferred_element_type=F32)
    t_lo = lax.dot_general(wr_ref[:ROUTER_PAD, :], lo, nt, preferred_element_type=F32)
    logits_t = (t_hi[:ROUTER_ROWS] + t_hi[ROUTER_PAD:ROUTER_PAD + ROUTER_ROWS]
                + t_lo[:ROUTER_ROWS] + br_ref[...])
    route_t = _router_gates_t(logits_t)
    pad = jnp.zeros((ROUTER_PAD - ROUTER_ROWS, route_t.shape[1]), F32)
    gate_ref[...] = jnp.concatenate([route_t, pad], axis=0).T


def _out_proj(x2, stats, attn, lru, ln_g, ln_b, ag, lg, wo, g1, b1, wr, br):
    T = x2.shape[0]
    tm = TM_OUT
    const = lambda i: (0, 0)
    row = lambda i: (i, 0)
    return pl.pallas_call(
        _out_proj_kernel,
        grid=(T // tm,),
        in_specs=[
            pl.BlockSpec((tm, D_MODEL), row),
            pl.BlockSpec((tm, LANES), row),
            pl.BlockSpec((tm, D_ATTN), row),
            pl.BlockSpec((tm, D_LRU), row),
            pl.BlockSpec((1, D_MODEL), const),
            pl.BlockSpec((1, D_MODEL), const),
            pl.BlockSpec((1, D_ATTN), const),
            pl.BlockSpec((1, D_LRU), const),
            pl.BlockSpec((D_MODEL, D_MODEL), const),
            pl.BlockSpec((1, D_MODEL), const),
            pl.BlockSpec((1, D_MODEL), const),
            pl.BlockSpec((2 * ROUTER_PAD, D_MODEL), const),
            pl.BlockSpec((ROUTER_ROWS, 1), const),
        ],
        out_specs=[pl.BlockSpec((tm, D_MODEL), row), pl.BlockSpec((tm, ROUTER_PAD), row)],
        out_shape=[jax.ShapeDtypeStruct((T, D_MODEL), F32),
                   jax.ShapeDtypeStruct((T, ROUTER_PAD), F32)],
        compiler_params=pltpu.CompilerParams(
            dimension_semantics=("parallel",), vmem_limit_bytes=VMEM_LIMIT),
        name="out_proj",
    )(x2, stats, attn, lru, ln_g, ln_b, ag, lg, wo, g1, b1, wr, br)


def _lane_scalar(row, idx):
    lane = lax.broadcasted_iota(jnp.int32, row.shape, 1)
    return jnp.sum(jnp.where(lane == idx, row, 0.0))


def _moe_kernel(h1_ref, route_ref, tril_ref, w13_ref, w2_ref, g2_ref, b2_ref, o_ref,
                pt_ref, xs_ref, gs_ref, ys_ref):
    for t in range(MOE_TILES_PER_STEP):
        rows = pl.ds(t * TM_MOE, TM_MOE)
        _moe_tile(h1_ref.at[rows, :], route_ref.at[rows, :], tril_ref, w13_ref, w2_ref, g2_ref, b2_ref,
                  o_ref.at[rows, :], pt_ref.at[t], xs_ref.at[t], gs_ref.at[t], ys_ref.at[t])


def _moe_tile(h1_ref, route_ref, tril_ref, w13_ref, w2_ref, g2_ref, b2_ref, o_ref,
              pt_ref, xs_ref, gs_ref, ys_ref):
    tm = h1_ref.shape[0]
    rs = xs_ref.shape[0]
    c = MOE_CHUNK
    h1 = h1_ref[...]
    route = route_ref[...]
    lane = lax.broadcasted_iota(jnp.int32, route.shape, 1)
    gid = jnp.sum(jnp.where(lane == 0, route, 0.0), axis=1, keepdims=True)
    onehot = jnp.where(lane.astype(F32) == gid, 1.0, 0.0)
    half = tm // 2
    ohb = onehot.astype(BF16)
    tril = tril_ref[...]
    cnt_top = jnp.sum(onehot[:half], axis=0, keepdims=True)
    before = jnp.concatenate(
        [jnp.dot(tril, ohb[:half], preferred_element_type=F32),
         jnp.dot(tril, ohb[half:], preferred_element_type=F32) + cnt_top], axis=0)
    cnt_row = cnt_top + jnp.sum(onehot[half:], axis=0, keepdims=True)
    cnt = [_lane_scalar(cnt_row, g).astype(jnp.int32) for g in range(N_GROUPS)]
    begin = [jnp.int32(0)]
    for g in range(1, N_GROUPS):
        end = begin[g - 1] + cnt[g - 1] + (MOE_ALIGN - 1)
        shift = jnp.int32(MOE_ALIGN_LOG2)
        begin.append(lax.shift_left(lax.shift_right_logical(end, shift), shift))
    lane1 = lax.broadcasted_iota(jnp.int32, cnt_row.shape, 1)
    begin_row = sum(jnp.where(lane1 == g, begin[g].astype(F32), 0.0) for g in range(N_GROUPS))
    dest = jnp.sum(onehot * (before + begin_row), axis=1, keepdims=True)
    col = lax.broadcasted_iota(jnp.int32, (tm, rs), 1).astype(F32)
    pt_ref[...] = jnp.where(col == dest, 1.0, 0.0).astype(BF16)
    dest_row = jnp.broadcast_to(dest, (tm, LANES)).T[0:1, :]
    row = lax.broadcasted_iota(jnp.int32, (rs, tm), 0).astype(F32)
    p = jnp.where(row == dest_row, 1.0, 0.0).astype(BF16)

    xs_ref[...] = jnp.dot(p, h1.astype(BF16), preferred_element_type=F32).astype(BF16)
    rhi = route.astype(BF16)
    rlo = (route - rhi.astype(F32)).astype(BF16)
    gs2 = jnp.dot(p, jnp.concatenate([rhi, rlo], axis=1), preferred_element_type=F32)
    gs_ref[...] = gs2[:, :LANES] + gs2[:, LANES:]
    ys_ref[...] = jnp.zeros_like(ys_ref)

    def run_chunk(g, first_row):
        start = jnp.minimum(first_row, rs - c)
        rows = pl.ds(pl.multiple_of(start, MOE_ALIGN), c)
        xj = xs_ref[rows, :]
        row_c = lax.broadcasted_iota(jnp.int32, (c, LANES), 0)
        gsj = jnp.where(row_c >= first_row - start, gs_ref[rows, :], 0.0)
        lane_c = lax.broadcasted_iota(jnp.int32, gsj.shape, 1)
        hids = []
        for k in range(EXPERTS_PER_GROUP):
            e = g * EXPERTS_PER_GROUP + k
            up = jnp.dot(xj, w13_ref[e], preferred_element_type=F32)
            u1 = up[:, :D_EXPERT]
            ge = jnp.sum(jnp.where(lane_c == EXPERT_LANE0 + e, gsj, 0.0), axis=1, keepdims=True)
            hids.append(((u1 * jax.nn.sigmoid(u1)) * up[:, D_EXPERT:] * ge).astype(BF16))
        ys_ref[rows, :] += jnp.dot(jnp.concatenate(hids, axis=1), w2_ref[g],
                                   preferred_element_type=F32)

    for g in range(N_GROUPS):
        run_chunk(g, begin[g])

    extra = [(g, k) for g in range(N_GROUPS) for k in range(1, N_GROUPS)]
    n_extra = sum((cnt[g] > k * c).astype(jnp.int32) for g, k in extra)

    def extra_chunk(s, carry):
        seen = jnp.int32(0)
        g_sel = jnp.int32(0)
        k_sel = jnp.int32(0)
        for g, k in extra:
            active = cnt[g] > k * c
            hit = active & (seen == s)
            g_sel = jnp.where(hit, g, g_sel)
            k_sel = jnp.where(hit, k, k_sel)
            seen = seen + active.astype(jnp.int32)
        first = sum(jnp.where(g_sel == g, begin[g], 0) for g in range(N_GROUPS)) + k_sel * c
        run_chunk(g_sel, first)
        return carry

    lax.fori_loop(0, n_extra, extra_chunk, 0)

    ffn = jnp.dot(pt_ref[...], ys_ref[...].astype(BF16), preferred_element_type=F32)
    o_ref[...] = _layer_norm(DEEPNORM_ALPHA * h1 + ffn, g2_ref[...], b2_ref[...])


def _moe(h1, route, tril, w13, w2, g2, b2):
    T = h1.shape[0]
    tm = TM_MOE
    nt = MOE_TILES_PER_STEP
    assert tm <= MOE_CHUNK * N_GROUPS and MOE_CHUNK % MOE_ALIGN == 0
    rs = -(-(tm + (N_GROUPS - 1) * (MOE_ALIGN - 1)) // MOE_ALIGN) * MOE_ALIGN
    gw = EXPERTS_PER_GROUP * D_EXPERT
    const2 = lambda i: (0, 0)
    const3 = lambda i: (0, 0, 0)
    resident = pl.Buffered(1)
    return pl.pallas_call(
        _moe_kernel,
        grid=(T // (nt * tm),),
        in_specs=[
            pl.BlockSpec((nt * tm, D_MODEL), lambda i: (i, 0)),
            pl.BlockSpec((nt * tm, ROUTER_PAD), lambda i: (i, 0)),
            pl.BlockSpec((tm // 2, tm // 2), const2, pipeline_mode=resident),
            pl.BlockSpec((N_EXPERTS, D_MODEL, 2 * D_EXPERT), const3, pipeline_mode=resident),
            pl.BlockSpec((N_GROUPS, gw, D_MODEL), const3, pipeline_mode=resident),
            pl.BlockSpec((1, D_MODEL), const2),
            pl.BlockSpec((1, D_MODEL), const2),
        ],
        out_specs=pl.BlockSpec((nt * tm, D_MODEL), lambda i: (i, 0)),
        out_shape=jax.ShapeDtypeStruct((T, D_MODEL), F32),
        scratch_shapes=[pltpu.VMEM((nt, tm, rs), BF16), pltpu.VMEM((nt, rs, D_MODEL), BF16),
                        pltpu.VMEM((nt, rs, ROUTER_PAD), F32), pltpu.VMEM((nt, rs, D_MODEL), F32)],
        compiler_params=pltpu.CompilerParams(
            dimension_semantics=("parallel",), vmem_limit_bytes=MOE_VMEM_LIMIT),
        name="moe",
    )(h1, route, tril, w13, w2, g2, b2)


def _rope_tables(S):
    t = jnp.arange(S, dtype=jnp.int32)
    row = (t // GRID_W).astype(F32)
    col = (t % GRID_W).astype(F32)
    half = ROPE_SECTION // 2
    inv_freq = ROPE_THETA ** (-jnp.arange(half, dtype=F32) / half)
    ang_r = row[:, None] * inv_freq
    ang_c = col[:, None] * inv_freq
    cos_h = jnp.concatenate([jnp.cos(ang_r)] * 2 + [jnp.cos(ang_c)] * 2, axis=1)
    sin_h = jnp.concatenate([-jnp.sin(ang_r), jnp.sin(ang_r), -jnp.sin(ang_c), jnp.sin(ang_c)], axis=1)
    reps = KV_WIDTH // HEAD_DIM
    return jnp.tile(cos_h, (1, reps)), jnp.tile(sin_h, (1, reps))


def _block_diag(w):
    H, d, _ = w.shape
    eye = jnp.eye(H, dtype=w.dtype)
    return (eye[:, None, :, None] * w[:, :, None, :]).reshape(H * d, H * d)


def _lru_gate_weights(wa, ba, wx, bx):
    C = LRU_CH
    ncg = D_LRU // C
    hpg = C // LRU_HEAD_DIM
    ws, bs = [], []
    for c in range(ncg):
        hs = slice(c * hpg, (c + 1) * hpg)
        cols, bias = [], []
        for d in range(2):
            cols += [_block_diag(wa[d, hs]), _block_diag(wx[d, hs])]
            bias += [ba[d, hs].reshape(C), bx[d, hs].reshape(C)]
        ws.append(jnp.concatenate(cols, axis=1))
        bs.append(jnp.concatenate(bias)[None, :])
    return (-LOG2_E * jnp.stack(ws)).astype(BF16), -LOG2_E * jnp.stack(bs)


def _trunk(x, P):
    B, S, _ = x.shape
    x2 = x.reshape(B * S, D_MODEL)
    q, k, v, xl, yl, stats = _in_proj(x2, B, S, P["ln_in_g"], P["ln_in_b"], P["wq"], P["wkv"], P["wx"],
                               P["wy"], P["seg"], P["qg"], P["kg"], P["cos"], P["sin"])
    attn = _attention(q, k, v, B, S)
    lru = _rg_lru(xl, yl, B, S, P["conv_w"], P["conv_b"], P["wg"], P["bg"], P["lam"])
    h1, route = _out_proj(x2, stats, attn, lru, P["ln_in_g"], P["ln_in_b"], P["ag"], P["lg"], P["wo"],
                          P["g1"], P["b1"], P["wr"], P["br"])
    out = _moe(h1, route, P["tril"], P["w13"], P["w2"], P["g2"], P["b2"])
    return out.reshape(B, S, D_MODEL)


def kernel(x_prompt, x_sample, ln_in_g, ln_in_b, w_in, conv_w, conv_b, lru_wa, lru_ba, lru_wx, lru_bx, lru_lambda, q_norm_g, k_norm_g, attn_out_g, lru_out_g, w_out, ln1_g, ln1_b, router_wg, router_bg, router_we, router_be, exp_w1, exp_w3, exp_w2, ln2_g, ln2_b):
    assert w_in.shape[0] == DEPTH == 1
    S = x_prompt.shape[1]
    l = 0
    w = w_in[l]
    c0, c1, c2 = D_ATTN, D_ATTN + 2 * KV_WIDTH, D_ATTN + 2 * KV_WIDTH + D_LRU
    seg = _block_diag(jnp.full((N_Q_HEADS, HEAD_DIM, HEAD_DIM), 1.0 / HEAD_DIM, F32)).astype(BF16)
    cos_t, sin_t = _rope_tables(S)
    wg, bg = _lru_gate_weights(lru_wa[l], lru_ba[l], lru_wx[l], lru_bx[l])
    wr = jnp.concatenate([router_wg[l], router_we[l].reshape(D_MODEL, N_EXPERTS)], axis=1)
    wr = jnp.pad(wr, ((0, 0), (0, ROUTER_PAD - wr.shape[1])))
    wrh = wr.astype(BF16)
    wrl = (wr - wrh.astype(F32)).astype(BF16)
    wr_t = jnp.concatenate([wrh.T, wrl.T], axis=0)
    br = jnp.concatenate([router_bg[l], router_be[l].reshape(N_EXPERTS)])
    br = jnp.pad(br, (0, ROUTER_ROWS - br.shape[0]))[:, None]
    P = dict(
        ln_in_g=ln_in_g[None, :], ln_in_b=ln_in_b[None, :],
        wq=w[:, :c0].astype(BF16), wkv=w[:, c0:c1].astype(BF16),
        wx=w[:, c1:c2].astype(BF16), wy=w[:, c2:].astype(BF16),
        seg=seg, qg=jnp.tile(q_norm_g[l], N_Q_HEADS)[None, :],
        kg=jnp.tile(k_norm_g[l], N_KV_HEADS)[None, :], cos=cos_t, sin=sin_t,
        conv_w=conv_w[l], conv_b=conv_b[l][None, :], wg=wg, bg=bg, lam=lru_lambda[l],
        ag=attn_out_g[l][None, :], lg=lru_out_g[l][None, :], wo=w_out[l].astype(BF16),
        g1=ln1_g[l][None, :], b1=ln1_b[l][None, :], wr=wr_t, br=br,
        w13=jnp.concatenate([exp_w1[l], exp_w3[l]], axis=2).astype(BF16),
        w2=exp_w2[l].reshape(N_GROUPS, EXPERTS_PER_GROUP * D_EXPERT, D_MODEL).astype(BF16),
        g2=ln2_g[l][None, :], b2=ln2_b[l][None, :],
        tril=jnp.tril(jnp.ones((TM_MOE // 2, TM_MOE // 2), F32), -1).astype(BF16),
    )
    return (_trunk(x_prompt, P), _trunk(x_sample, P))
```

```python
import math

import jax
import jax.numpy as jnp
from jax import lax
from jax.experimental import pallas as pl
from jax.experimental.pallas import tpu as pltpu

F32 = jnp.float32
BF16 = jnp.bfloat16

D_MODEL = 1024
GRID_W = 64
D_ATTN = 512
HEAD_DIM = 64
N_Q_HEADS = 8
N_KV_HEADS = 2
Q_PER_KV = 4
KV_WIDTH = 128
ROPE_THETA = 10000.0
ROPE_SECTION = 32
QK_EPS = 1e-6
D_LRU = 512
N_LRU_HEADS = 8
LRU_HEAD_DIM = 64
CONV_WIDTH = 4
LRU_C = 8.0
N_GROUPS = 4
EXPERTS_PER_GROUP = 4
N_EXPERTS = 16
D_EXPERT = 256
LN_EPS = 1e-5
DEPTH = 1
DEEPNORM_ALPHA = (2.0 * DEPTH) ** 0.25
LOG2_E = math.log2(math.e)

LANES = 128
SUBLANES = 8
VMEM_LIMIT = 48 * 1024 * 1024

TM_IN = 1024
IN_PROJ_SPLIT = 4
TM_OUT = 1024
TQ = 1024
KEY_BLOCK = 1024
LRU_CH = 256
SEG = 256
LRU_SCAN_UNROLL = 4
LRU_COMBINE_ROWS = 128
TM_MOE = 512
MOE_CHUNK = 144
MOE_ALIGN_LOG2 = 4
MOE_ALIGN = 1 << MOE_ALIGN_LOG2
MOE_TILES_PER_STEP = 1
MOE_VMEM_LIMIT = 56 * 1024 * 1024
ROUTER_PAD = LANES
EXPERT_LANE0 = N_GROUPS


def _layer_norm(x, g, b):
    mu = jnp.mean(x, axis=-1, keepdims=True)
    xc = x - mu
    var = jnp.mean(xc * xc, axis=-1, keepdims=True)
    return xc * lax.rsqrt(var + LN_EPS) * g + b


def _rms_norm_rows(x, g, eps):
    return x * lax.rsqrt(jnp.mean(x * x, axis=-1, keepdims=True) + eps) * g


def _head_norm_rope(xf, seg, g, cos, sin):
    ms = jnp.dot((xf * xf).astype(BF16), seg, preferred_element_type=F32)
    xn = xf * lax.rsqrt(ms + QK_EPS) * g
    n = xn.shape[1]
    half = ROPE_SECTION // 2
    up = pltpu.roll(xn, n - half, axis=1)
    dn = pltpu.roll(xn, half, axis=1)
    lane = lax.broadcasted_iota(jnp.int32, xn.shape, 1)
    first = (lane % ROPE_SECTION) < half
    partner = jnp.where(first, up, dn)
    return xn * cos + partner * sin


def _in_proj_kernel(x_ref, g_ref, b_ref, wq_ref, wkv_ref, wx_ref, wy_ref, seg_ref,
                    qg_ref, kg_ref, cos_ref, sin_ref,
                    q_ref, k_ref, v_ref, xl_ref, yl_ref, st_ref):
    reps = D_ATTN // KV_WIDTH
    blk = x_ref.shape[0] // IN_PROJ_SPLIT
    for i in range(IN_PROJ_SPLIT):
        rows = slice(i * blk, (i + 1) * blk)
        x = x_ref[rows, :]
        mu = jnp.mean(x, axis=-1, keepdims=True)
        xc = x - mu
        rstd = lax.rsqrt(jnp.mean(xc * xc, axis=-1, keepdims=True) + LN_EPS)
        hb = (xc * rstd * g_ref[...] + b_ref[...]).astype(BF16)
        lane = lax.broadcasted_iota(jnp.int32, (blk, LANES), 1)
        st_ref[rows, :] = jnp.where(lane == 0, mu, jnp.where(lane == 1, rstd, 0.0))
        cos = cos_ref[rows, :]
        sin = sin_ref[rows, :]

        qf = jnp.dot(hb, wq_ref[...], preferred_element_type=F32)
        q = _head_norm_rope(qf, seg_ref[...], qg_ref[...],
                            jnp.concatenate([cos] * reps, axis=1),
                            jnp.concatenate([sin] * reps, axis=1))
        q_ref[rows, :] = (q * (HEAD_DIM ** -0.5 * LOG2_E)).astype(BF16)

        kvf = jnp.dot(hb, wkv_ref[...], preferred_element_type=F32)
        k = _head_norm_rope(kvf[:, :KV_WIDTH], seg_ref[:KV_WIDTH, :KV_WIDTH], kg_ref[...], cos, sin)
        kb = k.astype(BF16)
        vb = kvf[:, KV_WIDTH:].astype(BF16)
        ones = jnp.ones((blk, HEAD_DIM), BF16)
        for j in range(N_KV_HEADS):
            k_ref[j, rows, :] = kb[:, j * HEAD_DIM:(j + 1) * HEAD_DIM]
            v_ref[j, rows, :] = jnp.concatenate([vb[:, j * HEAD_DIM:(j + 1) * HEAD_DIM], ones], axis=1)

        xl_ref[rows, :] = jnp.dot(hb, wx_ref[...], preferred_element_type=F32)
        yl_ref[rows, :] = jnp.dot(hb, wy_ref[...], preferred_element_type=F32)


def _in_proj(x2, B, S, ln_g, ln_b, wq, wkv, wx, wy, seg, qg, kg, cos_t, sin_t):
    T = B * S
    tm = TM_IN
    nt = S // tm
    const = lambda i: (0, 0)
    return pl.pallas_call(
        _in_proj_kernel,
        grid=(T // tm,),
        in_specs=[
            pl.BlockSpec((tm, D_MODEL), lambda i: (i, 0)),
            pl.BlockSpec((1, D_MODEL), const),
            pl.BlockSpec((1, D_MODEL), const),
            pl.BlockSpec((D_MODEL, D_ATTN), const),
            pl.BlockSpec((D_MODEL, 2 * KV_WIDTH), const),
            pl.BlockSpec((D_MODEL, D_LRU), const),
            pl.BlockSpec((D_MODEL, D_LRU), const),
            pl.BlockSpec((D_ATTN, D_ATTN), const),
            pl.BlockSpec((1, D_ATTN), const),
            pl.BlockSpec((1, KV_WIDTH), const),
            pl.BlockSpec((tm, KV_WIDTH), lambda i: (i % nt, 0)),
            pl.BlockSpec((tm, KV_WIDTH), lambda i: (i % nt, 0)),
        ],
        out_specs=[
            pl.BlockSpec((tm, D_ATTN), lambda i: (i, 0)),
            pl.BlockSpec((None, N_KV_HEADS, tm, HEAD_DIM), lambda i: (i // nt, 0, i % nt, 0)),
            pl.BlockSpec((None, N_KV_HEADS, tm, 2 * HEAD_DIM), lambda i: (i // nt, 0, i % nt, 0)),
            pl.BlockSpec((tm, D_LRU), lambda i: (i, 0)),
            pl.BlockSpec((tm, D_LRU), lambda i: (i, 0)),
            pl.BlockSpec((tm, LANES), lambda i: (i, 0)),
        ],
        out_shape=[
            jax.ShapeDtypeStruct((T, D_ATTN), BF16),
            jax.ShapeDtypeStruct((B, N_KV_HEADS, S, HEAD_DIM), BF16),
            jax.ShapeDtypeStruct((B, N_KV_HEADS, S, 2 * HEAD_DIM), BF16),
            jax.ShapeDtypeStruct((T, D_LRU), F32),
            jax.ShapeDtypeStruct((T, D_LRU), F32),
            jax.ShapeDtypeStruct((T, LANES), F32),
        ],
        compiler_params=pltpu.CompilerParams(
            dimension_semantics=("parallel",), vmem_limit_bytes=VMEM_LIMIT),
        name="in_proj",
    )(x2, ln_g, ln_b, wq, wkv, wx, wy, seg, qg, kg, cos_t, sin_t)


def _attn_kernel(q_ref, k_ref, v_ref, o_ref):
    nblk = k_ref.shape[1] // KEY_BLOCK
    outs = []
    for h in range(N_Q_HEADS):
        j = h // Q_PER_KV
        qh = q_ref[:, h * HEAD_DIM:(h + 1) * HEAD_DIM]
        m = acc = None
        for b in range(nblk):
            kb = k_ref[j, b * KEY_BLOCK:(b + 1) * KEY_BLOCK, :]
            vb = v_ref[j, b * KEY_BLOCK:(b + 1) * KEY_BLOCK, :]
            s = lax.dot_general(qh, kb, (((1,), (1,)), ((), ())), preferred_element_type=F32)
            mb = jnp.max(s, axis=1, keepdims=True)
            if b == 0:
                m = mb
                acc = jnp.dot(jnp.exp2(s - m).astype(BF16), vb, preferred_element_type=F32)
            else:
                m_new = jnp.maximum(m, mb)
                acc = (jnp.exp2(m - m_new) * acc
                       + jnp.dot(jnp.exp2(s - m_new).astype(BF16), vb, preferred_element_type=F32))
                m = m_new
        outs.append(acc[:, :HEAD_DIM] / acc[:, HEAD_DIM:HEAD_DIM + 1])
    o_ref[...] = jnp.concatenate(outs, axis=1)


def _attention(q, k, v, B, S):
    T = B * S
    nq = S // TQ
    return pl.pallas_call(
        _attn_kernel,
        grid=(B, nq),
        in_specs=[
            pl.BlockSpec((TQ, D_ATTN), lambda b, t: (b * nq + t, 0)),
            pl.BlockSpec((None, N_KV_HEADS, S, HEAD_DIM), lambda b, t: (b, 0, 0, 0)),
            pl.BlockSpec((None, N_KV_HEADS, S, 2 * HEAD_DIM), lambda b, t: (b, 0, 0, 0)),
        ],
        out_specs=pl.BlockSpec((TQ, D_ATTN), lambda b, t: (b * nq + t, 0)),
        out_shape=jax.ShapeDtypeStruct((T, D_ATTN), F32),
        compiler_params=pltpu.CompilerParams(
            dimension_semantics=("parallel", "parallel"),
            vmem_limit_bytes=VMEM_LIMIT),
        name="attention",
    )(q, k, v)


def _gelu_tanh(x):
    c = math.sqrt(2.0 / math.pi)
    return 0.5 * x * (1.0 + jnp.tanh(c * (x + 0.044715 * (x * x * x))))


def _rg_lru_kernel(x_ref, y_ref, cw_ref, cb_ref, wg_ref, bg_ref, lam_ref, o_ref,
                   af_ref, uf_ref, ab_ref, ub_ref, hf_ref, pf_ref, hb_ref, pb_ref, cf_ref, cbk_ref):
    S, C = x_ref.shape
    nslab = C // LANES
    nseg = S // SEG
    assert nseg == SUBLANES
    lam = lam_ref[...]
    log2_a_per_r = (-LRU_C * LOG2_E) * jnp.logaddexp(-lam, 0.0)
    cw = cw_ref[...]
    cb = cb_ref[...]
    bg = bg_ref[...]

    def gate_block(s, carry):
        start = pl.multiple_of(s * SEG, SEG)
        seg_rows = pl.ds(s, SEG, stride=SUBLANES)
        cur = x_ref[pl.ds(start, SEG), :]
        prev = x_ref[pl.ds(pl.multiple_of(jnp.maximum(start - SUBLANES, 0), SUBLANES), SUBLANES), :]
        prev = jnp.where(s > 0, prev, 0.0)
        nxt = x_ref[pl.ds(pl.multiple_of(jnp.minimum(start + SEG, S - SUBLANES), SUBLANES), SUBLANES), :]
        nxt = jnp.where(s < nseg - 1, nxt, 0.0)
        ext = jnp.concatenate([prev, cur, nxt], axis=0)
        xc = cb
        for tap in range(CONV_WIDTH):
            shift = CONV_WIDTH // 2 - tap
            src = ext if shift == 0 else pltpu.roll(ext, shift % ext.shape[0], axis=0)
            xc = xc + src[SUBLANES:SUBLANES + SEG, :] * cw[tap:tap + 1, :]
        z = jnp.dot(xc.astype(BF16), wg_ref[...], preferred_element_type=F32) + bg
        for d, (a_ref, u_ref) in enumerate(((af_ref, uf_ref), (ab_ref, ub_ref))):
            rg = 1.0 / (1.0 + jnp.exp2(z[:, (2 * d) * C:(2 * d + 1) * C]))
            ig = 1.0 / (1.0 + jnp.exp2(z[:, (2 * d + 1) * C:(2 * d + 2) * C]))
            a = jnp.exp2(log2_a_per_r[d:d + 1, :] * rg)
            s1 = 1.0 - a * a
            u = jnp.where(s1 > 0.0, s1 * lax.rsqrt(s1), 0.0) * (ig * xc)
            for slab in range(nslab):
                lanes = slice(slab * LANES, (slab + 1) * LANES)
                a_ref[slab, seg_rows, :] = a[:, lanes]
                u_ref[slab, seg_rows, :] = u[:, lanes]
        return carry

    lax.fori_loop(0, nseg, gate_block, 0, unroll=2)

    def two_steps(a_ref, u_ref, h_ref, p_ref, slab, j0, j1, h, p):
        r0 = pl.ds(pl.multiple_of(j0 * SUBLANES, SUBLANES), SUBLANES)
        r1 = pl.ds(pl.multiple_of(j1 * SUBLANES, SUBLANES), SUBLANES)
        a0, u0 = a_ref[slab, r0, :], u_ref[slab, r0, :]
        a1, u1 = a_ref[slab, r1, :], u_ref[slab, r1, :]
        a01 = a1 * a0
        u01 = a1 * u0 + u1
        h_ref[slab, r0, :] = a0 * h + u0
        p_ref[slab, r0, :] = a0 * p
        h = a01 * h + u01
        p = a01 * p
        h_ref[slab, r1, :] = h
        p_ref[slab, r1, :] = p
        return h, p

    def scan_pair(i, carry):
        out = []
        for slab in range(nslab):
            hf, pf, hb, pb = carry[slab]
            hf, pf = two_steps(af_ref, uf_ref, hf_ref, pf_ref, slab, 2 * i, 2 * i + 1, hf, pf)
            hb, pb = two_steps(ab_ref, ub_ref, hb_ref, pb_ref, slab,
                               SEG - 1 - 2 * i, SEG - 2 - 2 * i, hb, pb)
            out.append((hf, pf, hb, pb))
        return tuple(out)

    zero = jnp.zeros((SUBLANES, LANES), F32)
    one = jnp.ones((SUBLANES, LANES), F32)
    final = lax.fori_loop(0, SEG // 2, scan_pair, tuple((zero, one, zero, one) for _ in range(nslab)),
                          unroll=LRU_SCAN_UNROLL)

    for slab in range(nslab):
        hf, pf, hb, pb = final[slab]
        c = jnp.zeros((1, LANES), F32)
        fwd = [c]
        for sgm in range(1, nseg):
            c = hf[sgm - 1:sgm, :] + pf[sgm - 1:sgm, :] * c
            fwd.append(c)
        c = jnp.zeros((1, LANES), F32)
        bwd = [c]
        for sgm in range(nseg - 2, -1, -1):
            c = hb[sgm + 1:sgm + 2, :] + pb[sgm + 1:sgm + 2, :] * c
            bwd.append(c)
        cf_ref[slab] = jnp.concatenate(fwd, axis=0)
        cbk_ref[slab] = jnp.concatenate(bwd[::-1], axis=0)

    reps = LRU_COMBINE_ROWS // SUBLANES

    def combine(i, carry):
        rows = pl.ds(pl.multiple_of(i * LRU_COMBINE_ROWS, LRU_COMBINE_ROWS), LRU_COMBINE_ROWS)
        for slab in range(nslab):
            cf = jnp.concatenate([cf_ref[slab]] * reps, axis=0)
            cbk = jnp.concatenate([cbk_ref[slab]] * reps, axis=0)
            uf_ref[slab, rows, :] = ((hf_ref[slab, rows, :] + pf_ref[slab, rows, :] * cf)
                                     + (hb_ref[slab, rows, :] + pb_ref[slab, rows, :] * cbk))
        return carry

    lax.fori_loop(0, S // LRU_COMBINE_ROWS, combine, 0)

    def out_block(s, carry):
        start = pl.multiple_of(s * SEG, SEG)
        for slab in range(nslab):
            lanes = slice(slab * LANES, (slab + 1) * LANES)
            h = uf_ref[slab, pl.ds(s, SEG, stride=SUBLANES), :]
            o_ref[pl.ds(start, SEG), lanes] = h * _gelu_tanh(y_ref[pl.ds(start, SEG), lanes])
        return carry

    lax.fori_loop(0, nseg, out_block, 0)


def _rg_lru(xl, yl, B, S, conv_w, conv_b, wg, bg, lam):
    C = LRU_CH
    ncg = D_LRU // C
    nslab = C // LANES
    xl3 = xl.reshape(B, S, D_LRU)
    yl3 = yl.reshape(B, S, D_LRU)
    seg_buf = pltpu.VMEM((nslab, S, LANES), F32)
    carry_buf = pltpu.VMEM((nslab, S // SEG, LANES), F32)
    out = pl.pallas_call(
        _rg_lru_kernel,
        grid=(B, ncg),
        in_specs=[
            pl.BlockSpec((None, S, C), lambda b, c: (b, 0, c)),
            pl.BlockSpec((None, S, C), lambda b, c: (b, 0, c)),
            pl.BlockSpec((CONV_WIDTH, C), lambda b, c: (0, c)),
            pl.BlockSpec((1, C), lambda b, c: (0, c)),
            pl.BlockSpec((None, C, 4 * C), lambda b, c: (c, 0, 0)),
            pl.BlockSpec((None, 1, 4 * C), lambda b, c: (c, 0, 0)),
            pl.BlockSpec((2, C), lambda b, c: (0, c)),
        ],
        out_specs=pl.BlockSpec((None, S, C), lambda b, c: (b, 0, c)),
        out_shape=jax.ShapeDtypeStruct((B, S, D_LRU), F32),
        scratch_shapes=[seg_buf] * 8 + [carry_buf] * 2,
        compiler_params=pltpu.CompilerParams(
            dimension_semantics=("parallel", "parallel"), vmem_limit_bytes=VMEM_LIMIT),
        name="rg_lru",
    )(xl3, yl3, conv_w, conv_b, wg, bg, lam)
    return out.reshape(B * S, D_LRU)


ROUTER_ROWS = 24


def _router_gates_t(logits):
    row = lax.broadcasted_iota(jnp.int32, logits.shape, 0).astype(F32)
    neg = -jnp.inf
    big = float(2 * LANES)
    gl = jnp.where(row < N_GROUPS, logits, neg)
    gmax = jnp.max(gl, axis=0, keepdims=True)
    gidx = jnp.min(jnp.where(gl == gmax, row, big), axis=0, keepdims=True)
    g_sel_prob = 1.0 / jnp.sum(jnp.exp(gl - gmax), axis=0, keepdims=True)
    lo = EXPERT_LANE0 + EXPERTS_PER_GROUP * gidx
    el = jnp.where((row >= lo) & (row < lo + EXPERTS_PER_GROUP), logits, neg)
    v1 = jnp.max(el, axis=0, keepdims=True)
    i1 = jnp.min(jnp.where(el == v1, row, big), axis=0, keepdims=True)
    el2 = jnp.where(row == i1, neg, el)
    v2 = jnp.max(el2, axis=0, keepdims=True)
    i2 = jnp.min(jnp.where(el2 == v2, row, big), axis=0, keepdims=True)
    e2 = jnp.exp(v2 - v1)
    w1 = g_sel_prob / (1.0 + e2)
    w2 = g_sel_prob * e2 / (1.0 + e2)
    gates = jnp.where(row == i1, w1, jnp.where(row == i2, w2, 0.0))
    return jnp.where(row == 0.0, gidx, gates)


def _out_proj_kernel(x_ref, st_ref, attn_ref, lru_ref, lng_ref, lnb_ref, ag_ref, lg_ref, wo_ref,
                     g1_ref, b1_ref, wr_ref, br_ref, h1_ref, gate_ref):
    h = (x_ref[...] - st_ref[:, 0:1]) * st_ref[:, 1:2] * lng_ref[...] + lnb_ref[...]
    merged = jnp.concatenate(
        [_rms_norm_rows(attn_ref[...], ag_ref[...], LN_EPS),
         _rms_norm_rows(lru_ref[...], lg_ref[...], LN_EPS)], axis=1).astype(BF16)
    mix = jnp.dot(merged, wo_ref[...], preferred_element_type=F32)
    h1 = _layer_norm(DEEPNORM_ALPHA * h + mix, g1_ref[...], b1_ref[...])
    h1_ref[...] = h1
    hi = h1.astype(BF16)
    lo = (h1 - hi.astype(F32)).astype(BF16)
    nt = (((1,), (1,)), ((), ()))
    t_hi = lax.dot_general(wr_ref[...], hi, nt, preferred_element_type=F32)
    t_lo = lax.dot_general(wr_ref[:ROUTER_PAD, :], lo, nt, preferred_element_type=F32)
    logits_t = (t_hi[:ROUTER_ROWS] + t_hi[ROUTER_PAD:ROUTER_PAD + ROUTER_ROWS]
                + t_lo[:ROUTER_ROWS] + br_ref[...])
    route_t = _router_gates_t(logits_t)
    pad = jnp.zeros((ROUTER_PAD - ROUTER_ROWS, route_t.shape[1]), F32)
    gate_ref[...] = jnp.concatenate([route_t, pad], axis=0).T


def _out_proj(x2, stats, attn, lru, ln_g, ln_b, ag, lg, wo, g1, b1, wr, br):
    T = x2.shape[0]
    tm = TM_OUT
    const = lambda i: (0, 0)
    row = lambda i: (i, 0)
    return pl.pallas_call(
        _out_proj_kernel,
        grid=(T // tm,),
        in_specs=[
            pl.BlockSpec((tm, D_MODEL), row),
            pl.BlockSpec((tm, LANES), row),
            pl.BlockSpec((tm, D_ATTN), row),
            pl.BlockSpec((tm, D_LRU), row),
            pl.BlockSpec((1, D_MODEL), const),
            pl.BlockSpec((1, D_MODEL), const),
            pl.BlockSpec((1, D_ATTN), const),
            pl.BlockSpec((1, D_LRU), const),
            pl.BlockSpec((D_MODEL, D_MODEL), const),
            pl.BlockSpec((1, D_MODEL), const),
            pl.BlockSpec((1, D_MODEL), const),
            pl.BlockSpec((2 * ROUTER_PAD, D_MODEL), const),
            pl.BlockSpec((ROUTER_ROWS, 1), const),
        ],
        out_specs=[pl.BlockSpec((tm, D_MODEL), row), pl.BlockSpec((tm, ROUTER_PAD), row)],
        out_shape=[jax.ShapeDtypeStruct((T, D_MODEL), F32),
                   jax.ShapeDtypeStruct((T, ROUTER_PAD), F32)],
        compiler_params=pltpu.CompilerParams(
            dimension_semantics=("parallel",), vmem_limit_bytes=VMEM_LIMIT),
        name="out_proj",
    )(x2, stats, attn, lru, ln_g, ln_b, ag, lg, wo, g1, b1, wr, br)


def _lane_scalar(row, idx):
    lane = lax.broadcasted_iota(jnp.int32, row.shape, 1)
    return jnp.sum(jnp.where(lane == idx, row, 0.0))


def _moe_kernel(x_ref, st_ref, attn_ref, lru_ref, lng_ref, lnb_ref, ag_ref, lg_ref, wo_ref,
                g1_ref, b1_ref, wr_ref, br_ref, tril_ref, w13_ref, w2_ref, g2_ref, b2_ref, o_ref,
                h1_ref, route_ref, pt_ref, xs_ref, gs_ref, ys_ref):
    _out_proj_kernel(x_ref, st_ref, attn_ref, lru_ref, lng_ref, lnb_ref, ag_ref, lg_ref, wo_ref,
                     g1_ref, b1_ref, wr_ref, br_ref, h1_ref, route_ref)
    _moe_tile(h1_ref, route_ref, tril_ref, w13_ref, w2_ref, g2_ref, b2_ref,
              o_ref, pt_ref.at[0], xs_ref.at[0], gs_ref.at[0], ys_ref.at[0])


def _moe_tile(h1_ref, route_ref, tril_ref, w13_ref, w2_ref, g2_ref, b2_ref, o_ref,
              pt_ref, xs_ref, gs_ref, ys_ref):
    tm = h1_ref.shape[0]
    rs = xs_ref.shape[0]
    c = MOE_CHUNK
    h1 = h1_ref[...]
    route = route_ref[...]
    lane = lax.broadcasted_iota(jnp.int32, route.shape, 1)
    gid = jnp.sum(jnp.where(lane == 0, route, 0.0), axis=1, keepdims=True)
    onehot = jnp.where(lane.astype(F32) == gid, 1.0, 0.0)
    half = tm // 2
    ohb = onehot.astype(BF16)
    tril = tril_ref[...]
    cnt_top = jnp.sum(onehot[:half], axis=0, keepdims=True)
    before = jnp.concatenate(
        [jnp.dot(tril, ohb[:half], preferred_element_type=F32),
         jnp.dot(tril, ohb[half:], preferred_element_type=F32) + cnt_top], axis=0)
    cnt_row = cnt_top + jnp.sum(onehot[half:], axis=0, keepdims=True)
    cnt = [_lane_scalar(cnt_row, g).astype(jnp.int32) for g in range(N_GROUPS)]
    begin = [jnp.int32(0)]
    for g in range(1, N_GROUPS):
        end = begin[g - 1] + cnt[g - 1] + (MOE_ALIGN - 1)
        shift = jnp.int32(MOE_ALIGN_LOG2)
        begin.append(lax.shift_left(lax.shift_right_logical(end, shift), shift))
    lane1 = lax.broadcasted_iota(jnp.int32, cnt_row.shape, 1)
    begin_row = sum(jnp.where(lane1 == g, begin[g].astype(F32), 0.0) for g in range(N_GROUPS))
    dest = jnp.sum(onehot * (before + begin_row), axis=1, keepdims=True)
    col = lax.broadcasted_iota(jnp.int32, (tm, rs), 1).astype(F32)
    pt_ref[...] = jnp.where(col == dest, 1.0, 0.0).astype(BF16)
    dest_row = jnp.broadcast_to(dest, (tm, LANES)).T[0:1, :]
    row = lax.broadcasted_iota(jnp.int32, (rs, tm), 0).astype(F32)
    p = jnp.where(row == dest_row, 1.0, 0.0).astype(BF16)

    xs_ref[...] = jnp.dot(p, h1.astype(BF16), preferred_element_type=F32).astype(BF16)
    rhi = route.astype(BF16)
    rlo = (route - rhi.astype(F32)).astype(BF16)
    gs2 = jnp.dot(p, jnp.concatenate([rhi, rlo], axis=1), preferred_element_type=F32)
    gs_ref[...] = gs2[:, :LANES] + gs2[:, LANES:]
    ys_ref[...] = jnp.zeros_like(ys_ref)

    def run_chunk(g, first_row):
        start = jnp.minimum(first_row, rs - c)
        rows = pl.ds(pl.multiple_of(start, MOE_ALIGN), c)
        xj = xs_ref[rows, :]
        row_c = lax.broadcasted_iota(jnp.int32, (c, LANES), 0)
        gsj = jnp.where(row_c >= first_row - start, gs_ref[rows, :], 0.0)
        lane_c = lax.broadcasted_iota(jnp.int32, gsj.shape, 1)
        hids = []
        for k in range(EXPERTS_PER_GROUP):
            e = g * EXPERTS_PER_GROUP + k
            up = jnp.dot(xj, w13_ref[e], preferred_element_type=F32)
            u1 = up[:, :D_EXPERT]
            ge = jnp.sum(jnp.where(lane_c == EXPERT_LANE0 + e, gsj, 0.0), axis=1, keepdims=True)
            hids.append(((u1 * jax.nn.sigmoid(u1)) * up[:, D_EXPERT:] * ge).astype(BF16))
        ys_ref[rows, :] += jnp.dot(jnp.concatenate(hids, axis=1), w2_ref[g],
                                   preferred_element_type=F32)

    for g in range(N_GROUPS):
        run_chunk(g, begin[g])

    extra = [(g, k) for g in range(N_GROUPS) for k in range(1, N_GROUPS)]
    n_extra = sum((cnt[g] > k * c).astype(jnp.int32) for g, k in extra)

    def extra_chunk(s, carry):
        seen = jnp.int32(0)
        g_sel = jnp.int32(0)
        k_sel = jnp.int32(0)
        for g, k in extra:
            active = cnt[g] > k * c
            hit = active & (seen == s)
            g_sel = jnp.where(hit, g, g_sel)
            k_sel = jnp.where(hit, k, k_sel)
            seen = seen + active.astype(jnp.int32)
        first = sum(jnp.where(g_sel == g, begin[g], 0) for g in range(N_GROUPS)) + k_sel * c
        run_chunk(g_sel, first)
        return carry

    lax.fori_loop(0, n_extra, extra_chunk, 0)

    ffn = jnp.dot(pt_ref[...], ys_ref[...].astype(BF16), preferred_element_type=F32)
    o_ref[...] = _layer_norm(DEEPNORM_ALPHA * h1 + ffn, g2_ref[...], b2_ref[...])


def _moe(x2, stats, attn, lru, ln_g, ln_b, ag, lg, wo, g1, b1, wr, br, tril, w13, w2, g2, b2):
    T = x2.shape[0]
    tm = TM_MOE
    assert tm <= MOE_CHUNK * N_GROUPS and MOE_CHUNK % MOE_ALIGN == 0
    rs = -(-(tm + (N_GROUPS - 1) * (MOE_ALIGN - 1)) // MOE_ALIGN) * MOE_ALIGN
    gw = EXPERTS_PER_GROUP * D_EXPERT
    row = lambda i: (i, 0)
    const2 = lambda i: (0, 0)
    const3 = lambda i: (0, 0, 0)
    resident = pl.Buffered(1)
    return pl.pallas_call(
        _moe_kernel,
        grid=(T // tm,),
        in_specs=[
            pl.BlockSpec((tm, D_MODEL), row),
            pl.BlockSpec((tm, LANES), row),
            pl.BlockSpec((tm, D_ATTN), row),
            pl.BlockSpec((tm, D_LRU), row),
            pl.BlockSpec((1, D_MODEL), const2),
            pl.BlockSpec((1, D_MODEL), const2),
            pl.BlockSpec((1, D_ATTN), const2),
            pl.BlockSpec((1, D_LRU), const2),
            pl.BlockSpec((D_MODEL, D_MODEL), const2, pipeline_mode=resident),
            pl.BlockSpec((1, D_MODEL), const2),
            pl.BlockSpec((1, D_MODEL), const2),
            pl.BlockSpec((2 * ROUTER_PAD, D_MODEL), const2, pipeline_mode=resident),
            pl.BlockSpec((ROUTER_ROWS, 1), const2),
            pl.BlockSpec((tm // 2, tm // 2), const2, pipeline_mode=resident),
            pl.BlockSpec((N_EXPERTS, D_MODEL, 2 * D_EXPERT), const3, pipeline_mode=resident),
            pl.BlockSpec((N_GROUPS, gw, D_MODEL), const3, pipeline_mode=resident),
            pl.BlockSpec((1, D_MODEL), const2),
            pl.BlockSpec((1, D_MODEL), const2),
        ],
        out_specs=pl.BlockSpec((tm, D_MODEL), row),
        out_shape=jax.ShapeDtypeStruct((T, D_MODEL), F32),
        scratch_shapes=[pltpu.VMEM((tm, D_MODEL), F32), pltpu.VMEM((tm, ROUTER_PAD), F32),
                        pltpu.VMEM((1, tm, rs), BF16), pltpu.VMEM((1, rs, D_MODEL), BF16),
                        pltpu.VMEM((1, rs, ROUTER_PAD), F32), pltpu.VMEM((1, rs, D_MODEL), F32)],
        compiler_params=pltpu.CompilerParams(
            dimension_semantics=("parallel",), vmem_limit_bytes=MOE_VMEM_LIMIT),
        name="moe",
    )(x2, stats, attn, lru, ln_g, ln_b, ag, lg, wo, g1, b1, wr, br, tril, w13, w2, g2, b2)


def _rope_tables(S):
    t = jnp.arange(S, dtype=jnp.int32)
    row = (t // GRID_W).astype(F32)
    col = (t % GRID_W).astype(F32)
    half = ROPE_SECTION // 2
    inv_freq = ROPE_THETA ** (-jnp.arange(half, dtype=F32) / half)
    ang_r = row[:, None] * inv_freq
    ang_c = col[:, None] * inv_freq
    cos_h = jnp.concatenate([jnp.cos(ang_r)] * 2 + [jnp.cos(ang_c)] * 2, axis=1)
    sin_h = jnp.concatenate([-jnp.sin(ang_r), jnp.sin(ang_r), -jnp.sin(ang_c), jnp.sin(ang_c)], axis=1)
    reps = KV_WIDTH // HEAD_DIM
    return jnp.tile(cos_h, (1, reps)), jnp.tile(sin_h, (1, reps))


def _block_diag(w):
    H, d, _ = w.shape
    eye = jnp.eye(H, dtype=w.dtype)
    return (eye[:, None, :, None] * w[:, :, None, :]).reshape(H * d, H * d)


def _lru_gate_weights(wa, ba, wx, bx):
    C = LRU_CH
    ncg = D_LRU // C
    hpg = C // LRU_HEAD_DIM
    ws, bs = [], []
    for c in range(ncg):
        hs = slice(c * hpg, (c + 1) * hpg)
        cols, bias = [], []
        for d in range(2):
            cols += [_block_diag(wa[d, hs]), _block_diag(wx[d, hs])]
            bias += [ba[d, hs].reshape(C), bx[d, hs].reshape(C)]
        ws.append(jnp.concatenate(cols, axis=1))
        bs.append(jnp.concatenate(bias)[None, :])
    return (-LOG2_E * jnp.stack(ws)).astype(BF16), -LOG2_E * jnp.stack(bs)


def _trunk(x, P):
    B, S, _ = x.shape
    x2 = x.reshape(B * S, D_MODEL)
    q, k, v, xl, yl, stats = _in_proj(x2, B, S, P["ln_in_g"], P["ln_in_b"], P["wq"], P["wkv"], P["wx"],
                               P["wy"], P["seg"], P["qg"], P["kg"], P["cos"], P["sin"])
    attn = _attention(q, k, v, B, S)
    lru = _rg_lru(xl, yl, B, S, P["conv_w"], P["conv_b"], P["wg"], P["bg"], P["lam"])
    out = _moe(x2, stats, attn, lru, P["ln_in_g"], P["ln_in_b"], P["ag"], P["lg"], P["wo"],
               P["g1"], P["b1"], P["wr"], P["br"], P["tril"], P["w13"], P["w2"], P["g2"], P["b2"])
    return out.reshape(B, S, D_MODEL)


def kernel(x_prompt, x_sample, ln_in_g, ln_in_b, w_in, conv_w, conv_b, lru_wa, lru_ba, lru_wx, lru_bx, lru_lambda, q_norm_g, k_norm_g, attn_out_g, lru_out_g, w_out, ln1_g, ln1_b, router_wg, router_bg, router_we, router_be, exp_w1, exp_w3, exp_w2, ln2_g, ln2_b):
    assert w_in.shape[0] == DEPTH == 1
    S = x_prompt.shape[1]
    l = 0
    w = w_in[l]
    c0, c1, c2 = D_ATTN, D_ATTN + 2 * KV_WIDTH, D_ATTN + 2 * KV_WIDTH + D_LRU
    seg = _block_diag(jnp.full((N_Q_HEADS, HEAD_DIM, HEAD_DIM), 1.0 / HEAD_DIM, F32)).astype(BF16)
    cos_t, sin_t = _rope_tables(S)
    wg, bg = _lru_gate_weights(lru_wa[l], lru_ba[l], lru_wx[l], lru_bx[l])
    wr = jnp.concatenate([router_wg[l], router_we[l].reshape(D_MODEL, N_EXPERTS)], axis=1)
    wr = jnp.pad(wr, ((0, 0), (0, ROUTER_PAD - wr.shape[1])))
    wrh = wr.astype(BF16)
    wrl = (wr - wrh.astype(F32)).astype(BF16)
    wr_t = jnp.concatenate([wrh.T, wrl.T], axis=0)
    br = jnp.concatenate([router_bg[l], router_be[l].reshape(N_EXPERTS)])
    br = jnp.pad(br, (0, ROUTER_ROWS - br.shape[0]))[:, None]
    P = dict(
        ln_in_g=ln_in_g[None, :], ln_in_b=ln_in_b[None, :],
        wq=w[:, :c0].astype(BF16), wkv=w[:, c0:c1].astype(BF16),
        wx=w[:, c1:c2].astype(BF16), wy=w[:, c2:].astype(BF16),
        seg=seg, qg=jnp.tile(q_norm_g[l], N_Q_HEADS)[None, :],
        kg=jnp.tile(k_norm_g[l], N_KV_HEADS)[None, :], cos=cos_t, sin=sin_t,
        conv_w=conv_w[l], conv_b=conv_b[l][None, :], wg=wg, bg=bg, lam=lru_lambda[l],
        ag=attn_out_g[l][None, :], lg=lru_out_g[l][None, :], wo=w_out[l].astype(BF16),
        g1=ln1_g[l][None, :], b1=ln1_b[l][None, :], wr=wr_t, br=br,
        w13=jnp.concatenate([exp_w1[l], exp_w3[l]], axis=2).astype(BF16),
        w2=exp_w2[l].reshape(N_GROUPS, EXPERTS_PER_GROUP * D_EXPERT, D_MODEL).astype(BF16),
        g2=ln2_g[l][None, :], b2=ln2_b[l][None, :],
        tril=jnp.tril(jnp.ones((TM_MOE // 2, TM_MOE // 2), F32), -1).astype(BF16),
    )
    return (_trunk(x_prompt, P), _trunk(x_sample, P))
```
